```python
import math
import jax, jax.numpy as jnp
from jax import lax
import numpy as np

D_MODEL = 1024
BATCH = 32
SEQ = 256
DEPTH = 2
DEC_BATCH = 4
DEC_SEQ = 4096
PAST_LEN = 512

GRID_W = 64
ATT_HEADS = 8
ATT_QK_DIM = 64
ATT_V_DIM = 2 * ATT_QK_DIM
ATT_QK_WIDTH = ATT_HEADS * 2 * ATT_QK_DIM
ATT_WIDTH = ATT_HEADS * ATT_V_DIM
ROPE_BASE = 10000.0
ROPE_AXIS_DIM = ATT_QK_DIM // 2
Q_BLOCK = 128
CONV_WIDTH = D_MODEL
CONV_K = 3
REC_HEADS = 8
REC_DK = 128
REC_DV = D_MODEL // REC_HEADS
REC_KW = REC_HEADS * REC_DK
REC_WIDTH = REC_HEADS * REC_DV
REC_CHUNK = 32
N_BRANCH = 3
PEER_HEADS = 8
PEER_N_KEYS = 128
PEER_N_EXPERTS = PEER_N_KEYS * PEER_N_KEYS
PEER_KEY_DIM = 256
PEER_HALF = PEER_KEY_DIM // 2
PEER_TOPK = 16
PEER_BLOCK = 128
LN_EPS = 1e-5
DEEPNORM_ALPHA = (2 * DEPTH) ** 0.25
DEEPNORM_BETA = (8 * DEPTH) ** -0.25

IN_SIZES = (ATT_QK_WIDTH, ATT_QK_WIDTH, ATT_WIDTH,
            CONV_WIDTH, CONV_WIDTH, CONV_WIDTH,
            REC_KW, REC_KW, REC_KW, REC_WIDTH, REC_WIDTH,
            N_BRANCH * D_MODEL)
W_IN_COLS = sum(IN_SIZES)

kernel_name = 'hybrid_diffusion_trunk_step'


def layer_norm(x, g, b):
    xf = x.astype(jnp.float32)
    mu = jnp.mean(xf, axis=-1, keepdims=True)
    var = jnp.mean(jnp.square(xf - mu), axis=-1, keepdims=True)
    return ((xf - mu) * lax.rsqrt(var + LN_EPS) * g.astype(jnp.float32) + b.astype(jnp.float32)).astype(x.dtype)


def rms_norm(x, g):
    xf = x.astype(jnp.float32)
    y = xf * lax.rsqrt(jnp.mean(jnp.square(xf), axis=-1, keepdims=True) + LN_EPS)
    return (y * g.astype(jnp.float32)).astype(x.dtype)


def split_cols(z):
    outs, start = [], 0
    for n in IN_SIZES:
        outs.append(z[..., start:start + n])
        start += n
    return outs


def rope_axial(x):
    T = x.shape[1]
    n_rows = T // GRID_W
    rows = jnp.repeat(jnp.arange(n_rows), GRID_W)
    cols = jnp.tile(jnp.arange(GRID_W), n_rows)
    freqs = ROPE_BASE ** (-jnp.arange(0, ROPE_AXIS_DIM, 2, dtype=jnp.float32) / ROPE_AXIS_DIM)

    def rot(xa, pos):
        ang = pos.astype(jnp.float32)[:, None] * freqs[None, :]
        cos, sin = jnp.cos(ang)[None, :, None, :], jnp.sin(ang)[None, :, None, :]
        xa = xa.astype(jnp.float32)
        x1, x2 = xa[..., :ROPE_AXIS_DIM // 2], xa[..., ROPE_AXIS_DIM // 2:]
        return jnp.concatenate([x1 * cos - x2 * sin, x1 * sin + x2 * cos], axis=-1)

    out = jnp.concatenate([rot(x[..., :ROPE_AXIS_DIM], rows), rot(x[..., ROPE_AXIS_DIM:], cols)], axis=-1)
    return out.astype(x.dtype)


def diff_attention(q1, q2, k1, k2, v, lam):
    B, T, H, dq = q1.shape
    nb = T // Q_BLOCK
    scale = dq ** -0.5

    def blocks(a):
        return a.reshape(B, nb, Q_BLOCK, H, dq).transpose(1, 0, 2, 3, 4)

    def one(qs):
        a1, a2 = qs
        s1 = jnp.einsum('bqhd,bkhd->bhqk', a1, k1).astype(jnp.float32) * scale
        s2 = jnp.einsum('bqhd,bkhd->bhqk', a2, k2).astype(jnp.float32) * scale
        pmap = jax.nn.softmax(s1, axis=-1) - lam * jax.nn.softmax(s2, axis=-1)
        return jnp.einsum('bhqk,bkhv->bqhv', pmap.astype(v.dtype), v)

    o = lax.map(one, (blocks(q1), blocks(q2)))
    return o.transpose(1, 0, 2, 3, 4).reshape(B, T, H, v.shape[-1])


def short_conv(u, w, b):
    up = jnp.pad(u, ((0, 0), (1, 1), (0, 0)))
    return up[:, :-2] * w[0] + up[:, 1:-1] * w[1] + up[:, 2:] * w[2] + b


def forget_log(fx, lb, shape):
    f = lb + (1.0 - lb) * jax.nn.sigmoid(fx.astype(jnp.float32))
    return jnp.log(f).reshape(shape)


def hgrn2_scan(q, logf, inp, s0):
    B, T, H, DK = q.shape
    DV = inp.shape[-1]
    n = T // REC_CHUNK

    def chunks(a):
        return a.astype(jnp.float32).reshape(B, n, REC_CHUNK, H, a.shape[-1]).transpose(1, 0, 3, 2, 4)

    k = -jnp.expm1(logf)
    mask = jnp.tril(jnp.ones((REC_CHUNK, REC_CHUNK), dtype=bool))

    def step(S, xs):
        qc, lc, kc, ic = xs
        b = jnp.cumsum(lc, axis=2)
        rel = jnp.where(mask[:, :, None], b[:, :, :, None, :] - b[:, :, None, :, :], -jnp.inf)
        A = jnp.einsum('bhtk,bhtsk,bhsk->bhts', qc, jnp.exp(rel), kc)
        o = jnp.einsum('bhts,bhsv->bhtv', A, ic) + jnp.einsum('bhtk,bhkv->bhtv', qc * jnp.exp(b), S)
        bl = b[:, :, -1:, :]
        S = jnp.exp(bl[:, :, 0, :])[..., None] * S + jnp.einsum('bhsk,bhsv->bhkv', kc * jnp.exp(bl - b), ic)
        return S, o

    S, o = lax.scan(step, s0.astype(jnp.float32), (chunks(q), chunks(logf), chunks(k), chunks(inp)))
    o = o.transpose(1, 0, 3, 2, 4).reshape(B, T, H, DV)
    return o, S


def token_mixer(h, l, p, ctx):
    B, T, _ = h.shape
    zq, zk, zv, zb, zc, zx, rq, rff, rfb, ri, rg, zg = split_cols(h @ p['w_in'])
    q = zq.reshape(B, T, ATT_HEADS, 2, ATT_QK_DIM)
    q1, q2 = q[..., 0, :], q[..., 1, :]
    k = zk.reshape(B, T, ATT_HEADS, 2 * ATT_QK_DIM)
    v = zv.reshape(B, T, ATT_HEADS, ATT_V_DIM)
    lam_init = 0.8 - 0.6 * math.exp(-0.3 * l)
    lp = p['attn_lambda'].astype(jnp.float32)
    lam = jnp.exp(jnp.sum(lp[0] * lp[1])) - jnp.exp(jnp.sum(lp[2] * lp[3])) + lam_init
    q_r = jax.nn.silu(rq).reshape(B, T, REC_HEADS, REC_DK)
    i_r = ri.reshape(B, T, REC_HEADS, REC_DV)
    logf_f = forget_log(rff, p['lb'][0], (B, T, REC_HEADS, REC_DK))
    logf_b = forget_log(rfb, p['lb'][1], (B, T, REC_HEADS, REC_DK))
    if ctx is None:
        k1, k2, vals = k[..., :ATT_QK_DIM], k[..., ATT_QK_DIM:], v
        s0 = jnp.zeros((B, 2, REC_HEADS, REC_DK, REC_DV), jnp.float32)
    else:
        k_ctx, v_ctx, s_ctx = ctx
        k_ctx = k_ctx.astype(k.dtype)
        q1, q2 = rope_axial(q1), rope_axial(q2)
        k1 = jnp.concatenate([rope_axial(k[..., :ATT_QK_DIM]), k_ctx[..., :ATT_QK_DIM]], axis=1)
        k2 = jnp.concatenate([rope_axial(k[..., ATT_QK_DIM:]), k_ctx[..., ATT_QK_DIM:]], axis=1)
        vals = jnp.concatenate([v, v_ctx.astype(v.dtype)], axis=1)
        s0 = s_ctx
    o_att = diff_attention(q1, q2, k1, k2, vals, lam)
    o_att = (rms_norm(o_att, p['attn_subln']) * (1.0 - lam_init)).reshape(B, T, ATT_WIDTH).astype(h.dtype)
    o_conv = zb * short_conv(zc * zx, p['conv_w'], p['conv_b'])
    o_f, s_f = hgrn2_scan(q_r, logf_f, i_r, s0[:, 0])
    o_b, s_b = hgrn2_scan(q_r[:, ::-1], logf_b[:, ::-1], i_r[:, ::-1], s0[:, 1])
    g_r = jax.nn.silu(rg.reshape(B, T, REC_HEADS, REC_DV).astype(jnp.float32))
    o_rec = (rms_norm(o_f + o_b[:, ::-1], p['rec_norm']) * g_r).reshape(B, T, REC_WIDTH).astype(h.dtype)
    gates = jax.nn.sigmoid(zg.reshape(B, T, N_BRANCH, D_MODEL))
    wb = p['w_branch']
    merged = (gates[..., 0, :] * (o_att @ wb[0]) + gates[..., 1, :] * (o_conv @ wb[1])
              + gates[..., 2, :] * (o_rec @ wb[2]))
    out = merged @ p['w_out']
    new_ctx = (k, v, jnp.stack([s_f, s_b], axis=1)) if ctx is None else None
    return out, new_ctx


def peer(h, wq, keys, u_tab, v_tab):
    B, T, D = h.shape
    nb = (B * T) // PEER_BLOCK

    def one(xb):
        q = (xb @ wq).reshape(PEER_BLOCK, PEER_HEADS, 2, PEER_HALF)
        s = jnp.einsum('thpd,hpnd->thpn', q, keys).astype(jnp.float32)
        s_top, i_top = lax.top_k(s, PEER_TOPK)
        cand = s_top[:, :, 0, :, None] + s_top[:, :, 1, None, :]
        cand_idx = i_top[:, :, 0, :, None] * PEER_N_KEYS + i_top[:, :, 1, None, :]
        best, pos = lax.top_k(cand.reshape(PEER_BLOCK, PEER_HEADS, PEER_TOPK * PEER_TOPK), PEER_TOPK)
        idx = jnp.take_along_axis(cand_idx.reshape(PEER_BLOCK, PEER_HEADS, PEER_TOPK * PEER_TOPK), pos, axis=-1)
        g = jax.nn.softmax(best, axis=-1)
        ue = jnp.take(u_tab, idx, axis=0)
        ve = jnp.take(v_tab, idx, axis=0)
        act = jax.nn.gelu(jnp.einsum('td,thkd->thk', xb, ue).astype(jnp.float32), approximate=False)
        return jnp.einsum('thk,thkd->td', (g * act).astype(ve.dtype), ve)

    y = lax.map(one, h.reshape(nb, PEER_BLOCK, D))
    return y.reshape(B, T, D)


def trunk_layer(x, mod, l, p, ctx):
    sh_a, sc_a, g_a, sh_f, sc_f, g_f = jnp.split(mod.astype(x.dtype), 6, axis=-1)
    h = x * (1.0 + sc_a) + sh_a
    mix, new_ctx = token_mixer(h, l, p, ctx)
    x = layer_norm(DEEPNORM_ALPHA * x + g_a * mix, p['ln_g'][0], p['ln_b'][0])
    h = x * (1.0 + sc_f) + sh_f
    ff = peer(h, p['peer_wq'], p['peer_keys'], p['peer_u'], p['peer_v'])
    x = layer_norm(DEEPNORM_ALPHA * x + g_f * ff, p['ln_g'][1], p['ln_b'][1])
    return x, new_ctx


def setup_inputs(seed: int = 0) -> dict:
    key = jax.random.key(seed)
    ks = jax.random.split(key, 24)

    def nrm(k, shape, s):
        return jax.random.normal(k, shape, jnp.float32) * s

    d_inv = D_MODEL ** -0.5
    return {
        'x_prompt': nrm(ks[0], (BATCH, SEQ, D_MODEL), 1.0),
        'x_sample': nrm(ks[1], (DEC_BATCH, DEC_SEQ, D_MODEL), 1.0),
        'c': nrm(ks[2], (DEC_BATCH, D_MODEL), 1.0),
        'cache_attn_k': nrm(ks[3], (DEC_BATCH, DEPTH, PAST_LEN, ATT_HEADS, 2 * ATT_QK_DIM), 1.0),
        'cache_attn_v': nrm(ks[4], (DEC_BATCH, DEPTH, PAST_LEN, ATT_HEADS, ATT_V_DIM), 1.0),
        'state_hgrn': nrm(ks[5], (DEC_BATCH, DEPTH, 2, REC_HEADS, REC_DK, REC_DV), 0.3),
        'c_ctx': nrm(ks[6], (D_MODEL,), 1.0),
        'mod_w': nrm(ks[7], (DEPTH, D_MODEL, 6 * D_MODEL), 0.5 * d_inv),
        'mod_b': nrm(ks[8], (DEPTH, 6 * D_MODEL), 0.01),
        'w_in': nrm(ks[9], (DEPTH, D_MODEL, W_IN_COLS), d_inv),
        'attn_lambda': nrm(ks[10], (DEPTH, 4, ATT_QK_DIM), 0.1),
        'attn_subln': 1.0 + nrm(ks[11], (DEPTH, ATT_V_DIM), 0.02),
        'conv_w': nrm(ks[12], (DEPTH, CONV_K, CONV_WIDTH), 0.5),
        'conv_b': nrm(ks[13], (DEPTH, CONV_WIDTH), 0.01),
        'rec_lb': nrm(ks[14], (DEPTH, 2, REC_KW), 1.0),
        'rec_norm': 1.0 + nrm(ks[15], (DEPTH, REC_DV), 0.02),
        'w_branch': nrm(ks[16], (DEPTH, N_BRANCH, D_MODEL, D_MODEL), d_inv),
        'w_out': nrm(ks[17], (DEPTH, D_MODEL, D_MODEL), DEEPNORM_BETA * d_inv),
        'ln_g': 1.0 + nrm(ks[18], (DEPTH, 2, D_MODEL), 0.02),
        'ln_b': nrm(ks[19], (DEPTH, 2, D_MODEL), 0.01),
        'peer_wq': nrm(ks[20], (DEPTH, D_MODEL, PEER_HEADS * PEER_KEY_DIM), d_inv),
        'peer_keys': nrm(ks[21], (DEPTH, PEER_HEADS, 2, PEER_N_KEYS, PEER_HALF), PEER_HALF ** -0.5),
        'peer_u': nrm(ks[22], (DEPTH, PEER_N_EXPERTS, D_MODEL), d_inv),
        'peer_v': nrm(ks[23], (DEPTH, PEER_N_EXPERTS, D_MODEL), DEEPNORM_BETA),
    }


def reference(x_prompt, x_sample, c, cache_attn_k, cache_attn_v, state_hgrn, c_ctx,
              mod_w, mod_b, w_in, attn_lambda, attn_subln, conv_w, conv_b, rec_lb, rec_norm,
              w_branch, w_out, ln_g, ln_b, peer_wq, peer_keys, peer_u, peer_v):
    lb_all = jnp.cumsum(jax.nn.softmax(rec_lb.astype(jnp.float32), axis=0), axis=0)
    lb_all = lb_all - lb_all[:1]
    y_p, y_s = x_prompt, x_sample
    ks, vs, ss = [], [], []
    for l in range(DEPTH):
        p = {'w_in': w_in[l], 'attn_lambda': attn_lambda[l], 'attn_subln': attn_subln[l],
             'conv_w': conv_w[l], 'conv_b': conv_b[l], 'lb': lb_all[l], 'rec_norm': rec_norm[l],
             'w_branch': w_branch[l], 'w_out': w_out[l], 'ln_g': ln_g[l], 'ln_b': ln_b[l],
             'peer_wq': peer_wq[l], 'peer_keys': peer_keys[l], 'peer_u': peer_u[l], 'peer_v': peer_v[l]}
        m_ctx = (jax.nn.silu(c_ctx) @ mod_w[l] + mod_b[l])[None, None, :]
        y_p, (k_l, v_l, s_l) = trunk_layer(y_p, m_ctx, l, p, None)
        ks.append(k_l)
        vs.append(v_l)
        ss.append(s_l)
        m_lat = (jax.nn.silu(c) @ mod_w[l] + mod_b[l])[:, None, :]
        y_s, _ = trunk_layer(y_s, m_lat, l, p, (cache_attn_k[:, l], cache_attn_v[:, l], state_hgrn[:, l]))
    new_cache_attn_k = jnp.stack(ks, axis=1)
    new_cache_attn_v = jnp.stack(vs, axis=1)
    new_state_hgrn = jnp.stack(ss, axis=1)
    return (y_p, y_s, new_cache_attn_k, new_cache_attn_v, new_state_hgrn)
```

```python
import functools
import math

import numpy as np
import jax
import jax.numpy as jnp
from jax import lax
from jax.experimental import pallas as pl
from jax.experimental.pallas import tpu as pltpu

F32 = jnp.float32
BF16 = jnp.bfloat16

D_MODEL = 1024
DEPTH = 2
GRID_W = 64
N_HEADS = 8
HEAD_W = 128
ATT_QK_DIM = 64
ROPE_BASE = 10000.0
ROPE_AXIS_DIM = ATT_QK_DIM // 2
PEER_N_KEYS = 128
PEER_TOPK = 16
LN_EPS = 1e-5
DEEPNORM_ALPHA = (2 * DEPTH) ** 0.25
W_IN_COLS = 14 * D_MODEL
COL_Q, COL_K, COL_V, COL_CB, COL_CC, COL_CX, COL_RQ, COL_RFF, COL_RFB, COL_RI, COL_RG, COL_G = range(12)

VMEM_LIMIT = 56 * 1024 * 1024
REC_C = 128
REC_LEVELS = 7


def _cparams(sem):
    return pltpu.CompilerParams(dimension_semantics=sem, vmem_limit_bytes=VMEM_LIMIT)


def _dot(a, b):
    return jnp.dot(a, b, preferred_element_type=F32)


def _dot_nt(a, b):
    return lax.dot_general(a, b, (((1,), (1,)), ((), ())), preferred_element_type=F32)


def _sigmoid(x):
    e = jnp.exp(-jnp.abs(x))
    r = 1.0 / (1.0 + e)
    return jnp.where(x >= 0, r, e * r)


def _layer_norm(y, g, b):
    mu = jnp.mean(y, axis=-1, keepdims=True)
    yc = y - mu
    var = jnp.mean(yc * yc, axis=-1, keepdims=True)
    return yc * lax.rsqrt(var + LN_EPS) * g + b


def _mod_body(c_ref, w_ref, b_ref, o_ref):
    c = c_ref[...]
    s = (c * _sigmoid(c)).astype(BF16)
    o_ref[0] = _dot(s, w_ref[0].astype(BF16)) + b_ref[0]


def _mod_vectors(cmat, mod_w, mod_b):
    tn = 1536
    return pl.pallas_call(
        _mod_body,
        grid=(DEPTH, 6 * D_MODEL // tn),
        in_specs=[pl.BlockSpec((8, D_MODEL), lambda l, j: (0, 0)),
                  pl.BlockSpec((1, D_MODEL, tn), lambda l, j: (l, 0, j)),
                  pl.BlockSpec((1, 1, tn), lambda l, j: (l, 0, j))],
        out_specs=pl.BlockSpec((1, 8, tn), lambda l, j: (l, 0, j)),
        out_shape=jax.ShapeDtypeStruct((DEPTH, 8, 6 * D_MODEL), F32),
        compiler_params=_cparams(("parallel", "parallel")),
        name="mod_vectors",
    )(cmat, mod_w, mod_b.reshape(DEPTH, 1, 6 * D_MODEL))


def _in_proj_body(x_ref, mod_ref, w_ref, *refs, rope, tm):
    if rope:
        cos_ref, sin_ref, o_ref, h_ref = refs
    else:
        o_ref, h_ref = refs
    j = pl.program_id(2)

    @pl.when(j == 0)
    def _():
        h_ref[...] = (x_ref[0] * (1.0 + mod_ref[0, 1:2, :]) + mod_ref[0, 0:1, :]).astype(BF16)

    z = _dot(h_ref[...], w_ref[...])
    if not rope:
        o_ref[0] = z
        return

    @pl.when(j < 2)
    def _():
        cos = cos_ref[...]
        sin = sin_ref[...]
        lane = lax.broadcasted_iota(jnp.int32, (tm, HEAD_W), 1)
        first = (lane % ROPE_AXIS_DIM) < (ROPE_AXIS_DIM // 2)
        for g in range(N_HEADS):
            zg = z[:, g * HEAD_W:(g + 1) * HEAD_W]
            partner = jnp.where(first, pltpu.roll(zg, HEAD_W - ROPE_AXIS_DIM // 2, 1),
                                pltpu.roll(zg, ROPE_AXIS_DIM // 2, 1))
            o_ref[0, :, g * HEAD_W:(g + 1) * HEAD_W] = zg * cos + partner * sin

    @pl.when(j >= 2)
    def _():
        o_ref[0] = z


def _in_proj(x, mod, w_bf16, rope_tabs, tm):
    B, T, _ = x.shape
    per_b = mod.shape[0] > 1
    rope = rope_tabs is not None
    in_specs = [pl.BlockSpec((1, tm, D_MODEL), lambda b, i, j: (b, i, 0)),
                pl.BlockSpec((1, 6, D_MODEL), (lambda b, i, j: (b, 0, 0)) if per_b else (lambda b, i, j: (0, 0, 0))),
                pl.BlockSpec((D_MODEL, D_MODEL), lambda b, i, j: (0, j))]
    args = [x, mod, w_bf16]
    if rope:
        in_specs += [pl.BlockSpec((tm, HEAD_W), lambda b, i, j: (i, 0))] * 2
        args += list(rope_tabs)
    return pl.pallas_call(
        functools.partial(_in_proj_body, rope=rope, tm=tm),
        grid=(B, T // tm, W_IN_COLS // D_MODEL),
        in_specs=in_specs,
        out_specs=pl.BlockSpec((1, tm, D_MODEL), lambda b, i, j: (b, i, j)),
        out_shape=jax.ShapeDtypeStruct((B, T, W_IN_COLS), F32),
        scratch_shapes=[pltpu.VMEM((tm, D_MODEL), BF16)],
        compiler_params=_cparams(("parallel", "parallel", "arbitrary")),
        name="in_proj_rope" if rope else "in_proj",
    )(*args)


def _rope_tables(T):
    t = np.arange(T)
    pos = np.stack([t // GRID_W, t % GRID_W], axis=1).astype(np.float32)
    lane = np.arange(HEAD_W)
    axis = (lane % ATT_QK_DIM) // ROPE_AXIS_DIM
    r = lane % ROPE_AXIS_DIM
    half = ROPE_AXIS_DIM // 2
    freqs = ROPE_BASE ** (-jnp.arange(0, ROPE_AXIS_DIM, 2, dtype=F32) / ROPE_AXIS_DIM)
    ang = jnp.asarray(pos)[:, axis] * freqs[r % half][None, :]
    sign = jnp.asarray(np.where(r < half, -1.0, 1.0).astype(np.float32))[None, :]
    return jnp.cos(ang), jnp.sin(ang) * sign


def _attn_body(lam_ref, q_ref, k_ref, v_ref, *refs, tq, tk, n_self, n_ctx, out_scale):
    if n_ctx:
        kc_ref, vc_ref, g_ref, o_ref, kb_ref, vb_ref = refs
    else:
        g_ref, o_ref, kb_ref, vb_ref = refs
    t_self = n_self * tk

    @pl.when(pl.program_id(2) == 0)
    def _():
        kb_ref[0:t_self, :] = k_ref[0].astype(BF16)
        vb_ref[0:t_self, :] = v_ref[0].astype(BF16)
        if n_ctx:
            kb_ref[t_self:, :] = kc_ref[0, 0].astype(BF16)
            vb_ref[t_self:, :] = vc_ref[0, 0].astype(BF16)

    q = q_ref[0] * (ATT_QK_DIM ** -0.5)
    lane = lax.broadcasted_iota(jnp.int32, (tq, HEAD_W), 1)
    lo = lane < ATT_QK_DIM
    qs = jnp.concatenate([jnp.where(lo, q, 0.0), jnp.where(lo, 0.0, q)], axis=0).astype(BF16)

    def step(c, carry):
        m, l, acc = carry
        off = pl.multiple_of(c * tk, tk)
        s = _dot_nt(qs, kb_ref[pl.ds(off, tk), :])
        m_new = jnp.maximum(m, jnp.max(s, axis=-1, keepdims=True))
        a = jnp.exp(m - m_new)
        p = jnp.exp(s - m_new)
        l = a * l + jnp.sum(p, axis=-1, keepdims=True)
        acc = a * acc + _dot(p.astype(BF16), vb_ref[pl.ds(off, tk), :])
        return m_new, l, acc

    init = (jnp.full((2 * tq, 1), -jnp.inf, F32), jnp.zeros((2 * tq, 1), F32), jnp.zeros((2 * tq, HEAD_W), F32))
    m, l, acc = lax.fori_loop(0, n_self + n_ctx, step, init)
    o = acc / l
    o = o[:tq] - lam_ref[0] * o[tq:]
    o = o * lax.rsqrt(jnp.mean(o * o, axis=-1, keepdims=True) + LN_EPS) * g_ref[...] * out_scale
    o_ref[0] = o.astype(BF16)


def _attention(z, lam, subln, ctx_kv, layer, out_scale, tq, tk):
    B, T, _ = z.shape
    n_self = T // tk
    in_specs = [pl.BlockSpec(memory_space=pltpu.SMEM),
                pl.BlockSpec((1, tq, HEAD_W), lambda b, h, i: (b, i, COL_Q * N_HEADS + h)),
                pl.BlockSpec((1, T, HEAD_W), lambda b, h, i: (b, 0, COL_K * N_HEADS + h)),
                pl.BlockSpec((1, T, HEAD_W), lambda b, h, i: (b, 0, COL_V * N_HEADS + h))]
    args = [lam, z, z, z]
    t_all = T
    n_ctx = 0
    if ctx_kv is not None:
        P = ctx_kv[0].shape[2]
        assert P == tk
        n_ctx = 1
        t_all = T + P
        in_specs += [pl.BlockSpec((1, 1, P, HEAD_W), lambda b, h, i: (b, layer, 0, h))] * 2
        args += list(ctx_kv)
    in_specs.append(pl.BlockSpec((1, HEAD_W), lambda b, h, i: (0, 0)))
    args.append(subln)
    return pl.pallas_call(
        functools.partial(_attn_body, tq=tq, tk=tk, n_self=n_self, n_ctx=n_ctx, out_scale=out_scale),
        grid=(B, N_HEADS, T // tq),
        in_specs=in_specs,
        out_specs=pl.BlockSpec((1, tq, HEAD_W), lambda b, h, i: (b, i, h)),
        out_shape=jax.ShapeDtypeStruct((B, T, D_MODEL), BF16),
        scratch_shapes=[pltpu.VMEM((t_all, HEAD_W), BF16), pltpu.VMEM((t_all, HEAD_W), BF16)],
        compiler_params=_cparams(("parallel", "parallel", "arbitrary")),
        name="attention_ctx" if n_ctx else "attention",
    )(*args)


def _rec_constants():
    C = REC_C
    t = np.arange(C)[:, None]
    j = np.arange(C)[None, :]
    blocks = [(j <= t), (j > t)]
    for l in range(REC_LEVELS):
        m = 1 << l
        seg0 = (t // m) * m
        odd = ((t // m) % 2) == 1
        blocks.append(np.where(odd, (j >= seg0) & (j <= t), (j > t) & (j <= seg0 + m - 1)))
    mf = np.concatenate(blocks, axis=0).astype(np.float32)
    x = t ^ j
    lv = np.where(x == 0, REC_LEVELS, np.floor(np.log2(np.maximum(x, 1))).astype(np.int64))
    lvf = np.where(j <= t, lv, REC_LEVELS + 1).astype(np.int32)
    mb = mf.reshape(-1, C, C)[:, ::-1, ::-1].reshape(-1, C)
    lvb = lvf[::-1, ::-1]
    return (jnp.asarray(mf, BF16), jnp.asarray(np.ascontiguousarray(mb), BF16),
            jnp.asarray(lvf), jnp.asarray(np.ascontiguousarray(lvb)))


def _rec_direction(rq, ri, fx, lb, m_ref, lv, st, backward):
    C = REC_C
    e = jnp.exp(-jnp.abs(fx))
    r = 1.0 / (1.0 + e)
    pos = fx >= 0
    sig = jnp.where(pos, r, e * r)
    nsig = jnp.where(pos, e * r, r)
    logf = jnp.log(lb + (1.0 - lb) * sig)
    kk = (1.0 - lb) * nsig
    q = rq * _sigmoid(rq)
    hi = logf.astype(BF16)
    mid = (logf - hi.astype(F32)).astype(BF16)
    lf2 = jnp.concatenate([hi, mid], axis=1)

    def expo(blk):
        e2 = _dot(m_ref[blk * C:(blk + 1) * C, :], lf2)
        return e2[:, :HEAD_W] + e2[:, HEAD_W:]

    row = lax.broadcasted_iota(jnp.int32, (C, HEAD_W), 0)
    if backward:
        row = (C - 1) - row
    qb = q.astype(BF16)
    kb = kk.astype(BF16)
    a = jnp.where(lv == REC_LEVELS, _dot_nt(qb, kb), 0.0)
    for l in range(REC_LEVELS):
        odd = ((row >> l) & 1) == 1
        w = (jnp.exp(expo(2 + l)) * jnp.where(odd, q, kk)).astype(BF16)
        a = jnp.where(lv == l, _dot_nt(w, w), a)
    b_incl = expo(0)
    rib = ri.astype(BF16)
    o = _dot(a.astype(BF16), rib) + _dot_nt((q * jnp.exp(b_incl)).astype(BF16), st.astype(BF16))
    ki = (kk * jnp.exp(expo(1))).astype(BF16)
    b_last = b_incl[0:1, :] if backward else b_incl[C - 1:C, :]
    st_new = st * jnp.exp(b_last) + _dot(ri.T.astype(BF16), ki)
    return o, st_new


def _hgrn_body(mf_ref, mb_ref, lvf_ref, lvb_ref, rqf_ref, rif_ref, ff_ref, rqb_ref, rib_ref, fb_ref, lb_ref,
               *refs, has_s0):
    if has_s0:
        s0_ref, of_ref, ob_ref, so_ref, sf_scr, sb_scr = refs
    else:
        of_ref, ob_ref, so_ref, sf_scr, sb_scr = refs
    c = pl.program_id(2)

    @pl.when(c == 0)
    def _():
        if has_s0:
            sf_scr[...] = s0_ref[0, 0, 0].T
            sb_scr[...] = s0_ref[0, 1, 0].T
        else:
            sf_scr[...] = jnp.zeros_like(sf_scr)
            sb_scr[...] = jnp.zeros_like(sb_scr)

    o, st = _rec_direction(rqf_ref[0], rif_ref[0], ff_ref[0], lb_ref[0:1, :], mf_ref, lvf_ref[...], sf_scr[...], False)
    of_ref[0] = o
    sf_scr[...] = st
    o, st = _rec_direction(rqb_ref[0], rib_ref[0], fb_ref[0], lb_ref[1:2, :], mb_ref, lvb_ref[...], sb_scr[...], True)
    ob_ref[0] = o
    sb_scr[...] = st

    @pl.when(c == pl.num_programs(2) - 1)
    def _():
        so_ref[0, 0, 0] = sf_scr[...].T
        so_ref[0, 1, 0] = sb_scr[...].T


def _hgrn(z, lb, s0, layer, consts):
    B, T, _ = z.shape
    n = T // REC_C
    C = REC_C

    def fwd(col):
        return pl.BlockSpec((1, C, HEAD_W), lambda b, h, c: (b, c, col * N_HEADS + h))

    def bwd(col):
        return pl.BlockSpec((1, C, HEAD_W), lambda b, h, c: (b, n - 1 - c, col * N_HEADS + h))

    const2 = lambda b, h, c: (0, 0)
    in_specs = [pl.BlockSpec(((2 + REC_LEVELS) * C, C), const2), pl.BlockSpec(((2 + REC_LEVELS) * C, C), const2),
                pl.BlockSpec((C, C), const2), pl.BlockSpec((C, C), const2),
                fwd(COL_RQ), fwd(COL_RI), fwd(COL_RFF), bwd(COL_RQ), bwd(COL_RI), bwd(COL_RFB),
                pl.BlockSpec((2, HEAD_W), lambda b, h, c: (0, h))]
    args = list(consts) + [z] * 6 + [lb]
    if s0 is not None:
        s0v = s0.reshape(B, DEPTH * 2, N_HEADS, HEAD_W, HEAD_W)
        in_specs.append(pl.BlockSpec((1, 2, 1, HEAD_W, HEAD_W), lambda b, h, c: (b, layer, h, 0, 0)))
        args.append(s0v)
    return pl.pallas_call(
        functools.partial(_hgrn_body, has_s0=s0 is not None),
        grid=(B, N_HEADS, n),
        in_specs=in_specs,
        out_specs=[pl.BlockSpec((1, C, HEAD_W), lambda b, h, c: (b, c, h)),
                   pl.BlockSpec((1, C, HEAD_W), lambda b, h, c: (b, n - 1 - c, h)),
                   pl.BlockSpec((1, 2, 1, HEAD_W, HEAD_W), lambda b, h, c: (b, 0, h, 0, 0))],
        out_shape=[jax.ShapeDtypeStruct((B, T, D_MODEL), F32), jax.ShapeDtypeStruct((B, T, D_MODEL), F32),
                   jax.ShapeDtypeStruct((B, 2, N_HEADS, HEAD_W, HEAD_W), F32)],
        scratch_shapes=[pltpu.VMEM((HEAD_W, HEAD_W), F32), pltpu.VMEM((HEAD_W, HEAD_W), F32)],
        compiler_params=_cparams(("parallel", "parallel", "arbitrary")),
        name="hgrn_ctx" if s0 is not None else "hgrn",
    )(*args)


def _merge_body(oatt_ref, zb_ref, zc_ref, zx_ref, zcp_ref, zxp_ref, zcn_ref, zxn_ref, of_ref, ob_ref, rg_ref,
                g0_ref, g1_ref, g2_ref, x_ref, mod_ref, wb_ref, wo_ref, cw_ref, cb_ref, rn_ref, ln_ref,
                x1_ref, h2_ref, *, tm):
    i = pl.program_id(1)
    u = zc_ref[0] * zx_ref[0]
    row = lax.broadcasted_iota(jnp.int32, (tm, D_MODEL), 0)
    prev_ok = (i > 0).astype(F32)
    next_ok = (i < pl.num_programs(1) - 1).astype(F32)
    u_prev_edge = zcp_ref[0, 7:8, :] * zxp_ref[0, 7:8, :] * prev_ok
    u_next_edge = zcn_ref[0, 0:1, :] * zxn_ref[0, 0:1, :] * next_ok
    up = jnp.where(row == 0, u_prev_edge, pltpu.roll(u, 1, 0))
    un = jnp.where(row == tm - 1, u_next_edge, pltpu.roll(u, tm - 1, 0))
    conv = up * cw_ref[0:1, :] + u * cw_ref[1:2, :] + un * cw_ref[2:3, :] + cb_ref[...]
    o_conv = (zb_ref[0] * conv).astype(BF16)
    s = of_ref[0] + ob_ref[0]
    rg = rg_ref[0]
    parts = []
    for h in range(N_HEADS):
        sh = s[:, h * HEAD_W:(h + 1) * HEAD_W]
        parts.append(sh * lax.rsqrt(jnp.mean(sh * sh, axis=-1, keepdims=True) + LN_EPS))
    o_rec = (jnp.concatenate(parts, axis=1) * rn_ref[...] * (rg * _sigmoid(rg))).astype(BF16)
    merged = (_sigmoid(g0_ref[0]) * _dot(oatt_ref[0], wb_ref[0])
              + _sigmoid(g1_ref[0]) * _dot(o_conv, wb_ref[1])
              + _sigmoid(g2_ref[0]) * _dot(o_rec, wb_ref[2]))
    mix = _dot(merged.astype(BF16), wo_ref[...])
    y = DEEPNORM_ALPHA * x_ref[0] + mod_ref[0, 2:3, :] * mix
    x1 = _layer_norm(y, ln_ref[0:1, :], ln_ref[1:2, :])
    x1_ref[0] = x1
    h2_ref[0] = (x1 * (1.0 + mod_ref[0, 4:5, :]) + mod_ref[0, 3:4, :]).astype(BF16)


def _merge(x, z, oatt, o_f, o_b, mod, wb_bf16, wo_bf16, conv_w, conv_b, rec_norm_t, ln_gb, tm):
    B, T, _ = x.shape
    per_b = mod.shape[0] > 1
    nb8 = tm // 8
    last8 = T // 8 - 1

    def col(c):
        return pl.BlockSpec((1, tm, D_MODEL), lambda b, i: (b, i, c))

    def prev8(c):
        return pl.BlockSpec((1, 8, D_MODEL), lambda b, i: (b, jnp.maximum(i * nb8 - 1, 0), c))

    def next8(c):
        return pl.BlockSpec((1, 8, D_MODEL), lambda b, i: (b, jnp.minimum((i + 1) * nb8, last8), c))

    tile = pl.BlockSpec((1, tm, D_MODEL), lambda b, i: (b, i, 0))
    full2 = lambda b, i: (0, 0)
    in_specs = [tile, col(COL_CB), col(COL_CC), col(COL_CX), prev8(COL_CC), prev8(COL_CX), next8(COL_CC), next8(COL_CX),
                tile, tile, col(COL_RG), col(COL_G), col(COL_G + 1), col(COL_G + 2), tile,
                pl.BlockSpec((1, 6, D_MODEL), (lambda b, i: (b, 0, 0)) if per_b else (lambda b, i: (0, 0, 0))),
                pl.BlockSpec((3, D_MODEL, D_MODEL), lambda b, i: (0, 0, 0)),
                pl.BlockSpec((D_MODEL, D_MODEL), full2),
                pl.BlockSpec((3, D_MODEL), full2), pl.BlockSpec((1, D_MODEL), full2),
                pl.BlockSpec((1, D_MODEL), full2), pl.BlockSpec((2, D_MODEL), full2)]
    return pl.pallas_call(
        functools.partial(_merge_body, tm=tm),
        grid=(B, T // tm),
        in_specs=in_specs,
        out_specs=[tile, tile],
        out_shape=[jax.ShapeDtypeStruct((B, T, D_MODEL), F32), jax.ShapeDtypeStruct((B, T, D_MODEL), BF16)],
        compiler_params=_cparams(("parallel", "arbitrary")),
        name="merge",
    )(oatt, z, z, z, z, z, z, z, o_f, o_b, z, z, z, z, x, mod, wb_bf16, wo_bf16, conv_w, conv_b, rec_norm_t, ln_gb)


def _top_rows(s, k):
    n, w = s.shape
    rid = lax.broadcasted_iota(jnp.int32, (n, w), 0).astype(F32)
    vals = []
    for _ in range(k):
        m = jnp.max(s, axis=0, keepdims=True)
        first = jnp.min(jnp.where(s == m, rid, float(n)), axis=0, keepdims=True)
        s = jnp.where(rid == first, -jnp.inf, s)
        vals.append(m)
    return vals


def _route_body(h_ref, wq_ref, keys_ref, s1_ref, s2_ref, e1_ref, e2_ref, thr_ref, q_scr, *, tt):
    q_scr[...] = _dot(h_ref[0], wq_ref[...]).astype(BF16)

    def head(h, carry):
        for p, s_ref in ((0, s1_ref), (1, s2_ref)):
            off = pl.multiple_of((2 * h + p) * HEAD_W, HEAD_W)
            s_ref[h] = _dot_nt(keys_ref[2 * h + p], q_scr[:, pl.ds(off, HEAD_W)])
        for lc in range(tt // HEAD_W):
            ls = slice(lc * HEAD_W, (lc + 1) * HEAD_W)
            s1 = s1_ref[h, :, ls]
            s2 = s2_ref[h, :, ls]
            v1 = _top_rows(s1, PEER_TOPK)
            v2 = _top_rows(s2, PEER_TOPK)
            v2a = jnp.concatenate(v2, axis=0)
            cands = [v1[0] + v2a] + [v1[a] + v2a[:8] for a in range(1, 8)] + [jnp.concatenate(v1[8:], axis=0) + v2[0]]
            cand = jnp.concatenate(cands, axis=0)
            thr = _top_rows(cand, PEER_TOPK)[-1]
            zsum = jnp.sum(jnp.where(cand >= thr, jnp.exp(cand - (v1[0] + v2[0])), 0.0), axis=0, keepdims=True)
            e1_ref[h, :, ls] = jnp.exp(s1 - v1[0]) / zsum
            e2_ref[h, :, ls] = jnp.exp(s2 - v2[0])
            thr_ref[h, :, ls] = thr
        return carry

    lax.fori_loop(0, N_HEADS, head, 0)


def _peer_route(h2, wq_bf16, keys_bf16, tt):
    B, T, _ = h2.shape
    big = pl.BlockSpec((None, N_HEADS, PEER_N_KEYS, tt), lambda b, i: (b, 0, 0, i))
    big_shape = jax.ShapeDtypeStruct((B, N_HEADS, PEER_N_KEYS, T), F32)
    return pl.pallas_call(
        functools.partial(_route_body, tt=tt),
        grid=(B, T // tt),
        in_specs=[pl.BlockSpec((1, tt, D_MODEL), lambda b, i: (b, i, 0)),
                  pl.BlockSpec((D_MODEL, 2 * N_HEADS * HEAD_W), lambda b, i: (0, 0)),
                  pl.BlockSpec((2 * N_HEADS, PEER_N_KEYS, HEAD_W), lambda b, i: (0, 0, 0))],
        out_specs=[big, big, big, big, pl.BlockSpec((None, N_HEADS, 1, tt), lambda b, i: (b, 0, 0, i))],
        out_shape=[big_shape] * 4 + [jax.ShapeDtypeStruct((B, N_HEADS, 1, T), F32)],
        scratch_shapes=[pltpu.VMEM((tt, 2 * N_HEADS * HEAD_W), BF16)],
        compiler_params=_cparams(("parallel", "parallel")),
        name="peer_route",
    )(h2, wq_bf16, keys_bf16)


PEER_STEP_KEYS = 8


def _dense_body(h_ref, u_ref, vt_ref, s1_ref, e1_ref, s2_ref, e2_ref, thr_ref, x_ref, mod_ref, ln_ref, o_ref,
                act_scr, w_scr, acc_scr, *, tt):
    k = pl.program_id(2)

    @pl.when(k == 0)
    def _():
        acc_scr[...] = jnp.zeros_like(acc_scr)

    act_scr[...] = _dot_nt(u_ref[...], h_ref[0])

    def lane_chunk(lc, carry):
        ls = pl.ds(pl.multiple_of(lc * HEAD_W, HEAD_W), HEAD_W)
        for j in range(PEER_STEP_KEYS):
            g = jnp.zeros((PEER_N_KEYS, HEAD_W), F32)
            for h in range(N_HEADS):
                hit = (s1_ref[h, j:j + 1, ls] + s2_ref[h, :, ls]) >= thr_ref[h, :, ls]
                g = g + jnp.where(hit, e1_ref[h, j:j + 1, ls] * e2_ref[h, :, ls], 0.0)
            a = act_scr[j * PEER_N_KEYS:(j + 1) * PEER_N_KEYS, ls]
            gelu = 0.5 * a * (1.0 + lax.erf(a * (2.0 ** -0.5)))
            w_scr[j * PEER_N_KEYS:(j + 1) * PEER_N_KEYS, ls] = (gelu * g).astype(BF16)
        return carry

    lax.fori_loop(0, tt // HEAD_W, lane_chunk, 0)
    acc_scr[...] += _dot(vt_ref[...], w_scr[...])

    @pl.when(k == pl.num_programs(2) - 1)
    def _():
        y = DEEPNORM_ALPHA * x_ref[0] + mod_ref[0, 5:6, :] * acc_scr[...].T
        o_ref[0] = _layer_norm(y, ln_ref[0:1, :], ln_ref[1:2, :])


def _peer_dense(h2, x1, mod, u_bf16, vt_bf16, route, ln_gb, tt):
    B, T, _ = h2.shape
    s1, s2, e1, e2, thr = route
    per_b = mod.shape[0] > 1
    ne = PEER_STEP_KEYS * PEER_N_KEYS
    rows = pl.BlockSpec((None, N_HEADS, PEER_STEP_KEYS, tt), lambda b, i, k: (b, 0, k, i))
    full = pl.BlockSpec((None, N_HEADS, PEER_N_KEYS, tt), lambda b, i, k: (b, 0, 0, i))
    return pl.pallas_call(
        functools.partial(_dense_body, tt=tt),
        grid=(B, T // tt, PEER_N_KEYS // PEER_STEP_KEYS),
        in_specs=[pl.BlockSpec((1, tt, D_MODEL), lambda b, i, k: (b, i, 0)),
                  pl.BlockSpec((ne, D_MODEL), lambda b, i, k: (k, 0)),
                  pl.BlockSpec((D_MODEL, ne), lambda b, i, k: (0, k)),
                  rows, rows, full, full,
                  pl.BlockSpec((None, N_HEADS, 1, tt), lambda b, i, k: (b, 0, 0, i)),
                  pl.BlockSpec((1, tt, D_MODEL), lambda b, i, k: (b, i, 0)),
                  pl.BlockSpec((1, 6, D_MODEL), (lambda b, i, k: (b, 0, 0)) if per_b else (lambda b, i, k: (0, 0, 0))),
                  pl.BlockSpec((2, D_MODEL), lambda b, i, k: (0, 0))],
        out_specs=pl.BlockSpec((1, tt, D_MODEL), lambda b, i, k: (b, i, 0)),
        out_shape=jax.ShapeDtypeStruct((B, T, D_MODEL), F32),
        scratch_shapes=[pltpu.VMEM((ne, tt), F32), pltpu.VMEM((ne, tt), BF16), pltpu.VMEM((D_MODEL, tt), F32)],
        compiler_params=_cparams(("parallel", "parallel", "arbitrary")),
        name="peer_dense",
    )(h2, u_bf16, vt_bf16, s1, e1, s2, e2, thr, x1, mod, ln_gb)


def _trunk_layer(x, mod, layer, p, ctx, rope_tabs, consts, flat_rows):
    B, T, _ = x.shape
    if flat_rows is not None:
        z = _in_proj(x.reshape(-1, flat_rows, D_MODEL), mod, p["w_in"], None, tm=flat_rows).reshape(B, T, W_IN_COLS)
    else:
        z = _in_proj(x, mod, p["w_in"], rope_tabs, tm=1024)
    tq = 128
    if ctx is None:
        oatt = _attention(z, p["lam"], p["subln"], None, layer, p["att_scale"], tq=tq, tk=T)
        o_f, o_b, s_fin = _hgrn(z, p["lb"], None, layer, consts)
    else:
        oatt = _attention(z, p["lam"], p["subln"], (ctx[0], ctx[1]), layer, p["att_scale"], tq=tq, tk=512)
        o_f, o_b, s_fin = _hgrn(z, p["lb"], ctx[2], layer, consts)
    x1, h2 = _merge(x, z, oatt, o_f, o_b, mod, p["w_branch"], p["w_out"], p["conv_w"], p["conv_b"],
                    p["rec_norm"], p["ln0"], tm=256)
    if flat_rows is not None:
        h2r, x1r = h2.reshape(-1, flat_rows, D_MODEL), x1.reshape(-1, flat_rows, D_MODEL)
    else:
        h2r, x1r = h2, x1
    tt = 512
    route = _peer_route(h2r, p["peer_wq"], p["peer_keys"], tt)
    x2 = _peer_dense(h2r, x1r, mod, p["peer_u"], p["peer_vt"], route, p["ln1"], tt).reshape(B, T, D_MODEL)
    return x2, z, s_fin


def _layer_params(l, lb_all, w_in, attn_lambda, attn_subln, conv_w, conv_b, rec_norm, w_branch, w_out, ln_g, ln_b,
                  peer_wq, peer_keys, peer_u, peer_v):
    lam_init = 0.8 - 0.6 * math.exp(-0.3 * l)
    lp = attn_lambda[l].astype(F32)
    lam = jnp.exp(jnp.sum(lp[0] * lp[1])) - jnp.exp(jnp.sum(lp[2] * lp[3])) + lam_init
    return {
        "w_in": w_in[l].astype(BF16), "lam": lam.reshape(1), "att_scale": 1.0 - lam_init,
        "subln": attn_subln[l].reshape(1, HEAD_W), "lb": lb_all[l],
        "w_branch": w_branch[l].astype(BF16), "w_out": w_out[l].astype(BF16),
        "conv_w": conv_w[l], "conv_b": conv_b[l].reshape(1, D_MODEL),
        "rec_norm": jnp.tile(rec_norm[l], N_HEADS).reshape(1, D_MODEL),
        "ln0": jnp.stack([ln_g[l, 0], ln_b[l, 0]]), "ln1": jnp.stack([ln_g[l, 1], ln_b[l, 1]]),
        "peer_wq": peer_wq[l].astype(BF16),
        "peer_keys": peer_keys[l].astype(BF16).reshape(2 * N_HEADS, PEER_N_KEYS, HEAD_W),
        "peer_u": peer_u[l].astype(BF16), "peer_vt": peer_v[l].astype(BF16).T,
    }


def kernel(x_prompt, x_sample, c, cache_attn_k, cache_attn_v, state_hgrn, c_ctx, mod_w, mod_b, w_in, attn_lambda,
           attn_subln, conv_w, conv_b, rec_lb, rec_norm, w_branch, w_out, ln_g, ln_b, peer_wq, peer_keys, peer_u,
           peer_v):
    B, T, _ = x_prompt.shape
    Bs, Ts, _ = x_sample.shape
    P = cache_attn_k.shape[2]
    lb_all = jnp.cumsum(jax.nn.softmax(rec_lb.astype(F32), axis=0), axis=0)
    lb_all = lb_all - lb_all[:1]
    cmat = jnp.concatenate([c_ctx[None, :], c, jnp.zeros((8 - 1 - Bs, D_MODEL), F32)], axis=0)
    mods = _mod_vectors(cmat, mod_w, mod_b).reshape(DEPTH, 8, 6, D_MODEL)
    rope_tabs = _rope_tables(Ts)
    consts = _rec_constants()
    ck = cache_attn_k.reshape(Bs, DEPTH, P, N_HEADS * HEAD_W)
    cv = cache_attn_v.reshape(Bs, DEPTH, P, N_HEADS * HEAD_W)

    y_p, y_s = x_prompt, x_sample
    ks, vs, ss = [], [], []
    for l in range(DEPTH):
        p = _layer_params(l, lb_all, w_in, attn_lambda, attn_subln, conv_w, conv_b, rec_norm, w_branch, w_out,
                          ln_g, ln_b, peer_wq, peer_keys, peer_u, peer_v)
        y_p, z_p, s_p = _trunk_layer(y_p, mods[l, 0:1], l, p, None, None, consts, flat_rows=1024)
        ks.append(z_p[..., COL_K * D_MODEL:(COL_K + 1) * D_MODEL].reshape(B, T, N_HEADS, HEAD_W))
        vs.append(z_p[..., COL_V * D_MODEL:(COL_V + 1) * D_MODEL].reshape(B, T, N_HEADS, HEAD_W))
        ss.append(s_p)
        y_s, _, _ = _trunk_layer(y_s, mods[l, 1:1 + Bs], l, p, (ck, cv, state_hgrn), rope_tabs, consts, flat_rows=None)
    return (y_p, y_s, jnp.stack(ks, axis=1), jnp.stack(vs, axis=1), jnp.stack(ss, axis=1))
```

```python
import functools
import math

import numpy as np
import jax
import jax.numpy as jnp
from jax import lax
from jax.experimental import pallas as pl
from jax.experimental.pallas import tpu as pltpu

F32 = jnp.float32
BF16 = jnp.bfloat16

D_MODEL = 1024
DEPTH = 2
GRID_W = 64
N_HEADS = 8
HEAD_W = 128
ATT_QK_DIM = 64
ROPE_BASE = 10000.0
ROPE_AXIS_DIM = ATT_QK_DIM // 2
PEER_N_KEYS = 128
PEER_TOPK = 16
LN_EPS = 1e-5
DEEPNORM_ALPHA = (2 * DEPTH) ** 0.25
W_IN_COLS = 14 * D_MODEL
COL_Q, COL_K, COL_V, COL_CB, COL_CC, COL_CX, COL_RQ, COL_RFF, COL_RFB, COL_RI, COL_RG, COL_G = range(12)

VMEM_LIMIT = 56 * 1024 * 1024
REC_C = 128
REC_LEVELS = 7
REC_HEADS_PER_STEP = 4


def _cparams(sem):
    return pltpu.CompilerParams(dimension_semantics=sem, vmem_limit_bytes=VMEM_LIMIT)


def _dot(a, b):
    return jnp.dot(a, b, preferred_element_type=F32)


def _dot_nt(a, b):
    return lax.dot_general(a, b, (((1,), (1,)), ((), ())), preferred_element_type=F32)


def _sigmoid(x):
    e = jnp.exp(-jnp.abs(x))
    r = 1.0 / (1.0 + e)
    return jnp.where(x >= 0, r, e * r)


def _layer_norm(y, g, b):
    mu = jnp.mean(y, axis=-1, keepdims=True)
    yc = y - mu
    var = jnp.mean(yc * yc, axis=-1, keepdims=True)
    return yc * lax.rsqrt(var + LN_EPS) * g + b


def _mod_body(c_ref, w_ref, b_ref, o_ref):
    c = c_ref[...]
    s = (c * _sigmoid(c)).astype(BF16)
    o_ref[0] = _dot(s, w_ref[0].astype(BF16)) + b_ref[0]


def _mod_vectors(cmat, mod_w, mod_b):
    tn = 1536
    return pl.pallas_call(
        _mod_body,
        grid=(DEPTH, 6 * D_MODEL // tn),
        in_specs=[pl.BlockSpec((8, D_MODEL), lambda l, j: (0, 0)),
                  pl.BlockSpec((1, D_MODEL, tn), lambda l, j: (l, 0, j)),
                  pl.BlockSpec((1, 1, tn), lambda l, j: (l, 0, j))],
        out_specs=pl.BlockSpec((1, 8, tn), lambda l, j: (l, 0, j)),
        out_shape=jax.ShapeDtypeStruct((DEPTH, 8, 6 * D_MODEL), F32),
        compiler_params=_cparams(("parallel", "parallel")),
        name="mod_vectors",
    )(cmat, mod_w, mod_b.reshape(DEPTH, 1, 6 * D_MODEL))


def _in_proj_body(x_ref, mod_ref, w_ref, *refs, rope, tm):
    if rope:
        cos_ref, sin_ref, o_ref, h_ref = refs
    else:
        o_ref, h_ref = refs
    j = pl.program_id(2)

    @pl.when(j == 0)
    def _():
        h_ref[...] = (x_ref[0] * (1.0 + mod_ref[0, 1:2, :]) + mod_ref[0, 0:1, :]).astype(BF16)

    z = _dot(h_ref[...], w_ref[...])
    if not rope:
        o_ref[0] = z
        return

    @pl.when(j < 2)
    def _():
        cos = cos_ref[...]
        sin = sin_ref[...]
        lane = lax.broadcasted_iota(jnp.int32, (tm, HEAD_W), 1)
        first = (lane % ROPE_AXIS_DIM) < (ROPE_AXIS_DIM // 2)
        for g in range(N_HEADS):
            zg = z[:, g * HEAD_W:(g + 1) * HEAD_W]
            partner = jnp.where(first, pltpu.roll(zg, HEAD_W - ROPE_AXIS_DIM // 2, 1),
                                pltpu.roll(zg, ROPE_AXIS_DIM // 2, 1))
            o_ref[0, :, g * HEAD_W:(g + 1) * HEAD_W] = zg * cos + partner * sin

    @pl.when(j >= 2)
    def _():
        o_ref[0] = z


def _in_proj(x, mod, w_bf16, rope_tabs, tm):
    B, T, _ = x.shape
    per_b = mod.shape[0] > 1
    rope = rope_tabs is not None
    in_specs = [pl.BlockSpec((1, tm, D_MODEL), lambda b, i, j: (b, i, 0)),
                pl.BlockSpec((1, 6, D_MODEL), (lambda b, i, j: (b, 0, 0)) if per_b else (lambda b, i, j: (0, 0, 0))),
                pl.BlockSpec((D_MODEL, D_MODEL), lambda b, i, j: (0, j))]
    args = [x, mod, w_bf16]
    if rope:
        in_specs += [pl.BlockSpec((tm, HEAD_W), lambda b, i, j: (i, 0))] * 2
        args += list(rope_tabs)
    return pl.pallas_call(
        functools.partial(_in_proj_body, rope=rope, tm=tm),
        grid=(B, T // tm, W_IN_COLS // D_MODEL),
        in_specs=in_specs,
        out_specs=pl.BlockSpec((1, tm, D_MODEL), lambda b, i, j: (b, i, j)),
        out_shape=jax.ShapeDtypeStruct((B, T, W_IN_COLS), F32),
        scratch_shapes=[pltpu.VMEM((tm, D_MODEL), BF16)],
        compiler_params=_cparams(("parallel", "parallel", "arbitrary")),
        name="in_proj_rope" if rope else "in_proj",
    )(*args)


def _rope_tables(T):
    t = np.arange(T)
    pos = np.stack([t // GRID_W, t % GRID_W], axis=1).astype(np.float32)
    lane = np.arange(HEAD_W)
    axis = (lane % ATT_QK_DIM) // ROPE_AXIS_DIM
    r = lane % ROPE_AXIS_DIM
    half = ROPE_AXIS_DIM // 2
    freqs = ROPE_BASE ** (-jnp.arange(0, ROPE_AXIS_DIM, 2, dtype=F32) / ROPE_AXIS_DIM)
    ang = jnp.asarray(pos)[:, axis] * freqs[r % half][None, :]
    sign = jnp.asarray(np.where(r < half, -1.0, 1.0).astype(np.float32))[None, :]
    return jnp.cos(ang), jnp.sin(ang) * sign


def _attn_body(lam_ref, q_ref, k_ref, v_ref, *refs, tq, tk, n_self, n_ctx, out_scale):
    if n_ctx:
        kc_ref, vc_ref, g_ref, o_ref, kb_ref, vb_ref = refs
    else:
        g_ref, o_ref, kb_ref, vb_ref = refs
    t_self = n_self * tk

    @pl.when(pl.program_id(2) == 0)
    def _():
        kb_ref[0:t_self, :] = k_ref[0].astype(BF16)
        vb_ref[0:t_self, :] = v_ref[0].astype(BF16)
        if n_ctx:
            kb_ref[t_self:, :] = kc_ref[0, 0].astype(BF16)
            vb_ref[t_self:, :] = vc_ref[0, 0].astype(BF16)

    q = q_ref[0] * (ATT_QK_DIM ** -0.5 * math.log2(math.e))
    lane = lax.broadcasted_iota(jnp.int32, (tq, HEAD_W), 1)
    lo = lane < ATT_QK_DIM
    qs = jnp.concatenate([jnp.where(lo, q, 0.0), jnp.where(lo, 0.0, q)], axis=0).astype(BF16)

    m = jnp.full((2 * tq, 1), -jnp.inf, F32)
    l = jnp.zeros((2 * tq, 1), F32)
    acc = jnp.zeros((2 * tq, HEAD_W), F32)
    for c in range(n_self + n_ctx):
        s = _dot_nt(qs, kb_ref[c * tk:(c + 1) * tk, :])
        m_new = jnp.maximum(m, jnp.max(s, axis=-1, keepdims=True))
        a = jnp.exp2(m - m_new)
        p = jnp.exp2(s - m_new)
        l = a * l + jnp.sum(p, axis=-1, keepdims=True)
        acc = a * acc + _dot(p.astype(BF16), vb_ref[c * tk:(c + 1) * tk, :])
        m = m_new
    o = acc / l
    o = o[:tq] - lam_ref[0] * o[tq:]
    o = o * lax.rsqrt(jnp.mean(o * o, axis=-1, keepdims=True) + LN_EPS) * g_ref[...] * out_scale
    o_ref[0] = o.astype(BF16)


def _attention(z, lam, subln, ctx_kv, layer, out_scale, tq, tk):
    B, T, _ = z.shape
    n_self = T // tk
    in_specs = [pl.BlockSpec(memory_space=pltpu.SMEM),
                pl.BlockSpec((1, tq, HEAD_W), lambda b, h, i: (b, i, COL_Q * N_HEADS + h)),
                pl.BlockSpec((1, T, HEAD_W), lambda b, h, i: (b, 0, COL_K * N_HEADS + h)),
                pl.BlockSpec((1, T, HEAD_W), lambda b, h, i: (b, 0, COL_V * N_HEADS + h))]
    args = [lam, z, z, z]
    t_all = T
    n_ctx = 0
    if ctx_kv is not None:
        P = ctx_kv[0].shape[2]
        assert P == tk
        n_ctx = 1
        t_all = T + P
        in_specs += [pl.BlockSpec((1, 1, P, HEAD_W), lambda b, h, i: (b, layer, 0, h))] * 2
        args += list(ctx_kv)
    in_specs.append(pl.BlockSpec((1, HEAD_W), lambda b, h, i: (0, 0)))
    args.append(subln)
    return pl.pallas_call(
        functools.partial(_attn_body, tq=tq, tk=tk, n_self=n_self, n_ctx=n_ctx, out_scale=out_scale),
        grid=(B, N_HEADS, T // tq),
        in_specs=in_specs,
        out_specs=pl.BlockSpec((1, tq, HEAD_W), lambda b, h, i: (b, i, h)),
        out_shape=jax.ShapeDtypeStruct((B, T, D_MODEL), BF16),
        scratch_shapes=[pltpu.VMEM((t_all, HEAD_W), BF16), pltpu.VMEM((t_all, HEAD_W), BF16)],
        compiler_params=_cparams(("parallel", "parallel", "arbitrary")),
        name="attention_ctx" if n_ctx else "attention",
    )(*args)


def _rec_constants():
    C = REC_C
    t = np.arange(C)[:, None]
    j = np.arange(C)[None, :]
    blocks = [(j <= t), (j > t)]
    for l in range(REC_LEVELS):
        m = 1 << l
        seg0 = (t // m) * m
        odd = ((t // m) % 2) == 1
        blocks.append(np.where(odd, (j >= seg0) & (j <= t), (j > t) & (j <= seg0 + m - 1)))
    mf = np.concatenate(blocks, axis=0).astype(np.float32)
    x = t ^ j
    lv = np.where(x == 0, REC_LEVELS, np.floor(np.log2(np.maximum(x, 1))).astype(np.int64))
    lvf = np.where(j <= t, lv, REC_LEVELS + 1).astype(np.int32)
    mb = mf.reshape(-1, C, C)[:, ::-1, ::-1].reshape(-1, C)
    lvb = lvf[::-1, ::-1]
    mf, mb = np.concatenate([mf, mf], axis=1), np.concatenate([mb, mb], axis=1)
    return (jnp.asarray(mf, BF16), jnp.asarray(np.ascontiguousarray(mb), BF16),
            jnp.asarray(lvf), jnp.asarray(np.ascontiguousarray(lvb)))


def _rec_pair(rq, ri, fx, lb, m_ref, lv, sts, backward):
    C = REC_C
    e = jnp.exp(-jnp.abs(fx))
    r = 1.0 / (1.0 + e)
    pos = fx >= 0
    sig = jnp.where(pos, r, e * r)
    nsig = jnp.where(pos, e * r, r)
    logf = jnp.log(lb + (1.0 - lb) * sig)
    kk = (1.0 - lb) * nsig
    q = rq * _sigmoid(rq)
    hi = logf.astype(BF16)
    mid = (logf - hi.astype(F32)).astype(BF16)
    lf2 = jnp.concatenate([hi, mid], axis=0)

    def expo(blk):
        return _dot(m_ref[blk * C:(blk + 1) * C, :], lf2)

    row = lax.broadcasted_iota(jnp.int32, (C, 2 * HEAD_W), 0)
    if backward:
        row = (C - 1) - row
    qb = q.astype(BF16)
    kb = kk.astype(BF16)
    ws = []
    for l in range(REC_LEVELS):
        odd = ((row >> l) & 1) == 1
        ws.append((jnp.exp(expo(2 + l)) * jnp.where(odd, q, kk)).astype(BF16))
    b_incl = expo(0)
    qi = (q * jnp.exp(b_incl)).astype(BF16)
    ki = (kk * jnp.exp(expo(1))).astype(BF16)
    dec = jnp.exp(b_incl[0:1, :] if backward else b_incl[C - 1:C, :])
    outs, new_sts = [], []
    for g in range(2):
        ls = slice(g * HEAD_W, (g + 1) * HEAD_W)
        a = jnp.where(lv == REC_LEVELS, _dot_nt(qb[:, ls], kb[:, ls]), 0.0)
        for l in range(REC_LEVELS):
            a = jnp.where(lv == l, _dot_nt(ws[l][:, ls], ws[l][:, ls]), a)
        rig = ri[:, ls]
        outs.append(_dot(a.astype(BF16), rig.astype(BF16)) + _dot_nt(qi[:, ls], sts[g].astype(BF16)))
        new_sts.append(sts[g] * dec[:, ls] + _dot(rig.T.astype(BF16), ki[:, ls]))
    return jnp.concatenate(outs, axis=1), new_sts


def _hgrn_body(mf_ref, mb_ref, lvf_ref, lvb_ref, rqf_ref, rif_ref, ff_ref, rqb_ref, rib_ref, fb_ref, lb_ref,
               *refs, has_s0):
    if has_s0:
        s0_ref, of_ref, ob_ref, so_ref, sf_scr, sb_scr = refs
    else:
        of_ref, ob_ref, so_ref, sf_scr, sb_scr = refs
    c = pl.program_id(2)

    @pl.when(c == 0)
    def _():
        for g in range(REC_HEADS_PER_STEP):
            if has_s0:
                sf_scr[g] = s0_ref[0, 0, g].T
                sb_scr[g] = s0_ref[0, 1, g].T
            else:
                sf_scr[g] = jnp.zeros((HEAD_W, HEAD_W), F32)
                sb_scr[g] = jnp.zeros((HEAD_W, HEAD_W), F32)

    for g in range(0, REC_HEADS_PER_STEP, 2):
        ls = slice(g * HEAD_W, (g + 2) * HEAD_W)
        o, st = _rec_pair(rqf_ref[0, :, ls], rif_ref[0, :, ls], ff_ref[0, :, ls], lb_ref[0:1, ls], mf_ref,
                          lvf_ref[...], [sf_scr[g], sf_scr[g + 1]], False)
        of_ref[0, :, ls] = o
        sf_scr[g] = st[0]
        sf_scr[g + 1] = st[1]
        o, st = _rec_pair(rqb_ref[0, :, ls], rib_ref[0, :, ls], fb_ref[0, :, ls], lb_ref[1:2, ls], mb_ref,
                          lvb_ref[...], [sb_scr[g], sb_scr[g + 1]], True)
        ob_ref[0, :, ls] = o
        sb_scr[g] = st[0]
        sb_scr[g + 1] = st[1]

    @pl.when(c == pl.num_programs(2) - 1)
    def _():
        for g in range(REC_HEADS_PER_STEP):
            so_ref[0, 0, g] = sf_scr[g].T
            so_ref[0, 1, g] = sb_scr[g].T


def _hgrn(z, lb, s0, layer, consts):
    B, T, _ = z.shape
    n = T // REC_C
    C = REC_C
    G = REC_HEADS_PER_STEP
    ng = N_HEADS // G
    W = G * HEAD_W

    def fwd(col):
        return pl.BlockSpec((1, C, W), lambda b, h, c: (b, c, col * ng + h))

    def bwd(col):
        return pl.BlockSpec((1, C, W), lambda b, h, c: (b, n - 1 - c, col * ng + h))

    const2 = lambda b, h, c: (0, 0)
    in_specs = [pl.BlockSpec(((2 + REC_LEVELS) * C, 2 * C), const2), pl.BlockSpec(((2 + REC_LEVELS) * C, 2 * C), const2),
                pl.BlockSpec((C, C), const2), pl.BlockSpec((C, C), const2),
                fwd(COL_RQ), fwd(COL_RI), fwd(COL_RFF), bwd(COL_RQ), bwd(COL_RI), bwd(COL_RFB),
                pl.BlockSpec((2, W), lambda b, h, c: (0, h))]
    args = list(consts) + [z] * 6 + [lb]
    if s0 is not None:
        s0v = s0.reshape(B, DEPTH * 2, N_HEADS, HEAD_W, HEAD_W)
        in_specs.append(pl.BlockSpec((1, 2, G, HEAD_W, HEAD_W), lambda b, h, c: (b, layer, h, 0, 0)))
        args.append(s0v)
    return pl.pallas_call(
        functools.partial(_hgrn_body, has_s0=s0 is not None),
        grid=(B, ng, n),
        in_specs=in_specs,
        out_specs=[pl.BlockSpec((1, C, W), lambda b, h, c: (b, c, h)),
                   pl.BlockSpec((1, C, W), lambda b, h, c: (b, n - 1 - c, h)),
                   pl.BlockSpec((1, 2, G, HEAD_W, HEAD_W), lambda b, h, c: (b, 0, h, 0, 0))],
        out_shape=[jax.ShapeDtypeStruct((B, T, D_MODEL), F32), jax.ShapeDtypeStruct((B, T, D_MODEL), F32),
                   jax.ShapeDtypeStruct((B, 2, N_HEADS, HEAD_W, HEAD_W), F32)],
        scratch_shapes=[pltpu.VMEM((G, HEAD_W, HEAD_W), F32), pltpu.VMEM((G, HEAD_W, HEAD_W), F32)],
        compiler_params=_cparams(("parallel", "parallel", "arbitrary")),
        name="hgrn_ctx" if s0 is not None else "hgrn",
    )(*args)


def _merge_body(oatt_ref, zb_ref, zc_ref, zx_ref, zcp_ref, zxp_ref, zcn_ref, zxn_ref, of_ref, ob_ref, rg_ref,
                g0_ref, g1_ref, g2_ref, x_ref, mod_ref, wb_ref, wo_ref, cw_ref, cb_ref, rn_ref, ln_ref,
                x1_ref, h2_ref, *, tm):
    i = pl.program_id(1)
    u = zc_ref[0] * zx_ref[0]
    row = lax.broadcasted_iota(jnp.int32, (tm, D_MODEL), 0)
    prev_ok = (i > 0).astype(F32)
    next_ok = (i < pl.num_programs(1) - 1).astype(F32)
    u_prev_edge = zcp_ref[0, 7:8, :] * zxp_ref[0, 7:8, :] * prev_ok
    u_next_edge = zcn_ref[0, 0:1, :] * zxn_ref[0, 0:1, :] * next_ok
    up = jnp.where(row == 0, u_prev_edge, pltpu.roll(u, 1, 0))
    un = jnp.where(row == tm - 1, u_next_edge, pltpu.roll(u, tm - 1, 0))
    conv = up * cw_ref[0:1, :] + u * cw_ref[1:2, :] + un * cw_ref[2:3, :] + cb_ref[...]
    o_conv = (zb_ref[0] * conv).astype(BF16)
    s = of_ref[0] + ob_ref[0]
    rg = rg_ref[0]
    parts = []
    for h in range(N_HEADS):
        sh = s[:, h * HEAD_W:(h + 1) * HEAD_W]
        parts.append(sh * lax.rsqrt(jnp.mean(sh * sh, axis=-1, keepdims=True) + LN_EPS))
    o_rec = (jnp.concatenate(parts, axis=1) * rn_ref[...] * (rg * _sigmoid(rg))).astype(BF16)
    merged = (_sigmoid(g0_ref[0]) * _dot(oatt_ref[0], wb_ref[0])
              + _sigmoid(g1_ref[0]) * _dot(o_conv, wb_ref[1])
              + _sigmoid(g2_ref[0]) * _dot(o_rec, wb_ref[2]))
    mix = _dot(merged.astype(BF16), wo_ref[...])
    y = DEEPNORM_ALPHA * x_ref[0] + mod_ref[0, 2:3, :] * mix
    x1 = _layer_norm(y, ln_ref[0:1, :], ln_ref[1:2, :])
    x1_ref[0] = x1
    h2_ref[0] = (x1 * (1.0 + mod_ref[0, 4:5, :]) + mod_ref[0, 3:4, :]).astype(BF16)


def _merge(x, z, oatt, o_f, o_b, mod, wb_bf16, wo_bf16, conv_w, conv_b, rec_norm_t, ln_gb, tm):
    B, T, _ = x.shape
    per_b = mod.shape[0] > 1
    nb8 = tm // 8
    last8 = T // 8 - 1

    def col(c):
        return pl.BlockSpec((1, tm, D_MODEL), lambda b, i: (b, i, c))

    def prev8(c):
        return pl.BlockSpec((1, 8, D_MODEL), lambda b, i: (b, jnp.maximum(i * nb8 - 1, 0), c))

    def next8(c):
        return pl.BlockSpec((1, 8, D_MODEL), lambda b, i: (b, jnp.minimum((i + 1) * nb8, last8), c))

    tile = pl.BlockSpec((1, tm, D_MODEL), lambda b, i: (b, i, 0))
    full2 = lambda b, i: (0, 0)
    in_specs = [tile, col(COL_CB), col(COL_CC), col(COL_CX), prev8(COL_CC), prev8(COL_CX), next8(COL_CC), next8(COL_CX),
                tile, tile, col(COL_RG), col(COL_G), col(COL_G + 1), col(COL_G + 2), tile,
                pl.BlockSpec((1, 6, D_MODEL), (lambda b, i: (b, 0, 0)) if per_b else (lambda b, i: (0, 0, 0))),
                pl.BlockSpec((3, D_MODEL, D_MODEL), lambda b, i: (0, 0, 0)),
                pl.BlockSpec((D_MODEL, D_MODEL), full2),
                pl.BlockSpec((3, D_MODEL), full2), pl.BlockSpec((1, D_MODEL), full2),
                pl.BlockSpec((1, D_MODEL), full2), pl.BlockSpec((2, D_MODEL), full2)]
    return pl.pallas_call(
        functools.partial(_merge_body, tm=tm),
        grid=(B, T // tm),
        in_specs=in_specs,
        out_specs=[tile, tile],
        out_shape=[jax.ShapeDtypeStruct((B, T, D_MODEL), F32), jax.ShapeDtypeStruct((B, T, D_MODEL), BF16)],
        compiler_params=_cparams(("parallel", "arbitrary")),
        name="merge",
    )(oatt, z, z, z, z, z, z, z, o_f, o_b, z, z, z, z, x, mod, wb_bf16, wo_bf16, conv_w, conv_b, rec_norm_t, ln_gb)


def _top_rows(s, k):
    n, w = s.shape
    rid = lax.broadcasted_iota(jnp.int32, (n, w), 0).astype(F32)
    vals = []
    for _ in range(k):
        m = jnp.max(s, axis=0, keepdims=True)
        first = jnp.min(jnp.where(s == m, rid, float(n)), axis=0, keepdims=True)
        s = jnp.where(rid == first, -jnp.inf, s)
        vals.append(m)
    return vals


def _route_body(h_ref, wq_ref, keys_ref, s1_ref, s2_ref, e1_ref, e2_ref, thr_ref, q_scr, *, tt):
    q_scr[...] = _dot(h_ref[0], wq_ref[...]).astype(BF16)

    def head(h, carry):
        for p, s_ref in ((0, s1_ref), (1, s2_ref)):
            off = pl.multiple_of((2 * h + p) * HEAD_W, HEAD_W)
            s_ref[h] = _dot_nt(keys_ref[2 * h + p], q_scr[:, pl.ds(off, HEAD_W)])
        for lc in range(tt // HEAD_W):
            ls = slice(lc * HEAD_W, (lc + 1) * HEAD_W)
            s1 = s1_ref[h, :, ls]
            s2 = s2_ref[h, :, ls]
            v1 = _top_rows(s1, PEER_TOPK)
            v2 = _top_rows(s2, PEER_TOPK)
            v2a = jnp.concatenate(v2, axis=0)
            cands = [v1[0] + v2a] + [v1[a] + v2a[:8] for a in range(1, 8)] + [jnp.concatenate(v1[8:], axis=0) + v2[0]]
            cand = jnp.concatenate(cands, axis=0)
            thr = _top_rows(cand, PEER_TOPK)[-1]
            zsum = jnp.sum(jnp.where(cand >= thr, jnp.exp(cand - (v1[0] + v2[0])), 0.0), axis=0, keepdims=True)
            e1_ref[h, :, ls] = jnp.exp(s1 - v1[0]) / zsum
            e2_ref[h, :, ls] = jnp.exp(s2 - v2[0])
            thr_ref[h, :, ls] = thr
        return carry

    lax.fori_loop(0, N_HEADS, head, 0)


def _peer_route(h2, wq_bf16, keys_bf16, tt):
    B, T, _ = h2.shape
    big = pl.BlockSpec((None, N_HEADS, PEER_N_KEYS, tt), lambda b, i: (b, 0, 0, i))
    big_shape = jax.ShapeDtypeStruct((B, N_HEADS, PEER_N_KEYS, T), F32)
    return pl.pallas_call(
        functools.partial(_route_body, tt=tt),
        grid=(B, T // tt),
        in_specs=[pl.BlockSpec((1, tt, D_MODEL), lambda b, i: (b, i, 0)),
                  pl.BlockSpec((D_MODEL, 2 * N_HEADS * HEAD_W), lambda b, i: (0, 0)),
                  pl.BlockSpec((2 * N_HEADS, PEER_N_KEYS, HEAD_W), lambda b, i: (0, 0, 0))],
        out_specs=[big, big, big, big, pl.BlockSpec((None, N_HEADS, 1, tt), lambda b, i: (b, 0, 0, i))],
        out_shape=[big_shape] * 4 + [jax.ShapeDtypeStruct((B, N_HEADS, 1, T), F32)],
        scratch_shapes=[pltpu.VMEM((tt, 2 * N_HEADS * HEAD_W), BF16)],
        compiler_params=_cparams(("parallel", "parallel")),
        name="peer_route",
    )(h2, wq_bf16, keys_bf16)


PEER_STEP_KEYS = 8


def _dense_body(h_ref, u_ref, vt_ref, s1_ref, e1_ref, s2_ref, e2_ref, thr_ref, x_ref, mod_ref, ln_ref, o_ref,
                act_scr, w_scr, acc_scr, *, tt):
    k = pl.program_id(2)

    @pl.when(k == 0)
    def _():
        acc_scr[...] = jnp.zeros_like(acc_scr)

    hb = h_ref[0]
    sub = 2 * PEER_N_KEYS
    for jj in range(PEER_STEP_KEYS // 2):
        act_scr[jj * sub:(jj + 1) * sub, :] = _dot_nt(u_ref[jj * sub:(jj + 1) * sub, :], hb)
    for jj in range(PEER_STEP_KEYS // 2):
        for j2 in range(2):
            j = 2 * jj + j2
            for lc in range(tt // HEAD_W):
                ls = slice(lc * HEAD_W, (lc + 1) * HEAD_W)
                g = None
                for h in range(N_HEADS):
                    hit = (s1_ref[h, j:j + 1, ls] + s2_ref[h, :, ls]) >= thr_ref[h, :, ls]
                    term = jnp.where(hit, e1_ref[h, j:j + 1, ls] * e2_ref[h, :, ls], 0.0)
                    g = term if g is None else g + term
                a = act_scr[j * PEER_N_KEYS:(j + 1) * PEER_N_KEYS, ls]
                gelu = 0.5 * a * (1.0 + lax.erf(a * (2.0 ** -0.5)))
                w_scr[j * PEER_N_KEYS:(j + 1) * PEER_N_KEYS, ls] = (gelu * g).astype(BF16)
        acc_scr[...] += _dot(vt_ref[:, jj * sub:(jj + 1) * sub], w_scr[jj * sub:(jj + 1) * sub, :])

    @pl.when(k == pl.num_programs(2) - 1)
    def _():
        y = DEEPNORM_ALPHA * x_ref[0] + mod_ref[0, 5:6, :] * acc_scr[...].T
        o_ref[0] = _layer_norm(y, ln_ref[0:1, :], ln_ref[1:2, :])


def _peer_dense(h2, x1, mod, u_bf16, vt_bf16, route, ln_gb, tt):
    B, T, _ = h2.shape
    s1, s2, e1, e2, thr = route
    per_b = mod.shape[0] > 1
    ne = PEER_STEP_KEYS * PEER_N_KEYS
    rows = pl.BlockSpec((None, N_HEADS, PEER_STEP_KEYS, tt), lambda b, i, k: (b, 0, k, i))
    full = pl.BlockSpec((None, N_HEADS, PEER_N_KEYS, tt), lambda b, i, k: (b, 0, 0, i))
    return pl.pallas_call(
        functools.partial(_dense_body, tt=tt),
        grid=(B, T // tt, PEER_N_KEYS // PEER_STEP_KEYS),
        in_specs=[pl.BlockSpec((1, tt, D_MODEL), lambda b, i, k: (b, i, 0)),
                  pl.BlockSpec((ne, D_MODEL), lambda b, i, k: (k, 0)),
                  pl.BlockSpec((D_MODEL, ne), lambda b, i, k: (0, k)),
                  rows, rows, full, full,
                  pl.BlockSpec((None, N_HEADS, 1, tt), lambda b, i, k: (b, 0, 0, i)),
                  pl.BlockSpec((1, tt, D_MODEL), lambda b, i, k: (b, i, 0)),
                  pl.BlockSpec((1, 6, D_MODEL), (lambda b, i, k: (b, 0, 0)) if per_b else (lambda b, i, k: (0, 0, 0))),
                  pl.BlockSpec((2, D_MODEL), lambda b, i, k: (0, 0))],
        out_specs=pl.BlockSpec((1, tt, D_MODEL), lambda b, i, k: (b, i, 0)),
        out_shape=jax.ShapeDtypeStruct((B, T, D_MODEL), F32),
        scratch_shapes=[pltpu.VMEM((ne, tt), F32), pltpu.VMEM((ne, tt), BF16), pltpu.VMEM((D_MODEL, tt), F32)],
        compiler_params=_cparams(("parallel", "parallel", "arbitrary")),
        name="peer_dense",
    )(h2, u_bf16, vt_bf16, s1, e1, s2, e2, thr, x1, mod, ln_gb)


def _trunk_layer(x, mod, layer, p, ctx, rope_tabs, consts, flat_rows):
    B, T, _ = x.shape
    if flat_rows is not None:
        z = _in_proj(x.reshape(-1, flat_rows, D_MODEL), mod, p["w_in"], None, tm=flat_rows).reshape(B, T, W_IN_COLS)
    else:
        z = _in_proj(x, mod, p["w_in"], rope_tabs, tm=1024)
    tq = 256
    if ctx is None:
        oatt = _attention(z, p["lam"], p["subln"], None, layer, p["att_scale"], tq=tq, tk=T)
        o_f, o_b, s_fin = _hgrn(z, p["lb"], None, layer, consts)
    else:
        oatt = _attention(z, p["lam"], p["subln"], (ctx[0], ctx[1]), layer, p["att_scale"], tq=tq, tk=512)
        o_f, o_b, s_fin = _hgrn(z, p["lb"], ctx[2], layer, consts)
    x1, h2 = _merge(x, z, oatt, o_f, o_b, mod, p["w_branch"], p["w_out"], p["conv_w"], p["conv_b"],
                    p["rec_norm"], p["ln0"], tm=256)
    if flat_rows is not None:
        h2r, x1r = h2.reshape(-1, flat_rows, D_MODEL), x1.reshape(-1, flat_rows, D_MODEL)
    else:
        h2r, x1r = h2, x1
    tt = 512
    route = _peer_route(h2r, p["peer_wq"], p["peer_keys"], tt)
    x2 = _peer_dense(h2r, x1r, mod, p["peer_u"], p["peer_vt"], route, p["ln1"], tt).reshape(B, T, D_MODEL)
    return x2, z, s_fin


def _layer_params(l, lb_all, w_in, attn_lambda, attn_subln, conv_w, conv_b, rec_norm, w_branch, w_out, ln_g, ln_b,
                  peer_wq, peer_keys, peer_u, peer_v):
    lam_init = 0.8 - 0.6 * math.exp(-0.3 * l)
    lp = attn_lambda[l].astype(F32)
    lam = jnp.exp(jnp.sum(lp[0] * lp[1])) - jnp.exp(jnp.sum(lp[2] * lp[3])) + lam_init
    return {
        "w_in": w_in[l].astype(BF16), "lam": lam.reshape(1), "att_scale": 1.0 - lam_init,
        "subln": attn_subln[l].reshape(1, HEAD_W), "lb": lb_all[l],
        "w_branch": w_branch[l].astype(BF16), "w_out": w_out[l].astype(BF16),
        "conv_w": conv_w[l], "conv_b": conv_b[l].reshape(1, D_MODEL),
        "rec_norm": jnp.tile(rec_norm[l], N_HEADS).reshape(1, D_MODEL),
        "ln0": jnp.stack([ln_g[l, 0], ln_b[l, 0]]), "ln1": jnp.stack([ln_g[l, 1], ln_b[l, 1]]),
        "peer_wq": peer_wq[l].astype(BF16),
        "peer_keys": peer_keys[l].astype(BF16).reshape(2 * N_HEADS, PEER_N_KEYS, HEAD_W),
        "peer_u": peer_u[l].astype(BF16), "peer_vt": peer_v[l].astype(BF16).T,
    }


def kernel(x_prompt, x_sample, c, cache_attn_k, cache_attn_v, state_hgrn, c_ctx, mod_w, mod_b, w_in, attn_lambda,
           attn_subln, conv_w, conv_b, rec_lb, rec_norm, w_branch, w_out, ln_g, ln_b, peer_wq, peer_keys, peer_u,
           peer_v):
    B, T, _ = x_prompt.shape
    Bs, Ts, _ = x_sample.shape
    P = cache_attn_k.shape[2]
    lb_all = jnp.cumsum(jax.nn.softmax(rec_lb.astype(F32), axis=0), axis=0)
    lb_all = lb_all - lb_all[:1]
    cmat = jnp.concatenate([c_ctx[None, :], c, jnp.zeros((8 - 1 - Bs, D_MODEL), F32)], axis=0)
    mods = _mod_vectors(cmat, mod_w, mod_b).reshape(DEPTH, 8, 6, D_MODEL)
    rope_tabs = _rope_tables(Ts)
    consts = _rec_constants()
    ck = cache_attn_k.reshape(Bs, DEPTH, P, N_HEADS * HEAD_W)
    cv = cache_attn_v.reshape(Bs, DEPTH, P, N_HEADS * HEAD_W)

    y_p, y_s = x_prompt, x_sample
    ks, vs, ss = [], [], []
    for l in range(DEPTH):
        p = _layer_params(l, lb_all, w_in, attn_lambda, attn_subln, conv_w, conv_b, rec_norm, w_branch, w_out,
                          ln_g, ln_b, peer_wq, peer_keys, peer_u, peer_v)
        y_p, z_p, s_p = _trunk_layer(y_p, mods[l, 0:1], l, p, None, None, consts, flat_rows=1024)
        ks.append(z_p[..., COL_K * D_MODEL:(COL_K + 1) * D_MODEL].reshape(B, T, N_HEADS, HEAD_W))
        vs.append(z_p[..., COL_V * D_MODEL:(COL_V + 1) * D_MODEL].reshape(B, T, N_HEADS, HEAD_W))
        ss.append(s_p)
        y_s, _, _ = _trunk_layer(y_s, mods[l, 1:1 + Bs], l, p, (ck, cv, state_hgrn), rope_tabs, consts, flat_rows=None)
    return (y_p, y_s, jnp.stack(ks, axis=1), jnp.stack(vs, axis=1), jnp.stack(ss, axis=1))
```

```python
import functools
import math

import numpy as np
import jax
import jax.numpy as jnp
from jax import lax
from jax.experimental import pallas as pl
from jax.experimental.pallas import tpu as pltpu

F32 = jnp.float32
BF16 = jnp.bfloat16

D_MODEL = 1024
DEPTH = 2
GRID_W = 64
N_HEADS = 8
HEAD_W = 128
ATT_QK_DIM = 64
ROPE_BASE = 10000.0
ROPE_AXIS_DIM = ATT_QK_DIM // 2
PEER_N_KEYS = 128
PEER_TOPK = 16
LN_EPS = 1e-5
DEEPNORM_ALPHA = (2 * DEPTH) ** 0.25
W_IN_COLS = 14 * D_MODEL
COL_Q, COL_K, COL_V, COL_CB, COL_CC, COL_CX, COL_RQ, COL_RFF, COL_RFB, COL_RI, COL_RG, COL_G = range(12)

VMEM_LIMIT = 56 * 1024 * 1024
REC_C = 128
REC_LEVELS = 7
REC_HEADS_PER_STEP = 4


def _cparams(sem):
    return pltpu.CompilerParams(dimension_semantics=sem, vmem_limit_bytes=VMEM_LIMIT)


def _dot(a, b):
    return jnp.dot(a, b, preferred_element_type=F32)


def _dot_nt(a, b):
    return lax.dot_general(a, b, (((1,), (1,)), ((), ())), preferred_element_type=F32)


def _sigmoid(x):
    e = jnp.exp(-jnp.abs(x))
    r = 1.0 / (1.0 + e)
    return jnp.where(x >= 0, r, e * r)


def _layer_norm(y, g, b):
    mu = jnp.mean(y, axis=-1, keepdims=True)
    yc = y - mu
    var = jnp.mean(yc * yc, axis=-1, keepdims=True)
    return yc * lax.rsqrt(var + LN_EPS) * g + b


def _mod_body(c_ref, w_ref, b_ref, o_ref):
    c = c_ref[...]
    s = (c * _sigmoid(c)).astype(BF16)
    o_ref[0] = _dot(s, w_ref[0].astype(BF16)) + b_ref[0]


def _mod_vectors(cmat, mod_w, mod_b):
    tn = 1536
    return pl.pallas_call(
        _mod_body,
        grid=(DEPTH, 6 * D_MODEL // tn),
        in_specs=[pl.BlockSpec((8, D_MODEL), lambda l, j: (0, 0)),
                  pl.BlockSpec((1, D_MODEL, tn), lambda l, j: (l, 0, j)),
                  pl.BlockSpec((1, 1, tn), lambda l, j: (l, 0, j))],
        out_specs=pl.BlockSpec((1, 8, tn), lambda l, j: (l, 0, j)),
        out_shape=jax.ShapeDtypeStruct((DEPTH, 8, 6 * D_MODEL), F32),
        compiler_params=_cparams(("parallel", "parallel")),
        name="mod_vectors",
    )(cmat, mod_w, mod_b.reshape(DEPTH, 1, 6 * D_MODEL))


def _in_proj_body(x_ref, mod_ref, w_ref, *refs, rope, tm):
    if rope:
        cos_ref, sin_ref, o_ref, h_ref = refs
    else:
        o_ref, h_ref = refs
    j = pl.program_id(2)

    @pl.when(j == 0)
    def _():
        h_ref[...] = (x_ref[0] * (1.0 + mod_ref[0, 1:2, :]) + mod_ref[0, 0:1, :]).astype(BF16)

    z = _dot(h_ref[...], w_ref[...])
    if not rope:
        o_ref[0] = z
        return

    @pl.when(j < 2)
    def _():
        cos = cos_ref[...]
        sin = sin_ref[...]
        lane = lax.broadcasted_iota(jnp.int32, (tm, HEAD_W), 1)
        first = (lane % ROPE_AXIS_DIM) < (ROPE_AXIS_DIM // 2)
        for g in range(N_HEADS):
            zg = z[:, g * HEAD_W:(g + 1) * HEAD_W]
            partner = jnp.where(first, pltpu.roll(zg, HEAD_W - ROPE_AXIS_DIM // 2, 1),
                                pltpu.roll(zg, ROPE_AXIS_DIM // 2, 1))
            o_ref[0, :, g * HEAD_W:(g + 1) * HEAD_W] = zg * cos + partner * sin

    @pl.when(j >= 2)
    def _():
        o_ref[0] = z


def _in_proj(x, mod, w_bf16, rope_tabs, tm):
    B, T, _ = x.shape
    per_b = mod.shape[0] > 1
    rope = rope_tabs is not None
    in_specs = [pl.BlockSpec((1, tm, D_MODEL), lambda b, i, j: (b, i, 0)),
                pl.BlockSpec((1, 6, D_MODEL), (lambda b, i, j: (b, 0, 0)) if per_b else (lambda b, i, j: (0, 0, 0))),
                pl.BlockSpec((D_MODEL, D_MODEL), lambda b, i, j: (0, j))]
    args = [x, mod, w_bf16]
    if rope:
        in_specs += [pl.BlockSpec((tm, HEAD_W), lambda b, i, j: (i, 0))] * 2
        args += list(rope_tabs)
    return pl.pallas_call(
        functools.partial(_in_proj_body, rope=rope, tm=tm),
        grid=(B, T // tm, W_IN_COLS // D_MODEL),
        in_specs=in_specs,
        out_specs=pl.BlockSpec((1, tm, D_MODEL), lambda b, i, j: (b, i, j)),
        out_shape=jax.ShapeDtypeStruct((B, T, W_IN_COLS), F32),
        scratch_shapes=[pltpu.VMEM((tm, D_MODEL), BF16)],
        compiler_params=_cparams(("parallel", "parallel", "arbitrary")),
        name="in_proj_rope" if rope else "in_proj",
    )(*args)


def _rope_tables(T):
    t = np.arange(T)
    pos = np.stack([t // GRID_W, t % GRID_W], axis=1).astype(np.float32)
    lane = np.arange(HEAD_W)
    axis = (lane % ATT_QK_DIM) // ROPE_AXIS_DIM
    r = lane % ROPE_AXIS_DIM
    half = ROPE_AXIS_DIM // 2
    freqs = ROPE_BASE ** (-jnp.arange(0, ROPE_AXIS_DIM, 2, dtype=F32) / ROPE_AXIS_DIM)
    ang = jnp.asarray(pos)[:, axis] * freqs[r % half][None, :]
    sign = jnp.asarray(np.where(r < half, -1.0, 1.0).astype(np.float32))[None, :]
    return jnp.cos(ang), jnp.sin(ang) * sign


def _attn_body(lam_ref, q_ref, k_ref, v_ref, *refs, tq, tk, n_self, n_ctx, out_scale):
    if n_ctx:
        kc_ref, vc_ref, g_ref, o_ref, kb_ref, vb_ref = refs
    else:
        g_ref, o_ref, kb_ref, vb_ref = refs
    t_self = n_self * tk

    @pl.when(pl.program_id(2) == 0)
    def _():
        kb_ref[0:t_self, :] = k_ref[0].astype(BF16)
        vb_ref[0:t_self, :] = v_ref[0].astype(BF16)
        if n_ctx:
            kb_ref[t_self:, :] = kc_ref[0, 0].astype(BF16)
            vb_ref[t_self:, :] = vc_ref[0, 0].astype(BF16)

    q = q_ref[0] * (ATT_QK_DIM ** -0.5 * math.log2(math.e))
    lane = lax.broadcasted_iota(jnp.int32, (tq, HEAD_W), 1)
    lo = lane < ATT_QK_DIM
    qs = jnp.concatenate([jnp.where(lo, q, 0.0), jnp.where(lo, 0.0, q)], axis=0).astype(BF16)

    m = jnp.full((2 * tq, 1), -jnp.inf, F32)
    l = jnp.zeros((2 * tq, 1), F32)
    acc = jnp.zeros((2 * tq, HEAD_W), F32)
    for c in range(n_self + n_ctx):
        s = _dot_nt(qs, kb_ref[c * tk:(c + 1) * tk, :])
        m_new = jnp.maximum(m, jnp.max(s, axis=-1, keepdims=True))
        a = jnp.exp2(m - m_new)
        p = jnp.exp2(s - m_new)
        l = a * l + jnp.sum(p, axis=-1, keepdims=True)
        acc = a * acc + _dot(p.astype(BF16), vb_ref[c * tk:(c + 1) * tk, :])
        m = m_new
    o = acc / l
    o = o[:tq] - lam_ref[0] * o[tq:]
    o = o * lax.rsqrt(jnp.mean(o * o, axis=-1, keepdims=True) + LN_EPS) * g_ref[...] * out_scale
    o_ref[0] = o.astype(BF16)


def _attention(z, lam, subln, ctx_kv, layer, out_scale, tq, tk):
    B, T, _ = z.shape
    n_self = T // tk
    in_specs = [pl.BlockSpec(memory_space=pltpu.SMEM),
                pl.BlockSpec((1, tq, HEAD_W), lambda b, h, i: (b, i, COL_Q * N_HEADS + h)),
                pl.BlockSpec((1, T, HEAD_W), lambda b, h, i: (b, 0, COL_K * N_HEADS + h)),
                pl.BlockSpec((1, T, HEAD_W), lambda b, h, i: (b, 0, COL_V * N_HEADS + h))]
    args = [lam, z, z, z]
    t_all = T
    n_ctx = 0
    if ctx_kv is not None:
        P = ctx_kv[0].shape[2]
        assert P == tk
        n_ctx = 1
        t_all = T + P
        in_specs += [pl.BlockSpec((1, 1, P, HEAD_W), lambda b, h, i: (b, layer, 0, h))] * 2
        args += list(ctx_kv)
    in_specs.append(pl.BlockSpec((1, HEAD_W), lambda b, h, i: (0, 0)))
    args.append(subln)
    return pl.pallas_call(
        functools.partial(_attn_body, tq=tq, tk=tk, n_self=n_self, n_ctx=n_ctx, out_scale=out_scale),
        grid=(B, N_HEADS, T // tq),
        in_specs=in_specs,
        out_specs=pl.BlockSpec((1, tq, HEAD_W), lambda b, h, i: (b, i, h)),
        out_shape=jax.ShapeDtypeStruct((B, T, D_MODEL), BF16),
        scratch_shapes=[pltpu.VMEM((t_all, HEAD_W), BF16), pltpu.VMEM((t_all, HEAD_W), BF16)],
        compiler_params=_cparams(("parallel", "parallel", "arbitrary")),
        name="attention_ctx" if n_ctx else "attention",
    )(*args)


def _rec_constants():
    C = REC_C
    t = np.arange(C)[:, None]
    j = np.arange(C)[None, :]
    blocks = [(j <= t), (j > t)]
    for l in range(REC_LEVELS):
        m = 1 << l
        seg0 = (t // m) * m
        odd = ((t // m) % 2) == 1
        blocks.append(np.where(odd, (j >= seg0) & (j <= t), (j > t) & (j <= seg0 + m - 1)))
    mf = np.concatenate(blocks, axis=0).astype(np.float32)
    x = t ^ j
    lv = np.where(x == 0, REC_LEVELS, np.floor(np.log2(np.maximum(x, 1))).astype(np.int64))
    lvf = np.where(j <= t, lv, REC_LEVELS + 1).astype(np.int32)
    mb = mf.reshape(-1, C, C)[:, ::-1, ::-1].reshape(-1, C)
    lvb = lvf[::-1, ::-1]
    mf, mb = np.concatenate([mf, mf], axis=1), np.concatenate([mb, mb], axis=1)
    return (jnp.asarray(mf, BF16), jnp.asarray(np.ascontiguousarray(mb), BF16),
            jnp.asarray(lvf), jnp.asarray(np.ascontiguousarray(lvb)))


def _rec_pair(rq, ri, fx, lb, m_ref, lv, sts, backward):
    C = REC_C
    e = jnp.exp(-jnp.abs(fx))
    r = 1.0 / (1.0 + e)
    pos = fx >= 0
    sig = jnp.where(pos, r, e * r)
    nsig = jnp.where(pos, e * r, r)
    logf = jnp.log(lb + (1.0 - lb) * sig)
    kk = (1.0 - lb) * nsig
    q = rq * _sigmoid(rq)
    hi = logf.astype(BF16)
    mid = (logf - hi.astype(F32)).astype(BF16)
    lf2 = jnp.concatenate([hi, mid], axis=0)

    def expo(blk):
        return _dot(m_ref[blk * C:(blk + 1) * C, :], lf2)

    row = lax.broadcasted_iota(jnp.int32, (C, 2 * HEAD_W), 0)
    if backward:
        row = (C - 1) - row
    qb = q.astype(BF16)
    kb = kk.astype(BF16)
    ws = []
    for l in range(REC_LEVELS):
        odd = ((row >> l) & 1) == 1
        ws.append((jnp.exp(expo(2 + l)) * jnp.where(odd, q, kk)).astype(BF16))
    b_incl = expo(0)
    qi = (q * jnp.exp(b_incl)).astype(BF16)
    ki = (kk * jnp.exp(expo(1))).astype(BF16)
    dec = jnp.exp(b_incl[0:1, :] if backward else b_incl[C - 1:C, :])
    outs, new_sts = [], []
    for g in range(2):
        ls = slice(g * HEAD_W, (g + 1) * HEAD_W)
        a = jnp.where(lv == REC_LEVELS, _dot_nt(qb[:, ls], kb[:, ls]), 0.0)
        for l in range(REC_LEVELS):
            a = jnp.where(lv == l, _dot_nt(ws[l][:, ls], ws[l][:, ls]), a)
        rig = ri[:, ls]
        outs.append(_dot(a.astype(BF16), rig.astype(BF16)) + _dot_nt(qi[:, ls], sts[g].astype(BF16)))
        new_sts.append(sts[g] * dec[:, ls] + _dot(rig.T.astype(BF16), ki[:, ls]))
    return jnp.concatenate(outs, axis=1), new_sts


def _hgrn_body(mf_ref, mb_ref, lvf_ref, lvb_ref, rqf_ref, rif_ref, ff_ref, rqb_ref, rib_ref, fb_ref, lb_ref,
               *refs, has_s0):
    if has_s0:
        s0_ref, of_ref, ob_ref, so_ref, sf_scr, sb_scr = refs
    else:
        of_ref, ob_ref, so_ref, sf_scr, sb_scr = refs
    c = pl.program_id(2)

    @pl.when(c == 0)
    def _():
        for g in range(REC_HEADS_PER_STEP):
            if has_s0:
                sf_scr[g] = s0_ref[0, 0, g].T
                sb_scr[g] = s0_ref[0, 1, g].T
            else:
                sf_scr[g] = jnp.zeros((HEAD_W, HEAD_W), F32)
                sb_scr[g] = jnp.zeros((HEAD_W, HEAD_W), F32)

    for g in range(0, REC_HEADS_PER_STEP, 2):
        ls = slice(g * HEAD_W, (g + 2) * HEAD_W)
        o, st = _rec_pair(rqf_ref[0, :, ls], rif_ref[0, :, ls], ff_ref[0, :, ls], lb_ref[0:1, ls], mf_ref,
                          lvf_ref[...], [sf_scr[g], sf_scr[g + 1]], False)
        of_ref[0, :, ls] = o
        sf_scr[g] = st[0]
        sf_scr[g + 1] = st[1]
        o, st = _rec_pair(rqb_ref[0, :, ls], rib_ref[0, :, ls], fb_ref[0, :, ls], lb_ref[1:2, ls], mb_ref,
                          lvb_ref[...], [sb_scr[g], sb_scr[g + 1]], True)
        ob_ref[0, :, ls] = o
        sb_scr[g] = st[0]
        sb_scr[g + 1] = st[1]

    @pl.when(c == pl.num_programs(2) - 1)
    def _():
        for g in range(REC_HEADS_PER_STEP):
            so_ref[0, 0, g] = sf_scr[g].T
            so_ref[0, 1, g] = sb_scr[g].T


def _hgrn(z, lb, s0, layer, consts):
    B, T, _ = z.shape
    n = T // REC_C
    C = REC_C
    G = REC_HEADS_PER_STEP
    ng = N_HEADS // G
    W = G * HEAD_W

    def fwd(col):
        return pl.BlockSpec((1, C, W), lambda b, h, c: (b, c, col * ng + h))

    def bwd(col):
        return pl.BlockSpec((1, C, W), lambda b, h, c: (b, n - 1 - c, col * ng + h))

    const2 = lambda b, h, c: (0, 0)
    in_specs = [pl.BlockSpec(((2 + REC_LEVELS) * C, 2 * C), const2), pl.BlockSpec(((2 + REC_LEVELS) * C, 2 * C), const2),
                pl.BlockSpec((C, C), const2), pl.BlockSpec((C, C), const2),
                fwd(COL_RQ), fwd(COL_RI), fwd(COL_RFF), bwd(COL_RQ), bwd(COL_RI), bwd(COL_RFB),
                pl.BlockSpec((2, W), lambda b, h, c: (0, h))]
    args = list(consts) + [z] * 6 + [lb]
    if s0 is not None:
        s0v = s0.reshape(B, DEPTH * 2, N_HEADS, HEAD_W, HEAD_W)
        in_specs.append(pl.BlockSpec((1, 2, G, HEAD_W, HEAD_W), lambda b, h, c: (b, layer, h, 0, 0)))
        args.append(s0v)
    return pl.pallas_call(
        functools.partial(_hgrn_body, has_s0=s0 is not None),
        grid=(B, ng, n),
        in_specs=in_specs,
        out_specs=[pl.BlockSpec((1, C, W), lambda b, h, c: (b, c, h)),
                   pl.BlockSpec((1, C, W), lambda b, h, c: (b, n - 1 - c, h)),
                   pl.BlockSpec((1, 2, G, HEAD_W, HEAD_W), lambda b, h, c: (b, 0, h, 0, 0))],
        out_shape=[jax.ShapeDtypeStruct((B, T, D_MODEL), F32), jax.ShapeDtypeStruct((B, T, D_MODEL), F32),
                   jax.ShapeDtypeStruct((B, 2, N_HEADS, HEAD_W, HEAD_W), F32)],
        scratch_shapes=[pltpu.VMEM((G, HEAD_W, HEAD_W), F32), pltpu.VMEM((G, HEAD_W, HEAD_W), F32)],
        compiler_params=_cparams(("parallel", "parallel", "arbitrary")),
        name="hgrn_ctx" if s0 is not None else "hgrn",
    )(*args)


def _merge_body(oatt_ref, zb_ref, zc_ref, zx_ref, zcp_ref, zxp_ref, zcn_ref, zxn_ref, of_ref, ob_ref, rg_ref,
                g0_ref, g1_ref, g2_ref, x_ref, mod_ref, wb_ref, wo_ref, cw_ref, cb_ref, rn_ref, ln_ref,
                x1_ref, h2_ref, *, tm):
    i = pl.program_id(1)
    u = zc_ref[0] * zx_ref[0]
    row = lax.broadcasted_iota(jnp.int32, (tm, D_MODEL), 0)
    prev_ok = (i > 0).astype(F32)
    next_ok = (i < pl.num_programs(1) - 1).astype(F32)
    u_prev_edge = zcp_ref[0, 7:8, :] * zxp_ref[0, 7:8, :] * prev_ok
    u_next_edge = zcn_ref[0, 0:1, :] * zxn_ref[0, 0:1, :] * next_ok
    up = jnp.where(row == 0, u_prev_edge, pltpu.roll(u, 1, 0))
    un = jnp.where(row == tm - 1, u_next_edge, pltpu.roll(u, tm - 1, 0))
    conv = up * cw_ref[0:1, :] + u * cw_ref[1:2, :] + un * cw_ref[2:3, :] + cb_ref[...]
    o_conv = (zb_ref[0] * conv).astype(BF16)
    s = of_ref[0] + ob_ref[0]
    rg = rg_ref[0]
    parts = []
    for h in range(N_HEADS):
        sh = s[:, h * HEAD_W:(h + 1) * HEAD_W]
        parts.append(sh * lax.rsqrt(jnp.mean(sh * sh, axis=-1, keepdims=True) + LN_EPS))
    o_rec = (jnp.concatenate(parts, axis=1) * rn_ref[...] * (rg * _sigmoid(rg))).astype(BF16)
    merged = (_sigmoid(g0_ref[0]) * _dot(oatt_ref[0], wb_ref[0])
              + _sigmoid(g1_ref[0]) * _dot(o_conv, wb_ref[1])
              + _sigmoid(g2_ref[0]) * _dot(o_rec, wb_ref[2]))
    mix = _dot(merged.astype(BF16), wo_ref[...])
    y = DEEPNORM_ALPHA * x_ref[0] + mod_ref[0, 2:3, :] * mix
    x1 = _layer_norm(y, ln_ref[0:1, :], ln_ref[1:2, :])
    x1_ref[0] = x1
    h2_ref[0] = (x1 * (1.0 + mod_ref[0, 4:5, :]) + mod_ref[0, 3:4, :]).astype(BF16)


def _merge(x, z, oatt, o_f, o_b, mod, wb_bf16, wo_bf16, conv_w, conv_b, rec_norm_t, ln_gb, tm):
    B, T, _ = x.shape
    per_b = mod.shape[0] > 1
    nb8 = tm // 8
    last8 = T // 8 - 1

    def col(c):
        return pl.BlockSpec((1, tm, D_MODEL), lambda b, i: (b, i, c))

    def prev8(c):
        return pl.BlockSpec((1, 8, D_MODEL), lambda b, i: (b, jnp.maximum(i * nb8 - 1, 0), c))

    def next8(c):
        return pl.BlockSpec((1, 8, D_MODEL), lambda b, i: (b, jnp.minimum((i + 1) * nb8, last8), c))

    tile = pl.BlockSpec((1, tm, D_MODEL), lambda b, i: (b, i, 0))
    full2 = lambda b, i: (0, 0)
    in_specs = [tile, col(COL_CB), col(COL_CC), col(COL_CX), prev8(COL_CC), prev8(COL_CX), next8(COL_CC), next8(COL_CX),
                tile, tile, col(COL_RG), col(COL_G), col(COL_G + 1), col(COL_G + 2), tile,
                pl.BlockSpec((1, 6, D_MODEL), (lambda b, i: (b, 0, 0)) if per_b else (lambda b, i: (0, 0, 0))),
                pl.BlockSpec((3, D_MODEL, D_MODEL), lambda b, i: (0, 0, 0)),
                pl.BlockSpec((D_MODEL, D_MODEL), full2),
                pl.BlockSpec((3, D_MODEL), full2), pl.BlockSpec((1, D_MODEL), full2),
                pl.BlockSpec((1, D_MODEL), full2), pl.BlockSpec((2, D_MODEL), full2)]
    return pl.pallas_call(
        functools.partial(_merge_body, tm=tm),
        grid=(B, T // tm),
        in_specs=in_specs,
        out_specs=[tile, tile],
        out_shape=[jax.ShapeDtypeStruct((B, T, D_MODEL), F32), jax.ShapeDtypeStruct((B, T, D_MODEL), BF16)],
        compiler_params=_cparams(("parallel", "arbitrary")),
        name="merge",
    )(oatt, z, z, z, z, z, z, z, o_f, o_b, z, z, z, z, x, mod, wb_bf16, wo_bf16, conv_w, conv_b, rec_norm_t, ln_gb)


def _top_rows(s, k):
    n, w = s.shape
    rid = lax.broadcasted_iota(jnp.int32, (n, w), 0).astype(F32)
    rank = jnp.full((n, w), float(k), F32)
    vals = []
    for r in range(k):
        m = jnp.max(s, axis=0, keepdims=True)
        first = jnp.min(jnp.where(s == m, rid, float(n)), axis=0, keepdims=True)
        hit = rid == first
        s = jnp.where(hit, -jnp.inf, s)
        rank = jnp.where(hit, float(r), rank)
        vals.append(m)
    return vals, rank


def _route_body(h_ref, wq_ref, keys_ref, n1_ref, e1_ref, r2_ref, e2_ref, q_scr, s_scr, *, tt):
    q_scr[...] = _dot(h_ref[0], wq_ref[...]).astype(BF16)

    def head(h, carry):
        for p in range(2):
            off = pl.multiple_of((2 * h + p) * HEAD_W, HEAD_W)
            s_scr[p] = _dot_nt(keys_ref[2 * h + p], q_scr[:, pl.ds(off, HEAD_W)])
        for lc in range(tt // HEAD_W):
            ls = slice(lc * HEAD_W, (lc + 1) * HEAD_W)
            s1 = s_scr[0, :, ls]
            s2 = s_scr[1, :, ls]
            v1, r1 = _top_rows(s1, PEER_TOPK)
            v2, r2 = _top_rows(s2, PEER_TOPK)
            v2a = jnp.concatenate(v2, axis=0)
            cands = [v1[0] + v2a] + [v1[a] + v2a[:8] for a in range(1, 8)] + [jnp.concatenate(v1[8:], axis=0) + v2[0]]
            cand = jnp.concatenate(cands, axis=0)
            _, rc = _top_rows(cand, PEER_TOPK)
            sel = rc < PEER_TOPK
            zsum = jnp.sum(jnp.where(sel, jnp.exp(cand - (v1[0] + v2[0])), 0.0), axis=0, keepdims=True)
            selc = jnp.where(sel, 1.0, 0.0)
            n_a = [jnp.sum(selc[0:16], axis=0, keepdims=True)]
            n_a += [jnp.sum(selc[8 + 8 * a:16 + 8 * a], axis=0, keepdims=True) for a in range(1, 8)]
            n_a += [selc[72 + a:73 + a] for a in range(8)]
            n1 = jnp.zeros_like(s1)
            for a in range(PEER_TOPK):
                n1 = jnp.where(r1 == a, n_a[a], n1)
            n1_ref[h, :, ls] = n1
            e1_ref[h, :, ls] = jnp.exp(s1 - v1[0]) * (0.5 / zsum)
            r2_ref[h, :, ls] = r2.astype(BF16)
            e2_ref[h, :, ls] = jnp.exp(s2 - v2[0]).astype(BF16)
        return carry

    lax.fori_loop(0, N_HEADS, head, 0)


def _peer_route(h2, wq_bf16, keys_bf16, tt):
    B, T, _ = h2.shape
    big = pl.BlockSpec((None, N_HEADS, PEER_N_KEYS, tt), lambda b, i: (b, 0, 0, i))
    f32_shape = jax.ShapeDtypeStruct((B, N_HEADS, PEER_N_KEYS, T), F32)
    bf16_shape = jax.ShapeDtypeStruct((B, N_HEADS, PEER_N_KEYS, T), BF16)
    return pl.pallas_call(
        functools.partial(_route_body, tt=tt),
        grid=(B, T // tt),
        in_specs=[pl.BlockSpec((1, tt, D_MODEL), lambda b, i: (b, i, 0)),
                  pl.BlockSpec((D_MODEL, 2 * N_HEADS * HEAD_W), lambda b, i: (0, 0)),
                  pl.BlockSpec((2 * N_HEADS, PEER_N_KEYS, HEAD_W), lambda b, i: (0, 0, 0))],
        out_specs=[big, big, big, big],
        out_shape=[f32_shape, f32_shape, bf16_shape, bf16_shape],
        scratch_shapes=[pltpu.VMEM((tt, 2 * N_HEADS * HEAD_W), BF16), pltpu.VMEM((2, PEER_N_KEYS, tt), F32)],
        compiler_params=_cparams(("parallel", "parallel")),
        name="peer_route",
    )(h2, wq_bf16, keys_bf16)


PEER_STEP_KEYS = 8


def _dense_body(h_ref, u_ref, vt_ref, n1_ref, e1_ref, r2_ref, e2_ref, x_ref, mod_ref, ln_ref, o_ref,
                act_scr, w_scr, acc_scr, *, tt):
    k = pl.program_id(2)

    @pl.when(k == 0)
    def _():
        acc_scr[...] = jnp.zeros_like(acc_scr)

    hb = h_ref[0]
    sub = 2 * PEER_N_KEYS
    for jj in range(PEER_STEP_KEYS // 2):
        act_scr[jj * sub:(jj + 1) * sub, :] = _dot_nt(u_ref[jj * sub:(jj + 1) * sub, :], hb)
    for jj in range(PEER_STEP_KEYS // 2):
        for j2 in range(2):
            j = 2 * jj + j2
            for lc in range(tt // HEAD_W):
                ls = slice(lc * HEAD_W, (lc + 1) * HEAD_W)
                g = None
                for h in range(N_HEADS):
                    n_t = jnp.broadcast_to(n1_ref[h, j:j + 1, ls], (16, HEAD_W)).astype(BF16)[None]
                    e_t = jnp.broadcast_to(e1_ref[h, j:j + 1, ls], (16, HEAD_W)).astype(BF16)[None]
                    hit = r2_ref[h, :, ls].reshape(PEER_N_KEYS // 16, 16, HEAD_W) < n_t
                    term = jnp.where(hit, e2_ref[h, :, ls].reshape(PEER_N_KEYS // 16, 16, HEAD_W), 0.0) * e_t
                    g = term if g is None else g + term
                a = act_scr[j * PEER_N_KEYS:(j + 1) * PEER_N_KEYS, ls]
                gelu2 = a * (1.0 + lax.erf(a * (2.0 ** -0.5)))
                gf = g.reshape(PEER_N_KEYS, HEAD_W).astype(F32)
                w_scr[j * PEER_N_KEYS:(j + 1) * PEER_N_KEYS, ls] = (gelu2 * gf).astype(BF16)
        acc_scr[...] += _dot(vt_ref[:, jj * sub:(jj + 1) * sub], w_scr[jj * sub:(jj + 1) * sub, :])

    @pl.when(k == pl.num_programs(2) - 1)
    def _():
        y = DEEPNORM_ALPHA * x_ref[0] + mod_ref[0, 5:6, :] * acc_scr[...].T
        o_ref[0] = _layer_norm(y, ln_ref[0:1, :], ln_ref[1:2, :])


def _peer_dense(h2, x1, mod, u_bf16, vt_bf16, route, ln_gb, tt):
    B, T, _ = h2.shape
    n1, e1, r2, e2 = route
    per_b = mod.shape[0] > 1
    ne = PEER_STEP_KEYS * PEER_N_KEYS
    rows = pl.BlockSpec((None, N_HEADS, PEER_STEP_KEYS, tt), lambda b, i, k: (b, 0, k, i))
    full = pl.BlockSpec((None, N_HEADS, PEER_N_KEYS, tt), lambda b, i, k: (b, 0, 0, i))
    return pl.pallas_call(
        functools.partial(_dense_body, tt=tt),
        grid=(B, T // tt, PEER_N_KEYS // PEER_STEP_KEYS),
        in_specs=[pl.BlockSpec((1, tt, D_MODEL), lambda b, i, k: (b, i, 0)),
                  pl.BlockSpec((ne, D_MODEL), lambda b, i, k: (k, 0)),
                  pl.BlockSpec((D_MODEL, ne), lambda b, i, k: (0, k)),
                  rows, rows, full, full,
                  pl.BlockSpec((1, tt, D_MODEL), lambda b, i, k: (b, i, 0)),
                  pl.BlockSpec((1, 6, D_MODEL), (lambda b, i, k: (b, 0, 0)) if per_b else (lambda b, i, k: (0, 0, 0))),
                  pl.BlockSpec((2, D_MODEL), lambda b, i, k: (0, 0))],
        out_specs=pl.BlockSpec((1, tt, D_MODEL), lambda b, i, k: (b, i, 0)),
        out_shape=jax.ShapeDtypeStruct((B, T, D_MODEL), F32),
        scratch_shapes=[pltpu.VMEM((ne, tt), F32), pltpu.VMEM((ne, tt), BF16), pltpu.VMEM((D_MODEL, tt), F32)],
        compiler_params=_cparams(("parallel", "parallel", "arbitrary")),
        name="peer_dense",
    )(h2, u_bf16, vt_bf16, n1, e1, r2, e2, x1, mod, ln_gb)


def _trunk_layer(x, mod, layer, p, ctx, rope_tabs, consts, flat_rows):
    B, T, _ = x.shape
    if flat_rows is not None:
        z = _in_proj(x.reshape(-1, flat_rows, D_MODEL), mod, p["w_in"], None, tm=flat_rows).reshape(B, T, W_IN_COLS)
    else:
        z = _in_proj(x, mod, p["w_in"], rope_tabs, tm=1024)
    tq = 256
    if ctx is None:
        oatt = _attention(z, p["lam"], p["subln"], None, layer, p["att_scale"], tq=tq, tk=T)
        o_f, o_b, s_fin = _hgrn(z, p["lb"], None, layer, consts)
    else:
        oatt = _attention(z, p["lam"], p["subln"], (ctx[0], ctx[1]), layer, p["att_scale"], tq=tq, tk=512)
        o_f, o_b, s_fin = _hgrn(z, p["lb"], ctx[2], layer, consts)
    x1, h2 = _merge(x, z, oatt, o_f, o_b, mod, p["w_branch"], p["w_out"], p["conv_w"], p["conv_b"],
                    p["rec_norm"], p["ln0"], tm=256)
    if flat_rows is not None:
        h2r, x1r = h2.reshape(-1, flat_rows, D_MODEL), x1.reshape(-1, flat_rows, D_MODEL)
    else:
        h2r, x1r = h2, x1
    tt = 512
    route = _peer_route(h2r, p["peer_wq"], p["peer_keys"], tt)
    x2 = _peer_dense(h2r, x1r, mod, p["peer_u"], p["peer_vt"], route, p["ln1"], tt).reshape(B, T, D_MODEL)
    return x2, z, s_fin


def _layer_params(l, lb_all, w_in, attn_lambda, attn_subln, conv_w, conv_b, rec_norm, w_branch, w_out, ln_g, ln_b,
                  peer_wq, peer_keys, peer_u, peer_v):
    lam_init = 0.8 - 0.6 * math.exp(-0.3 * l)
    lp = attn_lambda[l].astype(F32)
    lam = jnp.exp(jnp.sum(lp[0] * lp[1])) - jnp.exp(jnp.sum(lp[2] * lp[3])) + lam_init
    return {
        "w_in": w_in[l].astype(BF16), "lam": lam.reshape(1), "att_scale": 1.0 - lam_init,
        "subln": attn_subln[l].reshape(1, HEAD_W), "lb": lb_all[l],
        "w_branch": w_branch[l].astype(BF16), "w_out": w_out[l].astype(BF16),
        "conv_w": conv_w[l], "conv_b": conv_b[l].reshape(1, D_MODEL),
        "rec_norm": jnp.tile(rec_norm[l], N_HEADS).reshape(1, D_MODEL),
        "ln0": jnp.stack([ln_g[l, 0], ln_b[l, 0]]), "ln1": jnp.stack([ln_g[l, 1], ln_b[l, 1]]),
        "peer_wq": peer_wq[l].astype(BF16),
        "peer_keys": peer_keys[l].astype(BF16).reshape(2 * N_HEADS, PEER_N_KEYS, HEAD_W),
        "peer_u": peer_u[l].astype(BF16), "peer_vt": peer_v[l].astype(BF16).T,
    }


def kernel(x_prompt, x_sample, c, cache_attn_k, cache_attn_v, state_hgrn, c_ctx, mod_w, mod_b, w_in, attn_lambda,
           attn_subln, conv_w, conv_b, rec_lb, rec_norm, w_branch, w_out, ln_g, ln_b, peer_wq, peer_keys, peer_u,
           peer_v):
    B, T, _ = x_prompt.shape
    Bs, Ts, _ = x_sample.shape
    P = cache_attn_k.shape[2]
    lb_all = jnp.cumsum(jax.nn.softmax(rec_lb.astype(F32), axis=0), axis=0)
    lb_all = lb_all - lb_all[:1]
    cmat = jnp.concatenate([c_ctx[None, :], c, jnp.zeros((8 - 1 - Bs, D_MODEL), F32)], axis=0)
    mods = _mod_vectors(cmat, mod_w, mod_b).reshape(DEPTH, 8, 6, D_MODEL)
    rope_tabs = _rope_tables(Ts)
    consts = _rec_constants()
    ck = cache_attn_k.reshape(Bs, DEPTH, P, N_HEADS * HEAD_W)
    cv = cache_attn_v.reshape(Bs, DEPTH, P, N_HEADS * HEAD_W)

    y_p, y_s = x_prompt, x_sample
    ks, vs, ss = [], [], []
    for l in range(DEPTH):
        p = _layer_params(l, lb_all, w_in, attn_lambda, attn_subln, conv_w, conv_b, rec_norm, w_branch, w_out,
                          ln_g, ln_b, peer_wq, peer_keys, peer_u, peer_v)
        y_p, z_p, s_p = _trunk_layer(y_p, mods[l, 0:1], l, p, None, None, consts, flat_rows=1024)
        ks.append(z_p[..., COL_K * D_MODEL:(COL_K + 1) * D_MODEL].reshape(B, T, N_HEADS, HEAD_W))
        vs.append(z_p[..., COL_V * D_MODEL:(COL_V + 1) * D_MODEL].reshape(B, T, N_HEADS, HEAD_W))
        ss.append(s_p)
        y_s, _, _ = _trunk_layer(y_s, mods[l, 1:1 + Bs], l, p, (ck, cv, state_hgrn), rope_tabs, consts, flat_rows=None)
    return (y_p, y_s, jnp.stack(ks, axis=1), jnp.stack(vs, axis=1), jnp.stack(ss, axis=1))
```

```python
import functools
import math

import numpy as np
import jax
import jax.numpy as jnp
from jax import lax
from jax.experimental import pallas as pl
from jax.experimental.pallas import tpu as pltpu

F32 = jnp.float32
BF16 = jnp.bfloat16

D_MODEL = 1024
DEPTH = 2
GRID_W = 64
N_HEADS = 8
HEAD_W = 128
ATT_QK_DIM = 64
ROPE_BASE = 10000.0
ROPE_AXIS_DIM = ATT_QK_DIM // 2
PEER_N_KEYS = 128
PEER_TOPK = 16
LN_EPS = 1e-5
DEEPNORM_ALPHA = (2 * DEPTH) ** 0.25
W_IN_COLS = 14 * D_MODEL
COL_Q, COL_K, COL_V, COL_CB, COL_CC, COL_CX, COL_RQ, COL_RFF, COL_RFB, COL_RI, COL_RG, COL_G = range(12)

VMEM_LIMIT = 56 * 1024 * 1024
REC_C = 128
REC_LEVELS = 7
REC_HEADS_PER_STEP = 4


def _cparams(sem):
    return pltpu.CompilerParams(dimension_semantics=sem, vmem_limit_bytes=VMEM_LIMIT)


def _dot(a, b):
    return jnp.dot(a, b, preferred_element_type=F32)


def _dot_nt(a, b):
    return lax.dot_general(a, b, (((1,), (1,)), ((), ())), preferred_element_type=F32)


def _sigmoid(x):
    e = jnp.exp(-jnp.abs(x))
    r = 1.0 / (1.0 + e)
    return jnp.where(x >= 0, r, e * r)


def _layer_norm(y, g, b):
    mu = jnp.mean(y, axis=-1, keepdims=True)
    yc = y - mu
    var = jnp.mean(yc * yc, axis=-1, keepdims=True)
    return yc * lax.rsqrt(var + LN_EPS) * g + b


def _mod_body(c_ref, w_ref, b_ref, o_ref):
    c = c_ref[...]
    s = (c * _sigmoid(c)).astype(BF16)
    o_ref[0] = _dot(s, w_ref[0].astype(BF16)) + b_ref[0]


def _mod_vectors(cmat, mod_w, mod_b):
    tn = 1536
    return pl.pallas_call(
        _mod_body,
        grid=(DEPTH, 6 * D_MODEL // tn),
        in_specs=[pl.BlockSpec((8, D_MODEL), lambda l, j: (0, 0)),
                  pl.BlockSpec((1, D_MODEL, tn), lambda l, j: (l, 0, j)),
                  pl.BlockSpec((1, 1, tn), lambda l, j: (l, 0, j))],
        out_specs=pl.BlockSpec((1, 8, tn), lambda l, j: (l, 0, j)),
        out_shape=jax.ShapeDtypeStruct((DEPTH, 8, 6 * D_MODEL), F32),
        compiler_params=_cparams(("parallel", "parallel")),
        name="mod_vectors",
    )(cmat, mod_w, mod_b.reshape(DEPTH, 1, 6 * D_MODEL))


def _in_proj_body(x_ref, mod_ref, w_ref, *refs, rope, tm):
    if rope:
        cos_ref, sin_ref, o_ref, h_ref = refs
    else:
        o_ref, h_ref = refs
    j = pl.program_id(2)

    @pl.when(j == 0)
    def _():
        h_ref[...] = (x_ref[0] * (1.0 + mod_ref[0, 1:2, :]) + mod_ref[0, 0:1, :]).astype(BF16)

    z = _dot(h_ref[...], w_ref[...])
    if not rope:
        o_ref[0] = z
        return

    @pl.when(j < 2)
    def _():
        cos = cos_ref[...]
        sin = sin_ref[...]
        lane = lax.broadcasted_iota(jnp.int32, (tm, HEAD_W), 1)
        first = (lane % ROPE_AXIS_DIM) < (ROPE_AXIS_DIM // 2)
        for g in range(N_HEADS):
            zg = z[:, g * HEAD_W:(g + 1) * HEAD_W]
            partner = jnp.where(first, pltpu.roll(zg, HEAD_W - ROPE_AXIS_DIM // 2, 1),
                                pltpu.roll(zg, ROPE_AXIS_DIM // 2, 1))
            o_ref[0, :, g * HEAD_W:(g + 1) * HEAD_W] = zg * cos + partner * sin

    @pl.when(j >= 2)
    def _():
        o_ref[0] = z


def _in_proj(x, mod, w_bf16, rope_tabs, tm):
    B, T, _ = x.shape
    per_b = mod.shape[0] > 1
    rope = rope_tabs is not None
    in_specs = [pl.BlockSpec((1, tm, D_MODEL), lambda b, i, j: (b, i, 0)),
                pl.BlockSpec((1, 6, D_MODEL), (lambda b, i, j: (b, 0, 0)) if per_b else (lambda b, i, j: (0, 0, 0))),
                pl.BlockSpec((D_MODEL, D_MODEL), lambda b, i, j: (0, j))]
    args = [x, mod, w_bf16]
    if rope:
        in_specs += [pl.BlockSpec((tm, HEAD_W), lambda b, i, j: (i, 0))] * 2
        args += list(rope_tabs)
    return pl.pallas_call(
        functools.partial(_in_proj_body, rope=rope, tm=tm),
        grid=(B, T // tm, W_IN_COLS // D_MODEL),
        in_specs=in_specs,
        out_specs=pl.BlockSpec((1, tm, D_MODEL), lambda b, i, j: (b, i, j)),
        out_shape=jax.ShapeDtypeStruct((B, T, W_IN_COLS), F32),
        scratch_shapes=[pltpu.VMEM((tm, D_MODEL), BF16)],
        compiler_params=_cparams(("parallel", "parallel", "arbitrary")),
        name="in_proj_rope" if rope else "in_proj",
    )(*args)


def _rope_tables(T):
    t = np.arange(T)
    pos = np.stack([t // GRID_W, t % GRID_W], axis=1).astype(np.float32)
    lane = np.arange(HEAD_W)
    axis = (lane % ATT_QK_DIM) // ROPE_AXIS_DIM
    r = lane % ROPE_AXIS_DIM
    half = ROPE_AXIS_DIM // 2
    freqs = ROPE_BASE ** (-jnp.arange(0, ROPE_AXIS_DIM, 2, dtype=F32) / ROPE_AXIS_DIM)
    ang = jnp.asarray(pos)[:, axis] * freqs[r % half][None, :]
    sign = jnp.asarray(np.where(r < half, -1.0, 1.0).astype(np.float32))[None, :]
    return jnp.cos(ang), jnp.sin(ang) * sign


def _attn_body(lam_ref, q_ref, k_ref, v_ref, *refs, tq, tk, n_self, n_ctx, out_scale):
    if n_ctx:
        kc_ref, vc_ref, g_ref, o_ref, kb_ref, vb_ref = refs
    else:
        g_ref, o_ref, kb_ref, vb_ref = refs
    t_self = n_self * tk

    @pl.when(pl.program_id(2) == 0)
    def _():
        kb_ref[0:t_self, :] = k_ref[0].astype(BF16)
        vb_ref[0:t_self, :] = v_ref[0].astype(BF16)
        if n_ctx:
            kb_ref[t_self:, :] = kc_ref[0, 0].astype(BF16)
            vb_ref[t_self:, :] = vc_ref[0, 0].astype(BF16)

    q = q_ref[0] * (ATT_QK_DIM ** -0.5 * math.log2(math.e))
    lane = lax.broadcasted_iota(jnp.int32, (tq, HEAD_W), 1)
    lo = lane < ATT_QK_DIM
    qs = jnp.concatenate([jnp.where(lo, q, 0.0), jnp.where(lo, 0.0, q)], axis=0).astype(BF16)

    m = jnp.full((2 * tq, 1), -jnp.inf, F32)
    l = jnp.zeros((2 * tq, 1), F32)
    acc = jnp.zeros((2 * tq, HEAD_W), F32)
    for c in range(n_self + n_ctx):
        s = _dot_nt(qs, kb_ref[c * tk:(c + 1) * tk, :])
        m_new = jnp.maximum(m, jnp.max(s, axis=-1, keepdims=True))
        a = jnp.exp2(m - m_new)
        p = jnp.exp2(s - m_new)
        l = a * l + jnp.sum(p, axis=-1, keepdims=True)
        acc = a * acc + _dot(p.astype(BF16), vb_ref[c * tk:(c + 1) * tk, :])
        m = m_new
    o = acc / l
    o = o[:tq] - lam_ref[0] * o[tq:]
    o = o * lax.rsqrt(jnp.mean(o * o, axis=-1, keepdims=True) + LN_EPS) * g_ref[...] * out_scale
    o_ref[0] = o.astype(BF16)


def _attention(z, lam, subln, ctx_kv, layer, out_scale, tq, tk):
    B, T, _ = z.shape
    n_self = T // tk
    in_specs = [pl.BlockSpec(memory_space=pltpu.SMEM),
                pl.BlockSpec((1, tq, HEAD_W), lambda b, h, i: (b, i, COL_Q * N_HEADS + h)),
                pl.BlockSpec((1, T, HEAD_W), lambda b, h, i: (b, 0, COL_K * N_HEADS + h)),
                pl.BlockSpec((1, T, HEAD_W), lambda b, h, i: (b, 0, COL_V * N_HEADS + h))]
    args = [lam, z, z, z]
    t_all = T
    n_ctx = 0
    if ctx_kv is not None:
        P = ctx_kv[0].shape[2]
        assert P == tk
        n_ctx = 1
        t_all = T + P
        in_specs += [pl.BlockSpec((1, 1, P, HEAD_W), lambda b, h, i: (b, layer, 0, h))] * 2
        args += list(ctx_kv)
    in_specs.append(pl.BlockSpec((1, HEAD_W), lambda b, h, i: (0, 0)))
    args.append(subln)
    return pl.pallas_call(
        functools.partial(_attn_body, tq=tq, tk=tk, n_self=n_self, n_ctx=n_ctx, out_scale=out_scale),
        grid=(B, N_HEADS, T // tq),
        in_specs=in_specs,
        out_specs=pl.BlockSpec((1, tq, HEAD_W), lambda b, h, i: (b, i, h)),
        out_shape=jax.ShapeDtypeStruct((B, T, D_MODEL), BF16),
        scratch_shapes=[pltpu.VMEM((t_all, HEAD_W), BF16), pltpu.VMEM((t_all, HEAD_W), BF16)],
        compiler_params=_cparams(("parallel", "parallel", "arbitrary")),
        name="attention_ctx" if n_ctx else "attention",
    )(*args)


def _rec_constants():
    C = REC_C
    t = np.arange(C)[:, None]
    j = np.arange(C)[None, :]
    blocks = [(j <= t), (j > t)]
    for l in range(REC_LEVELS):
        m = 1 << l
        seg0 = (t // m) * m
        odd = ((t // m) % 2) == 1
        blocks.append(np.where(odd, (j >= seg0) & (j <= t), (j > t) & (j <= seg0 + m - 1)))
    mf = np.concatenate(blocks, axis=0).astype(np.float32)
    x = t ^ j
    lv = np.where(x == 0, REC_LEVELS, np.floor(np.log2(np.maximum(x, 1))).astype(np.int64))
    lvf = np.where(j <= t, lv, REC_LEVELS + 1).astype(np.int32)
    mb = mf.reshape(-1, C, C)[:, ::-1, ::-1].reshape(-1, C)
    lvb = lvf[::-1, ::-1]
    mf, mb = np.concatenate([mf, mf], axis=1), np.concatenate([mb, mb], axis=1)
    return (jnp.asarray(mf, BF16), jnp.asarray(np.ascontiguousarray(mb), BF16),
            jnp.asarray(lvf), jnp.asarray(np.ascontiguousarray(lvb)))


def _rec_pair(rq, ri, fx, lb, m_ref, lv, sts, backward):
    C = REC_C
    e = jnp.exp(-jnp.abs(fx))
    r = 1.0 / (1.0 + e)
    pos = fx >= 0
    sig = jnp.where(pos, r, e * r)
    nsig = jnp.where(pos, e * r, r)
    logf = jnp.log(lb + (1.0 - lb) * sig)
    kk = (1.0 - lb) * nsig
    q = rq * _sigmoid(rq)
    hi = logf.astype(BF16)
    mid = (logf - hi.astype(F32)).astype(BF16)
    lf2 = jnp.concatenate([hi, mid], axis=0)

    def expo(blk):
        return _dot(m_ref[blk * C:(blk + 1) * C, :], lf2)

    row = lax.broadcasted_iota(jnp.int32, (C, 2 * HEAD_W), 0)
    if backward:
        row = (C - 1) - row
    qb = q.astype(BF16)
    kb = kk.astype(BF16)
    ws = []
    for l in range(REC_LEVELS):
        odd = ((row >> l) & 1) == 1
        ws.append((jnp.exp(expo(2 + l)) * jnp.where(odd, q, kk)).astype(BF16))
    b_incl = expo(0)
    qi = (q * jnp.exp(b_incl)).astype(BF16)
    ki = (kk * jnp.exp(expo(1))).astype(BF16)
    dec = jnp.exp(b_incl[0:1, :] if backward else b_incl[C - 1:C, :])
    outs, new_sts = [], []
    for g in range(2):
        ls = slice(g * HEAD_W, (g + 1) * HEAD_W)
        a = jnp.where(lv == REC_LEVELS, _dot_nt(qb[:, ls], kb[:, ls]), 0.0)
        for l in range(REC_LEVELS):
            a = jnp.where(lv == l, _dot_nt(ws[l][:, ls], ws[l][:, ls]), a)
        rig = ri[:, ls]
        outs.append(_dot(a.astype(BF16), rig.astype(BF16)) + _dot_nt(qi[:, ls], sts[g].astype(BF16)))
        new_sts.append(sts[g] * dec[:, ls] + _dot(rig.T.astype(BF16), ki[:, ls]))
    return jnp.concatenate(outs, axis=1), new_sts


def _hgrn_body(mf_ref, mb_ref, lvf_ref, lvb_ref, rqf_ref, rif_ref, ff_ref, rqb_ref, rib_ref, fb_ref, lb_ref,
               *refs, has_s0):
    if has_s0:
        s0_ref, of_ref, ob_ref, so_ref, sf_scr, sb_scr = refs
    else:
        of_ref, ob_ref, so_ref, sf_scr, sb_scr = refs
    c = pl.program_id(2)

    @pl.when(c == 0)
    def _():
        for g in range(REC_HEADS_PER_STEP):
            if has_s0:
                sf_scr[g] = s0_ref[0, 0, g].T
                sb_scr[g] = s0_ref[0, 1, g].T
            else:
                sf_scr[g] = jnp.zeros((HEAD_W, HEAD_W), F32)
                sb_scr[g] = jnp.zeros((HEAD_W, HEAD_W), F32)

    for g in range(0, REC_HEADS_PER_STEP, 2):
        ls = slice(g * HEAD_W, (g + 2) * HEAD_W)
        o, st = _rec_pair(rqf_ref[0, :, ls], rif_ref[0, :, ls], ff_ref[0, :, ls], lb_ref[0:1, ls], mf_ref,
                          lvf_ref[...], [sf_scr[g], sf_scr[g + 1]], False)
        of_ref[0, :, ls] = o
        sf_scr[g] = st[0]
        sf_scr[g + 1] = st[1]
        o, st = _rec_pair(rqb_ref[0, :, ls], rib_ref[0, :, ls], fb_ref[0, :, ls], lb_ref[1:2, ls], mb_ref,
                          lvb_ref[...], [sb_scr[g], sb_scr[g + 1]], True)
        ob_ref[0, :, ls] = o
        sb_scr[g] = st[0]
        sb_scr[g + 1] = st[1]

    @pl.when(c == pl.num_programs(2) - 1)
    def _():
        for g in range(REC_HEADS_PER_STEP):
            so_ref[0, 0, g] = sf_scr[g].T
            so_ref[0, 1, g] = sb_scr[g].T


def _hgrn(z, lb, s0, layer, consts):
    B, T, _ = z.shape
    n = T // REC_C
    C = REC_C
    G = REC_HEADS_PER_STEP
    ng = N_HEADS // G
    W = G * HEAD_W

    def fwd(col):
        return pl.BlockSpec((1, C, W), lambda b, h, c: (b, c, col * ng + h))

    def bwd(col):
        return pl.BlockSpec((1, C, W), lambda b, h, c: (b, n - 1 - c, col * ng + h))

    const2 = lambda b, h, c: (0, 0)
    in_specs = [pl.BlockSpec(((2 + REC_LEVELS) * C, 2 * C), const2), pl.BlockSpec(((2 + REC_LEVELS) * C, 2 * C), const2),
                pl.BlockSpec((C, C), const2), pl.BlockSpec((C, C), const2),
                fwd(COL_RQ), fwd(COL_RI), fwd(COL_RFF), bwd(COL_RQ), bwd(COL_RI), bwd(COL_RFB),
                pl.BlockSpec((2, W), lambda b, h, c: (0, h))]
    args = list(consts) + [z] * 6 + [lb]
    if s0 is not None:
        s0v = s0.reshape(B, DEPTH * 2, N_HEADS, HEAD_W, HEAD_W)
        in_specs.append(pl.BlockSpec((1, 2, G, HEAD_W, HEAD_W), lambda b, h, c: (b, layer, h, 0, 0)))
        args.append(s0v)
    return pl.pallas_call(
        functools.partial(_hgrn_body, has_s0=s0 is not None),
        grid=(B, ng, n),
        in_specs=in_specs,
        out_specs=[pl.BlockSpec((1, C, W), lambda b, h, c: (b, c, h)),
                   pl.BlockSpec((1, C, W), lambda b, h, c: (b, n - 1 - c, h)),
                   pl.BlockSpec((1, 2, G, HEAD_W, HEAD_W), lambda b, h, c: (b, 0, h, 0, 0))],
        out_shape=[jax.ShapeDtypeStruct((B, T, D_MODEL), F32), jax.ShapeDtypeStruct((B, T, D_MODEL), F32),
                   jax.ShapeDtypeStruct((B, 2, N_HEADS, HEAD_W, HEAD_W), F32)],
        scratch_shapes=[pltpu.VMEM((G, HEAD_W, HEAD_W), F32), pltpu.VMEM((G, HEAD_W, HEAD_W), F32)],
        compiler_params=_cparams(("parallel", "parallel", "arbitrary")),
        name="hgrn_ctx" if s0 is not None else "hgrn",
    )(*args)


def _merge_body(oatt_ref, zb_ref, zc_ref, zx_ref, zcp_ref, zxp_ref, zcn_ref, zxn_ref, of_ref, ob_ref, rg_ref,
                g0_ref, g1_ref, g2_ref, x_ref, mod_ref, wb_ref, wo_ref, cw_ref, cb_ref, rn_ref, ln_ref,
                x1_ref, h2_ref, *, tm):
    i = pl.program_id(1)
    u = zc_ref[0] * zx_ref[0]
    row = lax.broadcasted_iota(jnp.int32, (tm, D_MODEL), 0)
    prev_ok = (i > 0).astype(F32)
    next_ok = (i < pl.num_programs(1) - 1).astype(F32)
    u_prev_edge = zcp_ref[0, 7:8, :] * zxp_ref[0, 7:8, :] * prev_ok
    u_next_edge = zcn_ref[0, 0:1, :] * zxn_ref[0, 0:1, :] * next_ok
    up = jnp.where(row == 0, u_prev_edge, pltpu.roll(u, 1, 0))
    un = jnp.where(row == tm - 1, u_next_edge, pltpu.roll(u, tm - 1, 0))
    conv = up * cw_ref[0:1, :] + u * cw_ref[1:2, :] + un * cw_ref[2:3, :] + cb_ref[...]
    o_conv = (zb_ref[0] * conv).astype(BF16)
    s = of_ref[0] + ob_ref[0]
    rg = rg_ref[0]
    parts = []
    for h in range(N_HEADS):
        sh = s[:, h * HEAD_W:(h + 1) * HEAD_W]
        parts.append(sh * lax.rsqrt(jnp.mean(sh * sh, axis=-1, keepdims=True) + LN_EPS))
    o_rec = (jnp.concatenate(parts, axis=1) * rn_ref[...] * (rg * _sigmoid(rg))).astype(BF16)
    merged = (_sigmoid(g0_ref[0]) * _dot(oatt_ref[0], wb_ref[0])
              + _sigmoid(g1_ref[0]) * _dot(o_conv, wb_ref[1])
              + _sigmoid(g2_ref[0]) * _dot(o_rec, wb_ref[2]))
    mix = _dot(merged.astype(BF16), wo_ref[...])
    y = DEEPNORM_ALPHA * x_ref[0] + mod_ref[0, 2:3, :] * mix
    x1 = _layer_norm(y, ln_ref[0:1, :], ln_ref[1:2, :])
    x1_ref[0] = x1
    h2_ref[0] = (x1 * (1.0 + mod_ref[0, 4:5, :]) + mod_ref[0, 3:4, :]).astype(BF16)


def _merge(x, z, oatt, o_f, o_b, mod, wb_bf16, wo_bf16, conv_w, conv_b, rec_norm_t, ln_gb, tm):
    B, T, _ = x.shape
    per_b = mod.shape[0] > 1
    nb8 = tm // 8
    last8 = T // 8 - 1

    def col(c):
        return pl.BlockSpec((1, tm, D_MODEL), lambda b, i: (b, i, c))

    def prev8(c):
        return pl.BlockSpec((1, 8, D_MODEL), lambda b, i: (b, jnp.maximum(i * nb8 - 1, 0), c))

    def next8(c):
        return pl.BlockSpec((1, 8, D_MODEL), lambda b, i: (b, jnp.minimum((i + 1) * nb8, last8), c))

    tile = pl.BlockSpec((1, tm, D_MODEL), lambda b, i: (b, i, 0))
    full2 = lambda b, i: (0, 0)
    in_specs = [tile, col(COL_CB), col(COL_CC), col(COL_CX), prev8(COL_CC), prev8(COL_CX), next8(COL_CC), next8(COL_CX),
                tile, tile, col(COL_RG), col(COL_G), col(COL_G + 1), col(COL_G + 2), tile,
                pl.BlockSpec((1, 6, D_MODEL), (lambda b, i: (b, 0, 0)) if per_b else (lambda b, i: (0, 0, 0))),
                pl.BlockSpec((3, D_MODEL, D_MODEL), lambda b, i: (0, 0, 0)),
                pl.BlockSpec((D_MODEL, D_MODEL), full2),
                pl.BlockSpec((3, D_MODEL), full2), pl.BlockSpec((1, D_MODEL), full2),
                pl.BlockSpec((1, D_MODEL), full2), pl.BlockSpec((2, D_MODEL), full2)]
    return pl.pallas_call(
        functools.partial(_merge_body, tm=tm),
        grid=(B, T // tm),
        in_specs=in_specs,
        out_specs=[tile, tile],
        out_shape=[jax.ShapeDtypeStruct((B, T, D_MODEL), F32), jax.ShapeDtypeStruct((B, T, D_MODEL), BF16)],
        compiler_params=_cparams(("parallel", "arbitrary")),
        name="merge",
    )(oatt, z, z, z, z, z, z, z, o_f, o_b, z, z, z, z, x, mod, wb_bf16, wo_bf16, conv_w, conv_b, rec_norm_t, ln_gb)


def _top_rows(s, k):
    n, w = s.shape
    rid = lax.broadcasted_iota(jnp.int32, (n, w), 0).astype(F32)
    rank = jnp.full((n, w), float(k), F32)
    vals = []
    for r in range(k):
        m = jnp.max(s, axis=0, keepdims=True)
        first = jnp.min(jnp.where(s == m, rid, float(n)), axis=0, keepdims=True)
        hit = rid == first
        s = jnp.where(hit, -jnp.inf, s)
        rank = jnp.where(hit, float(r), rank)
        vals.append(m)
    return vals, rank


def _route_body(h_ref, wq_ref, keys_ref, n1_ref, e1_ref, r2_ref, e2_ref, q_scr, s_scr, *, tt):
    q_scr[...] = _dot(h_ref[0], wq_ref[...]).astype(BF16)

    def head(h, carry):
        for p in range(2):
            off = pl.multiple_of((2 * h + p) * HEAD_W, HEAD_W)
            s_scr[p] = _dot_nt(keys_ref[2 * h + p], q_scr[:, pl.ds(off, HEAD_W)])
        for lc in range(tt // HEAD_W):
            ls = slice(lc * HEAD_W, (lc + 1) * HEAD_W)
            s1 = s_scr[0, :, ls]
            s2 = s_scr[1, :, ls]
            v1, r1 = _top_rows(s1, PEER_TOPK)
            v2, r2 = _top_rows(s2, PEER_TOPK)
            v2a = jnp.concatenate(v2, axis=0)
            cands = [v1[0] + v2a] + [v1[a] + v2a[:8] for a in range(1, 8)] + [jnp.concatenate(v1[8:], axis=0) + v2[0]]
            cand = jnp.concatenate(cands, axis=0)
            _, rc = _top_rows(cand, PEER_TOPK)
            sel = rc < PEER_TOPK
            zsum = jnp.sum(jnp.where(sel, jnp.exp(cand - (v1[0] + v2[0])), 0.0), axis=0, keepdims=True)
            selc = jnp.where(sel, 1.0, 0.0)
            n_a = [jnp.sum(selc[0:16], axis=0, keepdims=True)]
            n_a += [jnp.sum(selc[8 + 8 * a:16 + 8 * a], axis=0, keepdims=True) for a in range(1, 8)]
            n_a += [selc[72 + a:73 + a] for a in range(8)]
            n1 = jnp.zeros_like(s1)
            for a in range(PEER_TOPK):
                n1 = jnp.where(r1 == a, n_a[a], n1)
            n1_ref[h, :, ls] = n1
            e1_ref[h, :, ls] = jnp.exp(s1 - v1[0]) * (0.5 / zsum)
            r2_ref[h, :, ls] = r2.astype(BF16)
            e2_ref[h, :, ls] = jnp.exp(s2 - v2[0]).astype(BF16)
        return carry

    lax.fori_loop(0, N_HEADS, head, 0)


def _peer_route(h2, wq_bf16, keys_bf16, tt):
    B, T, _ = h2.shape
    big = pl.BlockSpec((None, N_HEADS, PEER_N_KEYS, tt), lambda b, i: (b, 0, 0, i))
    f32_shape = jax.ShapeDtypeStruct((B, N_HEADS, PEER_N_KEYS, T), F32)
    bf16_shape = jax.ShapeDtypeStruct((B, N_HEADS, PEER_N_KEYS, T), BF16)
    return pl.pallas_call(
        functools.partial(_route_body, tt=tt),
        grid=(B, T // tt),
        in_specs=[pl.BlockSpec((1, tt, D_MODEL), lambda b, i: (b, i, 0)),
                  pl.BlockSpec((D_MODEL, 2 * N_HEADS * HEAD_W), lambda b, i: (0, 0)),
                  pl.BlockSpec((2 * N_HEADS, PEER_N_KEYS, HEAD_W), lambda b, i: (0, 0, 0))],
        out_specs=[big, big, big, big],
        out_shape=[f32_shape, f32_shape, bf16_shape, bf16_shape],
        scratch_shapes=[pltpu.VMEM((tt, 2 * N_HEADS * HEAD_W), BF16), pltpu.VMEM((2, PEER_N_KEYS, tt), F32)],
        compiler_params=_cparams(("parallel", "parallel")),
        name="peer_route",
    )(h2, wq_bf16, keys_bf16)


PEER_STEP_KEYS = 8


def _dense_body(h_ref, u_ref, vt_ref, n1_ref, e1_ref, r2_ref, e2_ref, x_ref, mod_ref, ln_ref, o_ref,
                act_scr, w_scr, acc_scr, r2_scr, e2_scr, *, tt):
    k = pl.program_id(2)

    @pl.when(k == 0)
    def _():
        acc_scr[...] = jnp.zeros_like(acc_scr)
        r2_scr[...] = r2_ref[...]
        e2_scr[...] = e2_ref[...]

    hb = h_ref[0]
    sub = 2 * PEER_N_KEYS
    for jj in range(PEER_STEP_KEYS // 2):
        act_scr[jj * sub:(jj + 1) * sub, :] = _dot_nt(u_ref[jj * sub:(jj + 1) * sub, :], hb)
    for jj in range(PEER_STEP_KEYS // 2):
        for j2 in range(2):
            j = 2 * jj + j2
            for lc in range(tt // HEAD_W):
                ls = slice(lc * HEAD_W, (lc + 1) * HEAD_W)
                g = None
                for h in range(N_HEADS):
                    n_t = jnp.broadcast_to(n1_ref[h, j:j + 1, ls], (16, HEAD_W)).astype(BF16)[None]
                    e_t = jnp.broadcast_to(e1_ref[h, j:j + 1, ls], (16, HEAD_W)).astype(BF16)[None]
                    hit = r2_scr[h, :, ls].reshape(PEER_N_KEYS // 16, 16, HEAD_W) < n_t
                    term = jnp.where(hit, e2_scr[h, :, ls].reshape(PEER_N_KEYS // 16, 16, HEAD_W), 0.0) * e_t
                    g = term if g is None else g + term
                a = act_scr[j * PEER_N_KEYS:(j + 1) * PEER_N_KEYS, ls]
                gelu2 = a * (1.0 + lax.erf(a * (2.0 ** -0.5)))
                gf = g.reshape(PEER_N_KEYS, HEAD_W).astype(F32)
                w_scr[j * PEER_N_KEYS:(j + 1) * PEER_N_KEYS, ls] = (gelu2 * gf).astype(BF16)
        acc_scr[...] += _dot(vt_ref[:, jj * sub:(jj + 1) * sub], w_scr[jj * sub:(jj + 1) * sub, :])

    @pl.when(k == pl.num_programs(2) - 1)
    def _():
        y = DEEPNORM_ALPHA * x_ref[0] + mod_ref[0, 5:6, :] * acc_scr[...].T
        o_ref[0] = _layer_norm(y, ln_ref[0:1, :], ln_ref[1:2, :])


def _peer_dense(h2, x1, mod, u_bf16, vt_bf16, route, ln_gb, tt):
    B, T, _ = h2.shape
    n1, e1, r2, e2 = route
    per_b = mod.shape[0] > 1
    ne = PEER_STEP_KEYS * PEER_N_KEYS
    rows = pl.BlockSpec((None, N_HEADS, PEER_STEP_KEYS, tt), lambda b, i, k: (b, 0, k, i))
    full = pl.BlockSpec((None, N_HEADS, PEER_N_KEYS, tt), lambda b, i, k: (b, 0, 0, i))
    return pl.pallas_call(
        functools.partial(_dense_body, tt=tt),
        grid=(B, T // tt, PEER_N_KEYS // PEER_STEP_KEYS),
        in_specs=[pl.BlockSpec((1, tt, D_MODEL), lambda b, i, k: (b, i, 0)),
                  pl.BlockSpec((ne, D_MODEL), lambda b, i, k: (k, 0)),
                  pl.BlockSpec((D_MODEL, ne), lambda b, i, k: (0, k)),
                  rows, rows, full, full,
                  pl.BlockSpec((1, tt, D_MODEL), lambda b, i, k: (b, i, 0)),
                  pl.BlockSpec((1, 6, D_MODEL), (lambda b, i, k: (b, 0, 0)) if per_b else (lambda b, i, k: (0, 0, 0))),
                  pl.BlockSpec((2, D_MODEL), lambda b, i, k: (0, 0))],
        out_specs=pl.BlockSpec((1, tt, D_MODEL), lambda b, i, k: (b, i, 0)),
        out_shape=jax.ShapeDtypeStruct((B, T, D_MODEL), F32),
        scratch_shapes=[pltpu.VMEM((ne, tt), F32), pltpu.VMEM((ne, tt), BF16), pltpu.VMEM((D_MODEL, tt), F32),
                        pltpu.VMEM((N_HEADS, PEER_N_KEYS, tt), BF16), pltpu.VMEM((N_HEADS, PEER_N_KEYS, tt), BF16)],
        compiler_params=_cparams(("parallel", "parallel", "arbitrary")),
        name="peer_dense",
    )(h2, u_bf16, vt_bf16, n1, e1, r2, e2, x1, mod, ln_gb)


def _trunk_layer(x, mod, layer, p, ctx, rope_tabs, consts, flat_rows):
    B, T, _ = x.shape
    if flat_rows is not None:
        z = _in_proj(x.reshape(-1, flat_rows, D_MODEL), mod, p["w_in"], None, tm=flat_rows).reshape(B, T, W_IN_COLS)
    else:
        z = _in_proj(x, mod, p["w_in"], rope_tabs, tm=1024)
    tq = 256
    if ctx is None:
        oatt = _attention(z, p["lam"], p["subln"], None, layer, p["att_scale"], tq=tq, tk=T)
        o_f, o_b, s_fin = _hgrn(z, p["lb"], None, layer, consts)
    else:
        oatt = _attention(z, p["lam"], p["subln"], (ctx[0], ctx[1]), layer, p["att_scale"], tq=tq, tk=512)
        o_f, o_b, s_fin = _hgrn(z, p["lb"], ctx[2], layer, consts)
    x1, h2 = _merge(x, z, oatt, o_f, o_b, mod, p["w_branch"], p["w_out"], p["conv_w"], p["conv_b"],
                    p["rec_norm"], p["ln0"], tm=256)
    if flat_rows is not None:
        h2r, x1r = h2.reshape(-1, flat_rows, D_MODEL), x1.reshape(-1, flat_rows, D_MODEL)
    else:
        h2r, x1r = h2, x1
    tt = 512
    route = _peer_route(h2r, p["peer_wq"], p["peer_keys"], tt)
    x2 = _peer_dense(h2r, x1r, mod, p["peer_u"], p["peer_vt"], route, p["ln1"], tt).reshape(B, T, D_MODEL)
    return x2, z, s_fin


def _layer_params(l, lb_all, w_in, attn_lambda, attn_subln, conv_w, conv_b, rec_norm, w_branch, w_out, ln_g, ln_b,
                  peer_wq, peer_keys, peer_u, peer_v):
    lam_init = 0.8 - 0.6 * math.exp(-0.3 * l)
    lp = attn_lambda[l].astype(F32)
    lam = jnp.exp(jnp.sum(lp[0] * lp[1])) - jnp.exp(jnp.sum(lp[2] * lp[3])) + lam_init
    return {
        "w_in": w_in[l].astype(BF16), "lam": lam.reshape(1), "att_scale": 1.0 - lam_init,
        "subln": attn_subln[l].reshape(1, HEAD_W), "lb": lb_all[l],
        "w_branch": w_branch[l].astype(BF16), "w_out": w_out[l].astype(BF16),
        "conv_w": conv_w[l], "conv_b": conv_b[l].reshape(1, D_MODEL),
        "rec_norm": jnp.tile(rec_norm[l], N_HEADS).reshape(1, D_MODEL),
        "ln0": jnp.stack([ln_g[l, 0], ln_b[l, 0]]), "ln1": jnp.stack([ln_g[l, 1], ln_b[l, 1]]),
        "peer_wq": peer_wq[l].astype(BF16),
        "peer_keys": peer_keys[l].astype(BF16).reshape(2 * N_HEADS, PEER_N_KEYS, HEAD_W),
        "peer_u": peer_u[l].astype(BF16), "peer_vt": peer_v[l].astype(BF16).T,
    }


def kernel(x_prompt, x_sample, c, cache_attn_k, cache_attn_v, state_hgrn, c_ctx, mod_w, mod_b, w_in, attn_lambda,
           attn_subln, conv_w, conv_b, rec_lb, rec_norm, w_branch, w_out, ln_g, ln_b, peer_wq, peer_keys, peer_u,
           peer_v):
    B, T, _ = x_prompt.shape
    Bs, Ts, _ = x_sample.shape
    P = cache_attn_k.shape[2]
    lb_all = jnp.cumsum(jax.nn.softmax(rec_lb.astype(F32), axis=0), axis=0)
    lb_all = lb_all - lb_all[:1]
    cmat = jnp.concatenate([c_ctx[None, :], c, jnp.zeros((8 - 1 - Bs, D_MODEL), F32)], axis=0)
    mods = _mod_vectors(cmat, mod_w, mod_b).reshape(DEPTH, 8, 6, D_MODEL)
    rope_tabs = _rope_tables(Ts)
    consts = _rec_constants()
    ck = cache_attn_k.reshape(Bs, DEPTH, P, N_HEADS * HEAD_W)
    cv = cache_attn_v.reshape(Bs, DEPTH, P, N_HEADS * HEAD_W)

    y_p, y_s = x_prompt, x_sample
    ks, vs, ss = [], [], []
    for l in range(DEPTH):
        p = _layer_params(l, lb_all, w_in, attn_lambda, attn_subln, conv_w, conv_b, rec_norm, w_branch, w_out,
                          ln_g, ln_b, peer_wq, peer_keys, peer_u, peer_v)
        y_p, z_p, s_p = _trunk_layer(y_p, mods[l, 0:1], l, p, None, None, consts, flat_rows=1024)
        ks.append(z_p[..., COL_K * D_MODEL:(COL_K + 1) * D_MODEL].reshape(B, T, N_HEADS, HEAD_W))
        vs.append(z_p[..., COL_V * D_MODEL:(COL_V + 1) * D_MODEL].reshape(B, T, N_HEADS, HEAD_W))
        ss.append(s_p)
        y_s, _, _ = _trunk_layer(y_s, mods[l, 1:1 + Bs], l, p, (ck, cv, state_hgrn), rope_tabs, consts, flat_rows=None)
    return (y_p, y_s, jnp.stack(ks, axis=1), jnp.stack(vs, axis=1), jnp.stack(ss, axis=1))
```

```python
import functools
import math

import numpy as np
import jax
import jax.numpy as jnp
from jax import lax
from jax.experimental import pallas as pl
from jax.experimental.pallas import tpu as pltpu

F32 = jnp.float32
BF16 = jnp.bfloat16

D_MODEL = 1024
DEPTH = 2
GRID_W = 64
N_HEADS = 8
HEAD_W = 128
ATT_QK_DIM = 64
ROPE_BASE = 10000.0
ROPE_AXIS_DIM = ATT_QK_DIM // 2
PEER_N_KEYS = 128
PEER_TOPK = 16
LN_EPS = 1e-5
DEEPNORM_ALPHA = (2 * DEPTH) ** 0.25
W_IN_COLS = 14 * D_MODEL
COL_Q, COL_K, COL_V, COL_CB, COL_CC, COL_CX, COL_RQ, COL_RFF, COL_RFB, COL_RI, COL_RG, COL_G = range(12)

VMEM_LIMIT = 56 * 1024 * 1024
REC_C = 128
REC_LEVELS = 7
REC_HEADS_PER_STEP = 4


def _cparams(sem):
    return pltpu.CompilerParams(dimension_semantics=sem, vmem_limit_bytes=VMEM_LIMIT)


def _dot(a, b):
    return jnp.dot(a, b, preferred_element_type=F32)


def _dot_nt(a, b):
    return lax.dot_general(a, b, (((1,), (1,)), ((), ())), preferred_element_type=F32)


def _sigmoid(x):
    e = jnp.exp(-jnp.abs(x))
    r = 1.0 / (1.0 + e)
    return jnp.where(x >= 0, r, e * r)


def _layer_norm(y, g, b):
    mu = jnp.mean(y, axis=-1, keepdims=True)
    yc = y - mu
    var = jnp.mean(yc * yc, axis=-1, keepdims=True)
    return yc * lax.rsqrt(var + LN_EPS) * g + b


def _mod_body(c_ref, w_ref, b_ref, o_ref):
    c = c_ref[...]
    s = (c * _sigmoid(c)).astype(BF16)
    o_ref[0] = _dot(s, w_ref[0].astype(BF16)) + b_ref[0]


def _mod_vectors(cmat, mod_w, mod_b):
    tn = 1536
    return pl.pallas_call(
        _mod_body,
        grid=(DEPTH, 6 * D_MODEL // tn),
        in_specs=[pl.BlockSpec((8, D_MODEL), lambda l, j: (0, 0)),
                  pl.BlockSpec((1, D_MODEL, tn), lambda l, j: (l, 0, j)),
                  pl.BlockSpec((1, 1, tn), lambda l, j: (l, 0, j))],
        out_specs=pl.BlockSpec((1, 8, tn), lambda l, j: (l, 0, j)),
        out_shape=jax.ShapeDtypeStruct((DEPTH, 8, 6 * D_MODEL), F32),
        compiler_params=_cparams(("parallel", "parallel")),
        name="mod_vectors",
    )(cmat, mod_w, mod_b.reshape(DEPTH, 1, 6 * D_MODEL))


def _in_proj_body(x_ref, mod_ref, w_ref, *refs, rope, tm):
    if rope:
        cos_ref, sin_ref, o_ref, h_ref = refs
    else:
        o_ref, h_ref = refs
    j = pl.program_id(2)

    @pl.when(j == 0)
    def _():
        h_ref[...] = (x_ref[0] * (1.0 + mod_ref[0, 1:2, :]) + mod_ref[0, 0:1, :]).astype(BF16)

    z = _dot(h_ref[...], w_ref[...])
    if not rope:
        o_ref[0] = z
        return

    @pl.when(j < 2)
    def _():
        cos = cos_ref[...]
        sin = sin_ref[...]
        lane = lax.broadcasted_iota(jnp.int32, (tm, HEAD_W), 1)
        first = (lane % ROPE_AXIS_DIM) < (ROPE_AXIS_DIM // 2)
        for g in range(N_HEADS):
            zg = z[:, g * HEAD_W:(g + 1) * HEAD_W]
            partner = jnp.where(first, pltpu.roll(zg, HEAD_W - ROPE_AXIS_DIM // 2, 1),
                                pltpu.roll(zg, ROPE_AXIS_DIM // 2, 1))
            o_ref[0, :, g * HEAD_W:(g + 1) * HEAD_W] = zg * cos + partner * sin

    @pl.when(j >= 2)
    def _():
        o_ref[0] = z


def _in_proj(x, mod, w_bf16, rope_tabs, tm):
    B, T, _ = x.shape
    per_b = mod.shape[0] > 1
    rope = rope_tabs is not None
    in_specs = [pl.BlockSpec((1, tm, D_MODEL), lambda b, i, j: (b, i, 0)),
                pl.BlockSpec((1, 6, D_MODEL), (lambda b, i, j: (b, 0, 0)) if per_b else (lambda b, i, j: (0, 0, 0))),
                pl.BlockSpec((D_MODEL, D_MODEL), lambda b, i, j: (0, j))]
    args = [x, mod, w_bf16]
    if rope:
        in_specs += [pl.BlockSpec((tm, HEAD_W), lambda b, i, j: (i, 0))] * 2
        args += list(rope_tabs)
    return pl.pallas_call(
        functools.partial(_in_proj_body, rope=rope, tm=tm),
        grid=(B, T // tm, W_IN_COLS // D_MODEL),
        in_specs=in_specs,
        out_specs=pl.BlockSpec((1, tm, D_MODEL), lambda b, i, j: (b, i, j)),
        out_shape=jax.ShapeDtypeStruct((B, T, W_IN_COLS), F32),
        scratch_shapes=[pltpu.VMEM((tm, D_MODEL), BF16)],
        compiler_params=_cparams(("parallel", "parallel", "arbitrary")),
        name="in_proj_rope" if rope else "in_proj",
    )(*args)


def _rope_tables(T):
    t = np.arange(T)
    pos = np.stack([t // GRID_W, t % GRID_W], axis=1).astype(np.float32)
    lane = np.arange(HEAD_W)
    axis = (lane % ATT_QK_DIM) // ROPE_AXIS_DIM
    r = lane % ROPE_AXIS_DIM
    half = ROPE_AXIS_DIM // 2
    freqs = ROPE_BASE ** (-jnp.arange(0, ROPE_AXIS_DIM, 2, dtype=F32) / ROPE_AXIS_DIM)
    ang = jnp.asarray(pos)[:, axis] * freqs[r % half][None, :]
    sign = jnp.asarray(np.where(r < half, -1.0, 1.0).astype(np.float32))[None, :]
    return jnp.cos(ang), jnp.sin(ang) * sign


def _attn_body(lam_ref, q_ref, k_ref, v_ref, *refs, tq, tk, n_self, n_ctx, out_scale):
    if n_ctx:
        kc_ref, vc_ref, g_ref, o_ref, kb_ref, vb_ref = refs
    else:
        g_ref, o_ref, kb_ref, vb_ref = refs
    t_self = n_self * tk

    @pl.when(pl.program_id(2) == 0)
    def _():
        kb_ref[0:t_self, :] = k_ref[0].astype(BF16)
        vb_ref[0:t_self, :] = v_ref[0].astype(BF16)
        if n_ctx:
            kb_ref[t_self:, :] = kc_ref[0, 0].astype(BF16)
            vb_ref[t_self:, :] = vc_ref[0, 0].astype(BF16)

    q = q_ref[0] * (ATT_QK_DIM ** -0.5 * math.log2(math.e))
    lane = lax.broadcasted_iota(jnp.int32, (tq, HEAD_W), 1)
    lo = lane < ATT_QK_DIM
    qs = jnp.concatenate([jnp.where(lo, q, 0.0), jnp.where(lo, 0.0, q)], axis=0).astype(BF16)

    m = jnp.full((2 * tq, 1), -jnp.inf, F32)
    l = jnp.zeros((2 * tq, 1), F32)
    acc = jnp.zeros((2 * tq, HEAD_W), F32)
    for c in range(n_self + n_ctx):
        s = _dot_nt(qs, kb_ref[c * tk:(c + 1) * tk, :])
        m_new = jnp.maximum(m, jnp.max(s, axis=-1, keepdims=True))
        a = jnp.exp2(m - m_new)
        p = jnp.exp2(s - m_new)
        l = a * l + jnp.sum(p, axis=-1, keepdims=True)
        acc = a * acc + _dot(p.astype(BF16), vb_ref[c * tk:(c + 1) * tk, :])
        m = m_new
    o = acc / l
    o = o[:tq] - lam_ref[0] * o[tq:]
    o = o * lax.rsqrt(jnp.mean(o * o, axis=-1, keepdims=True) + LN_EPS) * g_ref[...] * out_scale
    o_ref[0] = o.astype(BF16)


def _attention(z, lam, subln, ctx_kv, layer, out_scale, tq, tk):
    B, T, _ = z.shape
    n_self = T // tk
    in_specs = [pl.BlockSpec(memory_space=pltpu.SMEM),
                pl.BlockSpec((1, tq, HEAD_W), lambda b, h, i: (b, i, COL_Q * N_HEADS + h)),
                pl.BlockSpec((1, T, HEAD_W), lambda b, h, i: (b, 0, COL_K * N_HEADS + h)),
                pl.BlockSpec((1, T, HEAD_W), lambda b, h, i: (b, 0, COL_V * N_HEADS + h))]
    args = [lam, z, z, z]
    t_all = T
    n_ctx = 0
    if ctx_kv is not None:
        P = ctx_kv[0].shape[2]
        assert P == tk
        n_ctx = 1
        t_all = T + P
        in_specs += [pl.BlockSpec((1, 1, P, HEAD_W), lambda b, h, i: (b, layer, 0, h))] * 2
        args += list(ctx_kv)
    in_specs.append(pl.BlockSpec((1, HEAD_W), lambda b, h, i: (0, 0)))
    args.append(subln)
    return pl.pallas_call(
        functools.partial(_attn_body, tq=tq, tk=tk, n_self=n_self, n_ctx=n_ctx, out_scale=out_scale),
        grid=(B, N_HEADS, T // tq),
        in_specs=in_specs,
        out_specs=pl.BlockSpec((1, tq, HEAD_W), lambda b, h, i: (b, i, h)),
        out_shape=jax.ShapeDtypeStruct((B, T, D_MODEL), BF16),
        scratch_shapes=[pltpu.VMEM((t_all, HEAD_W), BF16), pltpu.VMEM((t_all, HEAD_W), BF16)],
        compiler_params=_cparams(("parallel", "parallel", "arbitrary")),
        name="attention_ctx" if n_ctx else "attention",
    )(*args)


def _rec_constants():
    C = REC_C
    t = np.arange(C)[:, None]
    j = np.arange(C)[None, :]
    blocks = [(j <= t), (j > t)]
    for l in range(REC_LEVELS):
        m = 1 << l
        seg0 = (t // m) * m
        odd = ((t // m) % 2) == 1
        blocks.append(np.where(odd, (j >= seg0) & (j <= t), (j > t) & (j <= seg0 + m - 1)))
    mf = np.concatenate(blocks, axis=0).astype(np.float32)
    x = t ^ j
    lv = np.where(x == 0, REC_LEVELS, np.floor(np.log2(np.maximum(x, 1))).astype(np.int64))
    lvf = np.where(j <= t, lv, REC_LEVELS + 1).astype(np.int32)
    mb = mf.reshape(-1, C, C)[:, ::-1, ::-1].reshape(-1, C)
    lvb = lvf[::-1, ::-1]
    mf, mb = np.concatenate([mf, mf], axis=1), np.concatenate([mb, mb], axis=1)
    return (jnp.asarray(mf, BF16), jnp.asarray(np.ascontiguousarray(mb), BF16),
            jnp.asarray(lvf), jnp.asarray(np.ascontiguousarray(lvb)))


def _rec_pair(rq, ri, fx, lb, m_ref, lv, sts, backward):
    C = REC_C
    e = jnp.exp(-jnp.abs(fx))
    r = 1.0 / (1.0 + e)
    pos = fx >= 0
    sig = jnp.where(pos, r, e * r)
    nsig = jnp.where(pos, e * r, r)
    logf = jnp.log(lb + (1.0 - lb) * sig)
    kk = (1.0 - lb) * nsig
    q = rq * _sigmoid(rq)
    hi = logf.astype(BF16)
    mid = (logf - hi.astype(F32)).astype(BF16)
    lf2 = jnp.concatenate([hi, mid], axis=0)

    def expo(blk):
        return _dot(m_ref[blk * C:(blk + 1) * C, :], lf2)

    row = lax.broadcasted_iota(jnp.int32, (C, 2 * HEAD_W), 0)
    if backward:
        row = (C - 1) - row
    qb = q.astype(BF16)
    kb = kk.astype(BF16)
    ws = []
    for l in range(REC_LEVELS):
        odd = ((row >> l) & 1) == 1
        ws.append((jnp.exp(expo(2 + l)) * jnp.where(odd, q, kk)).astype(BF16))
    b_incl = expo(0)
    qi = (q * jnp.exp(b_incl)).astype(BF16)
    ki = (kk * jnp.exp(expo(1))).astype(BF16)
    dec = jnp.exp(b_incl[0:1, :] if backward else b_incl[C - 1:C, :])
    outs, new_sts = [], []
    for g in range(2):
        ls = slice(g * HEAD_W, (g + 1) * HEAD_W)
        a = jnp.where(lv == REC_LEVELS, _dot_nt(qb[:, ls], kb[:, ls]), 0.0)
        for l in range(REC_LEVELS):
            a = jnp.where(lv == l, _dot_nt(ws[l][:, ls], ws[l][:, ls]), a)
        rig = ri[:, ls]
        outs.append(_dot(a.astype(BF16), rig.astype(BF16)) + _dot_nt(qi[:, ls], sts[g].astype(BF16)))
        new_sts.append(sts[g] * dec[:, ls] + _dot(rig.T.astype(BF16), ki[:, ls]))
    return jnp.concatenate(outs, axis=1), new_sts


def _hgrn_body(mf_ref, mb_ref, lvf_ref, lvb_ref, rqf_ref, rif_ref, ff_ref, rqb_ref, rib_ref, fb_ref, lb_ref,
               *refs, has_s0):
    if has_s0:
        s0_ref, of_ref, ob_ref, so_ref, sf_scr, sb_scr = refs
    else:
        of_ref, ob_ref, so_ref, sf_scr, sb_scr = refs
    c = pl.program_id(2)

    @pl.when(c == 0)
    def _():
        for g in range(REC_HEADS_PER_STEP):
            if has_s0:
                sf_scr[g] = s0_ref[0, 0, g].T
                sb_scr[g] = s0_ref[0, 1, g].T
            else:
                sf_scr[g] = jnp.zeros((HEAD_W, HEAD_W), F32)
                sb_scr[g] = jnp.zeros((HEAD_W, HEAD_W), F32)

    for g in range(0, REC_HEADS_PER_STEP, 2):
        ls = slice(g * HEAD_W, (g + 2) * HEAD_W)
        o, st = _rec_pair(rqf_ref[0, :, ls], rif_ref[0, :, ls], ff_ref[0, :, ls], lb_ref[0:1, ls], mf_ref,
                          lvf_ref[...], [sf_scr[g], sf_scr[g + 1]], False)
        of_ref[0, :, ls] = o
        sf_scr[g] = st[0]
        sf_scr[g + 1] = st[1]
        o, st = _rec_pair(rqb_ref[0, :, ls], rib_ref[0, :, ls], fb_ref[0, :, ls], lb_ref[1:2, ls], mb_ref,
                          lvb_ref[...], [sb_scr[g], sb_scr[g + 1]], True)
        ob_ref[0, :, ls] = o
        sb_scr[g] = st[0]
        sb_scr[g + 1] = st[1]

    @pl.when(c == pl.num_programs(2) - 1)
    def _():
        for g in range(REC_HEADS_PER_STEP):
            so_ref[0, 0, g] = sf_scr[g].T
            so_ref[0, 1, g] = sb_scr[g].T


def _hgrn(z, lb, s0, layer, consts):
    B, T, _ = z.shape
    n = T // REC_C
    C = REC_C
    G = REC_HEADS_PER_STEP
    ng = N_HEADS // G
    W = G * HEAD_W

    def fwd(col):
        return pl.BlockSpec((1, C, W), lambda b, h, c: (b, c, col * ng + h))

    def bwd(col):
        return pl.BlockSpec((1, C, W), lambda b, h, c: (b, n - 1 - c, col * ng + h))

    const2 = lambda b, h, c: (0, 0)
    in_specs = [pl.BlockSpec(((2 + REC_LEVELS) * C, 2 * C), const2), pl.BlockSpec(((2 + REC_LEVELS) * C, 2 * C), const2),
                pl.BlockSpec((C, C), const2), pl.BlockSpec((C, C), const2),
                fwd(COL_RQ), fwd(COL_RI), fwd(COL_RFF), bwd(COL_RQ), bwd(COL_RI), bwd(COL_RFB),
                pl.BlockSpec((2, W), lambda b, h, c: (0, h))]
    args = list(consts) + [z] * 6 + [lb]
    if s0 is not None:
        s0v = s0.reshape(B, DEPTH * 2, N_HEADS, HEAD_W, HEAD_W)
        in_specs.append(pl.BlockSpec((1, 2, G, HEAD_W, HEAD_W), lambda b, h, c: (b, layer, h, 0, 0)))
        args.append(s0v)
    return pl.pallas_call(
        functools.partial(_hgrn_body, has_s0=s0 is not None),
        grid=(B, ng, n),
        in_specs=in_specs,
        out_specs=[pl.BlockSpec((1, C, W), lambda b, h, c: (b, c, h)),
                   pl.BlockSpec((1, C, W), lambda b, h, c: (b, n - 1 - c, h)),
                   pl.BlockSpec((1, 2, G, HEAD_W, HEAD_W), lambda b, h, c: (b, 0, h, 0, 0))],
        out_shape=[jax.ShapeDtypeStruct((B, T, D_MODEL), F32), jax.ShapeDtypeStruct((B, T, D_MODEL), F32),
                   jax.ShapeDtypeStruct((B, 2, N_HEADS, HEAD_W, HEAD_W), F32)],
        scratch_shapes=[pltpu.VMEM((G, HEAD_W, HEAD_W), F32), pltpu.VMEM((G, HEAD_W, HEAD_W), F32)],
        compiler_params=_cparams(("parallel", "parallel", "arbitrary")),
        name="hgrn_ctx" if s0 is not None else "hgrn",
    )(*args)


def _merge_body(oatt_ref, zb_ref, zc_ref, zx_ref, zcp_ref, zxp_ref, zcn_ref, zxn_ref, of_ref, ob_ref, rg_ref,
                g0_ref, g1_ref, g2_ref, x_ref, mod_ref, wb_ref, wo_ref, cw_ref, cb_ref, rn_ref, ln_ref,
                x1_ref, h2_ref, *, tm):
    i = pl.program_id(1)
    u = zc_ref[0] * zx_ref[0]
    row = lax.broadcasted_iota(jnp.int32, (tm, D_MODEL), 0)
    prev_ok = (i > 0).astype(F32)
    next_ok = (i < pl.num_programs(1) - 1).astype(F32)
    u_prev_edge = zcp_ref[0, 7:8, :] * zxp_ref[0, 7:8, :] * prev_ok
    u_next_edge = zcn_ref[0, 0:1, :] * zxn_ref[0, 0:1, :] * next_ok
    up = jnp.where(row == 0, u_prev_edge, pltpu.roll(u, 1, 0))
    un = jnp.where(row == tm - 1, u_next_edge, pltpu.roll(u, tm - 1, 0))
    conv = up * cw_ref[0:1, :] + u * cw_ref[1:2, :] + un * cw_ref[2:3, :] + cb_ref[...]
    o_conv = (zb_ref[0] * conv).astype(BF16)
    s = of_ref[0] + ob_ref[0]
    rg = rg_ref[0]
    parts = []
    for h in range(N_HEADS):
        sh = s[:, h * HEAD_W:(h + 1) * HEAD_W]
        parts.append(sh * lax.rsqrt(jnp.mean(sh * sh, axis=-1, keepdims=True) + LN_EPS))
    o_rec = (jnp.concatenate(parts, axis=1) * rn_ref[...] * (rg * _sigmoid(rg))).astype(BF16)
    merged = (_sigmoid(g0_ref[0]) * _dot(oatt_ref[0], wb_ref[0])
              + _sigmoid(g1_ref[0]) * _dot(o_conv, wb_ref[1])
              + _sigmoid(g2_ref[0]) * _dot(o_rec, wb_ref[2]))
    mix = _dot(merged.astype(BF16), wo_ref[...])
    y = DEEPNORM_ALPHA * x_ref[0] + mod_ref[0, 2:3, :] * mix
    x1 = _layer_norm(y, ln_ref[0:1, :], ln_ref[1:2, :])
    x1_ref[0] = x1
    h2_ref[0] = (x1 * (1.0 + mod_ref[0, 4:5, :]) + mod_ref[0, 3:4, :]).astype(BF16)


def _merge(x, z, oatt, o_f, o_b, mod, wb_bf16, wo_bf16, conv_w, conv_b, rec_norm_t, ln_gb, tm):
    B, T, _ = x.shape
    per_b = mod.shape[0] > 1
    nb8 = tm // 8
    last8 = T // 8 - 1

    def col(c):
        return pl.BlockSpec((1, tm, D_MODEL), lambda b, i: (b, i, c))

    def prev8(c):
        return pl.BlockSpec((1, 8, D_MODEL), lambda b, i: (b, jnp.maximum(i * nb8 - 1, 0), c))

    def next8(c):
        return pl.BlockSpec((1, 8, D_MODEL), lambda b, i: (b, jnp.minimum((i + 1) * nb8, last8), c))

    tile = pl.BlockSpec((1, tm, D_MODEL), lambda b, i: (b, i, 0))
    full2 = lambda b, i: (0, 0)
    in_specs = [tile, col(COL_CB), col(COL_CC), col(COL_CX), prev8(COL_CC), prev8(COL_CX), next8(COL_CC), next8(COL_CX),
                tile, tile, col(COL_RG), col(COL_G), col(COL_G + 1), col(COL_G + 2), tile,
                pl.BlockSpec((1, 6, D_MODEL), (lambda b, i: (b, 0, 0)) if per_b else (lambda b, i: (0, 0, 0))),
                pl.BlockSpec((3, D_MODEL, D_MODEL), lambda b, i: (0, 0, 0)),
                pl.BlockSpec((D_MODEL, D_MODEL), full2),
                pl.BlockSpec((3, D_MODEL), full2), pl.BlockSpec((1, D_MODEL), full2),
                pl.BlockSpec((1, D_MODEL), full2), pl.BlockSpec((2, D_MODEL), full2)]
    return pl.pallas_call(
        functools.partial(_merge_body, tm=tm),
        grid=(B, T // tm),
        in_specs=in_specs,
        out_specs=[tile, tile],
        out_shape=[jax.ShapeDtypeStruct((B, T, D_MODEL), F32), jax.ShapeDtypeStruct((B, T, D_MODEL), BF16)],
        compiler_params=_cparams(("parallel", "arbitrary")),
        name="merge",
    )(oatt, z, z, z, z, z, z, z, o_f, o_b, z, z, z, z, x, mod, wb_bf16, wo_bf16, conv_w, conv_b, rec_norm_t, ln_gb)


def _top_rows(s, k):
    n, w = s.shape
    rid = lax.broadcasted_iota(jnp.int32, (n, w), 0).astype(F32)
    rank = jnp.full((n, w), float(k), F32)
    vals = []
    for r in range(k):
        m = jnp.max(s, axis=0, keepdims=True)
        first = jnp.min(jnp.where(s == m, rid, float(n)), axis=0, keepdims=True)
        hit = rid == first
        s = jnp.where(hit, -jnp.inf, s)
        rank = jnp.where(hit, float(r), rank)
        vals.append(m)
    return vals, rank


def _route_body(h_ref, wq_ref, keys_ref, n1_ref, e1_ref, r2_ref, e2_ref, q_scr, s_scr, *, tt):
    q_scr[...] = _dot(h_ref[0], wq_ref[...]).astype(BF16)

    def head(h, carry):
        for p in range(2):
            off = pl.multiple_of((2 * h + p) * HEAD_W, HEAD_W)
            s_scr[p] = _dot_nt(keys_ref[2 * h + p], q_scr[:, pl.ds(off, HEAD_W)])
        for lc in range(tt // HEAD_W):
            ls = slice(lc * HEAD_W, (lc + 1) * HEAD_W)
            s1 = s_scr[0, :, ls]
            s2 = s_scr[1, :, ls]
            v1, r1 = _top_rows(s1, PEER_TOPK)
            v2, r2 = _top_rows(s2, PEER_TOPK)
            v2a = jnp.concatenate(v2, axis=0)
            cands = [v1[0] + v2a] + [v1[a] + v2a[:8] for a in range(1, 8)] + [jnp.concatenate(v1[8:], axis=0) + v2[0]]
            cand = jnp.concatenate(cands, axis=0)
            _, rc = _top_rows(cand, PEER_TOPK)
            sel = rc < PEER_TOPK
            zsum = jnp.sum(jnp.where(sel, jnp.exp(cand - (v1[0] + v2[0])), 0.0), axis=0, keepdims=True)
            selc = jnp.where(sel, 1.0, 0.0)
            n_a = [jnp.sum(selc[0:16], axis=0, keepdims=True)]
            n_a += [jnp.sum(selc[8 + 8 * a:16 + 8 * a], axis=0, keepdims=True) for a in range(1, 8)]
            n_a += [selc[72 + a:73 + a] for a in range(8)]
            n1 = jnp.zeros_like(s1)
            for a in range(PEER_TOPK):
                n1 = jnp.where(r1 == a, n_a[a], n1)
            n1_ref[h, :, ls] = n1
            e1_ref[h, :, ls] = jnp.exp(s1 - v1[0]) * (0.5 / zsum)
            r2_ref[h, :, ls] = r2.astype(BF16)
            e2_ref[h, :, ls] = jnp.exp(s2 - v2[0]).astype(BF16)
        return carry

    lax.fori_loop(0, N_HEADS, head, 0)


def _peer_route(h2, wq_bf16, keys_bf16, tt):
    B, T, _ = h2.shape
    big = pl.BlockSpec((None, N_HEADS, PEER_N_KEYS, tt), lambda b, i: (b, 0, 0, i))
    f32_shape = jax.ShapeDtypeStruct((B, N_HEADS, PEER_N_KEYS, T), F32)
    bf16_shape = jax.ShapeDtypeStruct((B, N_HEADS, PEER_N_KEYS, T), BF16)
    return pl.pallas_call(
        functools.partial(_route_body, tt=tt),
        grid=(B, T // tt),
        in_specs=[pl.BlockSpec((1, tt, D_MODEL), lambda b, i: (b, i, 0)),
                  pl.BlockSpec((D_MODEL, 2 * N_HEADS * HEAD_W), lambda b, i: (0, 0)),
                  pl.BlockSpec((2 * N_HEADS, PEER_N_KEYS, HEAD_W), lambda b, i: (0, 0, 0))],
        out_specs=[big, big, big, big],
        out_shape=[f32_shape, f32_shape, bf16_shape, bf16_shape],
        scratch_shapes=[pltpu.VMEM((tt, 2 * N_HEADS * HEAD_W), BF16), pltpu.VMEM((2, PEER_N_KEYS, tt), F32)],
        compiler_params=_cparams(("parallel", "parallel")),
        name="peer_route",
    )(h2, wq_bf16, keys_bf16)


PEER_STEP_KEYS = 8


def _dense_body(h_ref, u_ref, vt_ref, n1_ref, e1_ref, r2_ref, e2_ref, x_ref, mod_ref, ln_ref, o_ref,
                act_scr, g_scr, w_scr, acc_scr, r2_scr, e2_scr, *, tt):
    k = pl.program_id(2)

    @pl.when(k == 0)
    def _():
        acc_scr[...] = jnp.zeros_like(acc_scr)
        r2_scr[...] = r2_ref[...]
        e2_scr[...] = e2_ref[...]

    hb = h_ref[0]
    sub = 2 * PEER_N_KEYS

    def routing_weights(jj):
        for j in (2 * jj, 2 * jj + 1):
            for lc in range(tt // HEAD_W):
                ls = slice(lc * HEAD_W, (lc + 1) * HEAD_W)
                g = None
                for h in range(N_HEADS):
                    n_t = jnp.broadcast_to(n1_ref[h, j:j + 1, ls], (16, HEAD_W)).astype(BF16)[None]
                    e_t = jnp.broadcast_to(e1_ref[h, j:j + 1, ls], (16, HEAD_W)).astype(BF16)[None]
                    hit = r2_scr[h, :, ls].reshape(PEER_N_KEYS // 16, 16, HEAD_W) < n_t
                    term = jnp.where(hit, e2_scr[h, :, ls].reshape(PEER_N_KEYS // 16, 16, HEAD_W), 0.0) * e_t
                    g = term if g is None else g + term
                g_scr[j * PEER_N_KEYS:(j + 1) * PEER_N_KEYS, ls] = g.reshape(PEER_N_KEYS, HEAD_W)
        return lax.shift_right_logical(pltpu.bitcast(g[0], jnp.uint32), jnp.uint32(32))

    def after(x, zero_tile):
        xi = pltpu.bitcast(x, jnp.uint32)
        z = jnp.tile(zero_tile, (xi.shape[0] // zero_tile.shape[0], xi.shape[1] // zero_tile.shape[1]))
        return pltpu.bitcast(xi | z, x.dtype)

    nsub = PEER_STEP_KEYS // 2
    for jj in range(nsub):
        rows = slice(jj * sub, (jj + 1) * sub)
        zero_tile = routing_weights(jj)
        act_scr[rows, :] = _dot_nt(after(u_ref[rows, :], zero_tile), hb)
    for jj in range(nsub):
        rows = slice(jj * sub, (jj + 1) * sub)
        a = act_scr[rows, :]
        gelu2 = a * (1.0 + lax.erf(a * (2.0 ** -0.5)))
        w_scr[rows, :] = (gelu2 * g_scr[rows, :].astype(F32)).astype(BF16)
        acc_scr[...] += _dot(vt_ref[0, :, rows], w_scr[rows, :])

    @pl.when(k == pl.num_programs(2) - 1)
    def _():
        y = DEEPNORM_ALPHA * x_ref[0] + mod_ref[0, 5:6, :] * acc_scr[...].T
        o_ref[0] = _layer_norm(y, ln_ref[0:1, :], ln_ref[1:2, :])


def _peer_dense(h2, x1, mod, u_bf16, vt_bf16, route, ln_gb, tt):
    B, T, _ = h2.shape
    n1, e1, r2, e2 = route
    per_b = mod.shape[0] > 1
    ne = PEER_STEP_KEYS * PEER_N_KEYS
    rows = pl.BlockSpec((None, N_HEADS, PEER_STEP_KEYS, tt), lambda b, i, k: (b, 0, k, i))
    full = pl.BlockSpec((None, N_HEADS, PEER_N_KEYS, tt), lambda b, i, k: (b, 0, 0, i))
    return pl.pallas_call(
        functools.partial(_dense_body, tt=tt),
        grid=(B, T // tt, PEER_N_KEYS // PEER_STEP_KEYS),
        in_specs=[pl.BlockSpec((1, tt, D_MODEL), lambda b, i, k: (b, i, 0)),
                  pl.BlockSpec((ne, D_MODEL), lambda b, i, k: (k, 0)),
                  pl.BlockSpec((1, D_MODEL, ne), lambda b, i, k: (k, 0, 0)),
                  rows, rows, full, full,
                  pl.BlockSpec((1, tt, D_MODEL), lambda b, i, k: (b, i, 0)),
                  pl.BlockSpec((1, 6, D_MODEL), (lambda b, i, k: (b, 0, 0)) if per_b else (lambda b, i, k: (0, 0, 0))),
                  pl.BlockSpec((2, D_MODEL), lambda b, i, k: (0, 0))],
        out_specs=pl.BlockSpec((1, tt, D_MODEL), lambda b, i, k: (b, i, 0)),
        out_shape=jax.ShapeDtypeStruct((B, T, D_MODEL), F32),
        scratch_shapes=[pltpu.VMEM((ne, tt), F32), pltpu.VMEM((ne, tt), BF16), pltpu.VMEM((ne, tt), BF16),
                        pltpu.VMEM((D_MODEL, tt), F32),
                        pltpu.VMEM((N_HEADS, PEER_N_KEYS, tt), BF16), pltpu.VMEM((N_HEADS, PEER_N_KEYS, tt), BF16)],
        compiler_params=_cparams(("parallel", "parallel", "arbitrary")),
        name="peer_dense",
    )(h2, u_bf16, vt_bf16, n1, e1, r2, e2, x1, mod, ln_gb)


def _trunk_layer(x, mod, layer, p, ctx, rope_tabs, consts, flat_rows):
    B, T, _ = x.shape
    if flat_rows is not None:
        z = _in_proj(x.reshape(-1, flat_rows, D_MODEL), mod, p["w_in"], None, tm=flat_rows).reshape(B, T, W_IN_COLS)
    else:
        z = _in_proj(x, mod, p["w_in"], rope_tabs, tm=1024)
    tq = 256
    if ctx is None:
        oatt = _attention(z, p["lam"], p["subln"], None, layer, p["att_scale"], tq=tq, tk=T)
        o_f, o_b, s_fin = _hgrn(z, p["lb"], None, layer, consts)
    else:
        oatt = _attention(z, p["lam"], p["subln"], (ctx[0], ctx[1]), layer, p["att_scale"], tq=tq, tk=512)
        o_f, o_b, s_fin = _hgrn(z, p["lb"], ctx[2], layer, consts)
    x1, h2 = _merge(x, z, oatt, o_f, o_b, mod, p["w_branch"], p["w_out"], p["conv_w"], p["conv_b"],
                    p["rec_norm"], p["ln0"], tm=256)
    if flat_rows is not None:
        h2r, x1r = h2.reshape(-1, flat_rows, D_MODEL), x1.reshape(-1, flat_rows, D_MODEL)
    else:
        h2r, x1r = h2, x1
    tt = 512
    route = _peer_route(h2r, p["peer_wq"], p["peer_keys"], tt)
    x2 = _peer_dense(h2r, x1r, mod, p["peer_u"], p["peer_vt"], route, p["ln1"], tt).reshape(B, T, D_MODEL)
    return x2, z, s_fin


def _layer_params(l, lb_all, w_in, attn_lambda, attn_subln, conv_w, conv_b, rec_norm, w_branch, w_out, ln_g, ln_b,
                  peer_wq, peer_keys, peer_u, peer_v):
    lam_init = 0.8 - 0.6 * math.exp(-0.3 * l)
    lp = attn_lambda[l].astype(F32)
    lam = jnp.exp(jnp.sum(lp[0] * lp[1])) - jnp.exp(jnp.sum(lp[2] * lp[3])) + lam_init
    return {
        "w_in": w_in[l].astype(BF16), "lam": lam.reshape(1), "att_scale": 1.0 - lam_init,
        "subln": attn_subln[l].reshape(1, HEAD_W), "lb": lb_all[l],
        "w_branch": w_branch[l].astype(BF16), "w_out": w_out[l].astype(BF16),
        "conv_w": conv_w[l], "conv_b": conv_b[l].reshape(1, D_MODEL),
        "rec_norm": jnp.tile(rec_norm[l], N_HEADS).reshape(1, D_MODEL),
        "ln0": jnp.stack([ln_g[l, 0], ln_b[l, 0]]), "ln1": jnp.stack([ln_g[l, 1], ln_b[l, 1]]),
        "peer_wq": peer_wq[l].astype(BF16),
        "peer_keys": peer_keys[l].astype(BF16).reshape(2 * N_HEADS, PEER_N_KEYS, HEAD_W),
        "peer_u": peer_u[l].astype(BF16),
        "peer_vt": peer_v[l].astype(BF16).reshape(-1, PEER_STEP_KEYS * PEER_N_KEYS, D_MODEL).transpose(0, 2, 1),
    }


def kernel(x_prompt, x_sample, c, cache_attn_k, cache_attn_v, state_hgrn, c_ctx, mod_w, mod_b, w_in, attn_lambda,
           attn_subln, conv_w, conv_b, rec_lb, rec_norm, w_branch, w_out, ln_g, ln_b, peer_wq, peer_keys, peer_u,
           peer_v):
    B, T, _ = x_prompt.shape
    Bs, Ts, _ = x_sample.shape
    P = cache_attn_k.shape[2]
    lb_all = jnp.cumsum(jax.nn.softmax(rec_lb.astype(F32), axis=0), axis=0)
    lb_all = lb_all - lb_all[:1]
    cmat = jnp.concatenate([c_ctx[None, :], c, jnp.zeros((8 - 1 - Bs, D_MODEL), F32)], axis=0)
    mods = _mod_vectors(cmat, mod_w, mod_b).reshape(DEPTH, 8, 6, D_MODEL)
    rope_tabs = _rope_tables(Ts)
    consts = _rec_constants()
    ck = cache_attn_k.reshape(Bs, DEPTH, P, N_HEADS * HEAD_W)
    cv = cache_attn_v.reshape(Bs, DEPTH, P, N_HEADS * HEAD_W)

    y_p, y_s = x_prompt, x_sample
    ks, vs, ss = [], [], []
    for l in range(DEPTH):
        p = _layer_params(l, lb_all, w_in, attn_lambda, attn_subln, conv_w, conv_b, rec_norm, w_branch, w_out,
                          ln_g, ln_b, peer_wq, peer_keys, peer_u, peer_v)
        y_p, z_p, s_p = _trunk_layer(y_p, mods[l, 0:1], l, p, None, None, consts, flat_rows=1024)
        ks.append(z_p[..., COL_K * D_MODEL:(COL_K + 1) * D_MODEL].reshape(B, T, N_HEADS, HEAD_W))
        vs.append(z_p[..., COL_V * D_MODEL:(COL_V + 1) * D_MODEL].reshape(B, T, N_HEADS, HEAD_W))
        ss.append(s_p)
        y_s, _, _ = _trunk_layer(y_s, mods[l, 1:1 + Bs], l, p, (ck, cv, state_hgrn), rope_tabs, consts, flat_rows=None)
    return (y_p, y_s, jnp.stack(ks, axis=1), jnp.stack(vs, axis=1), jnp.stack(ss, axis=1))
```

```python
import functools
import math

import numpy as np
import jax
import jax.numpy as jnp
from jax import lax
from jax.experimental import pallas as pl
from jax.experimental.pallas import tpu as pltpu

F32 = jnp.float32
BF16 = jnp.bfloat16

D_MODEL = 1024
DEPTH = 2
GRID_W = 64
N_HEADS = 8
HEAD_W = 128
ATT_QK_DIM = 64
ROPE_BASE = 10000.0
ROPE_AXIS_DIM = ATT_QK_DIM // 2
PEER_N_KEYS = 128
PEER_TOPK = 16
LN_EPS = 1e-5
DEEPNORM_ALPHA = (2 * DEPTH) ** 0.25
W_IN_COLS = 14 * D_MODEL
COL_Q, COL_K, COL_V, COL_CB, COL_CC, COL_CX, COL_RQ, COL_RFF, COL_RFB, COL_RI, COL_RG, COL_G = range(12)

VMEM_LIMIT = 56 * 1024 * 1024
PROJ_ROWS = 1024
ATT_Q_ROWS = 512
ATT_KEY_CHUNK = 512
MERGE_ROWS = 256
PEER_TOKENS = 512
REC_C = 128
REC_LEVELS = 7
REC_HEADS_PER_STEP = 8


def _cparams(sem):
    return pltpu.CompilerParams(dimension_semantics=sem, vmem_limit_bytes=VMEM_LIMIT)


def _dot(a, b):
    return jnp.dot(a, b, preferred_element_type=F32)


def _dot_nt(a, b):
    return lax.dot_general(a, b, (((1,), (1,)), ((), ())), preferred_element_type=F32)


def _sigmoid(x):
    e = jnp.exp(-jnp.abs(x))
    r = 1.0 / (1.0 + e)
    return jnp.where(x >= 0, r, e * r)


def _layer_norm(y, g, b):
    mu = jnp.mean(y, axis=-1, keepdims=True)
    yc = y - mu
    var = jnp.mean(yc * yc, axis=-1, keepdims=True)
    return yc * lax.rsqrt(var + LN_EPS) * g + b


def _mod_body(c_ref, w_ref, b_ref, o_ref):
    c = c_ref[...]
    s = (c * _sigmoid(c)).astype(BF16)
    o_ref[0] = _dot(s, w_ref[0].astype(BF16)) + b_ref[0]


def _mod_vectors(cmat, mod_w, mod_b):
    tn = 1536
    return pl.pallas_call(
        _mod_body,
        grid=(DEPTH, 6 * D_MODEL // tn),
        in_specs=[pl.BlockSpec((8, D_MODEL), lambda l, j: (0, 0)),
                  pl.BlockSpec((1, D_MODEL, tn), lambda l, j: (l, 0, j)),
                  pl.BlockSpec((1, 1, tn), lambda l, j: (l, 0, j))],
        out_specs=pl.BlockSpec((1, 8, tn), lambda l, j: (l, 0, j)),
        out_shape=jax.ShapeDtypeStruct((DEPTH, 8, 6 * D_MODEL), F32),
        compiler_params=_cparams(("parallel", "parallel")),
        name="mod_vectors",
    )(cmat, mod_w, mod_b.reshape(DEPTH, 1, 6 * D_MODEL))


def _in_proj_body(x_ref, mod_ref, w_ref, *refs, rope, tm):
    if rope:
        cos_ref, sin_ref, o_ref, h_ref = refs
    else:
        o_ref, h_ref = refs
    j = pl.program_id(2)

    @pl.when(j == 0)
    def _():
        h_ref[...] = (x_ref[0] * (1.0 + mod_ref[0, 1:2, :]) + mod_ref[0, 0:1, :]).astype(BF16)

    z = _dot(h_ref[...], w_ref[...])
    if not rope:
        o_ref[0] = z
        return

    @pl.when(j < 2)
    def _():
        cos = cos_ref[...]
        sin = sin_ref[...]
        lane = lax.broadcasted_iota(jnp.int32, (tm, HEAD_W), 1)
        first = (lane % ROPE_AXIS_DIM) < (ROPE_AXIS_DIM // 2)
        for g in range(N_HEADS):
            zg = z[:, g * HEAD_W:(g + 1) * HEAD_W]
            partner = jnp.where(first, pltpu.roll(zg, HEAD_W - ROPE_AXIS_DIM // 2, 1),
                                pltpu.roll(zg, ROPE_AXIS_DIM // 2, 1))
            o_ref[0, :, g * HEAD_W:(g + 1) * HEAD_W] = zg * cos + partner * sin

    @pl.when(j >= 2)
    def _():
        o_ref[0] = z


def _in_proj(x, mod, w_bf16, rope_tabs, tm):
    B, T, _ = x.shape
    per_b = mod.shape[0] > 1
    rope = rope_tabs is not None
    in_specs = [pl.BlockSpec((1, tm, D_MODEL), lambda b, i, j: (b, i, 0)),
                pl.BlockSpec((1, 6, D_MODEL), (lambda b, i, j: (b, 0, 0)) if per_b else (lambda b, i, j: (0, 0, 0))),
                pl.BlockSpec((D_MODEL, D_MODEL), lambda b, i, j: (0, j))]
    args = [x, mod, w_bf16]
    if rope:
        in_specs += [pl.BlockSpec((tm, HEAD_W), lambda b, i, j: (i, 0))] * 2
        args += list(rope_tabs)
    return pl.pallas_call(
        functools.partial(_in_proj_body, rope=rope, tm=tm),
        grid=(B, T // tm, W_IN_COLS // D_MODEL),
        in_specs=in_specs,
        out_specs=pl.BlockSpec((1, tm, D_MODEL), lambda b, i, j: (b, i, j)),
        out_shape=jax.ShapeDtypeStruct((B, T, W_IN_COLS), F32),
        scratch_shapes=[pltpu.VMEM((tm, D_MODEL), BF16)],
        compiler_params=_cparams(("parallel", "parallel", "arbitrary")),
        name="in_proj_rope" if rope else "in_proj",
    )(*args)


def _rope_tables(T):
    t = np.arange(T)
    pos = np.stack([t // GRID_W, t % GRID_W], axis=1).astype(np.float32)
    lane = np.arange(HEAD_W)
    axis = (lane % ATT_QK_DIM) // ROPE_AXIS_DIM
    r = lane % ROPE_AXIS_DIM
    half = ROPE_AXIS_DIM // 2
    freqs = ROPE_BASE ** (-jnp.arange(0, ROPE_AXIS_DIM, 2, dtype=F32) / ROPE_AXIS_DIM)
    ang = jnp.asarray(pos)[:, axis] * freqs[r % half][None, :]
    sign = jnp.asarray(np.where(r < half, -1.0, 1.0).astype(np.float32))[None, :]
    return jnp.cos(ang), jnp.sin(ang) * sign


def _attn_body(lam_ref, q_ref, k_ref, v_ref, *refs, tq, tk, n_self, n_ctx, out_scale):
    if n_ctx:
        kc_ref, vc_ref, g_ref, o_ref, kb_ref, vb_ref = refs
    else:
        g_ref, o_ref, kb_ref, vb_ref = refs
    t_self = n_self * tk

    @pl.when(pl.program_id(2) == 0)
    def _():
        kb_ref[0:t_self, :] = k_ref[0].astype(BF16)
        vb_ref[0:t_self, :] = v_ref[0].astype(BF16)
        if n_ctx:
            kb_ref[t_self:, :] = kc_ref[0, 0].astype(BF16)
            vb_ref[t_self:, :] = vc_ref[0, 0].astype(BF16)

    q = q_ref[0] * (ATT_QK_DIM ** -0.5 * math.log2(math.e))
    lane = lax.broadcasted_iota(jnp.int32, (tq, HEAD_W), 1)
    lo = lane < ATT_QK_DIM
    qs = jnp.concatenate([jnp.where(lo, q, 0.0), jnp.where(lo, 0.0, q)], axis=0).astype(BF16)

    m = jnp.full((2 * tq, 1), -jnp.inf, F32)
    l = jnp.zeros((2 * tq, 1), F32)
    acc = jnp.zeros((2 * tq, HEAD_W), F32)
    for c in range(n_self + n_ctx):
        s = _dot_nt(qs, kb_ref[c * tk:(c + 1) * tk, :])
        m_new = jnp.maximum(m, jnp.max(s, axis=-1, keepdims=True))
        a = jnp.exp2(m - m_new)
        p = jnp.exp2(s - m_new)
        l = a * l + jnp.sum(p, axis=-1, keepdims=True)
        acc = a * acc + _dot(p.astype(BF16), vb_ref[c * tk:(c + 1) * tk, :])
        m = m_new
    o = acc / l
    o = o[:tq] - lam_ref[0] * o[tq:]
    o = o * lax.rsqrt(jnp.mean(o * o, axis=-1, keepdims=True) + LN_EPS) * g_ref[...] * out_scale
    o_ref[0] = o.astype(BF16)


def _attention(z, lam, subln, ctx_kv, layer, out_scale, tq, tk):
    B, T, _ = z.shape
    n_self = T // tk
    in_specs = [pl.BlockSpec(memory_space=pltpu.SMEM),
                pl.BlockSpec((1, tq, HEAD_W), lambda b, h, i: (b, i, COL_Q * N_HEADS + h)),
                pl.BlockSpec((1, T, HEAD_W), lambda b, h, i: (b, 0, COL_K * N_HEADS + h)),
                pl.BlockSpec((1, T, HEAD_W), lambda b, h, i: (b, 0, COL_V * N_HEADS + h))]
    args = [lam, z, z, z]
    t_all = T
    n_ctx = 0
    if ctx_kv is not None:
        P = ctx_kv[0].shape[2]
        assert P == tk
        n_ctx = 1
        t_all = T + P
        in_specs += [pl.BlockSpec((1, 1, P, HEAD_W), lambda b, h, i: (b, layer, 0, h))] * 2
        args += list(ctx_kv)
    in_specs.append(pl.BlockSpec((1, HEAD_W), lambda b, h, i: (0, 0)))
    args.append(subln)
    return pl.pallas_call(
        functools.partial(_attn_body, tq=tq, tk=tk, n_self=n_self, n_ctx=n_ctx, out_scale=out_scale),
        grid=(B, N_HEADS, T // tq),
        in_specs=in_specs,
        out_specs=pl.BlockSpec((1, tq, HEAD_W), lambda b, h, i: (b, i, h)),
        out_shape=jax.ShapeDtypeStruct((B, T, D_MODEL), BF16),
        scratch_shapes=[pltpu.VMEM((t_all, HEAD_W), BF16), pltpu.VMEM((t_all, HEAD_W), BF16)],
        compiler_params=_cparams(("parallel", "parallel", "arbitrary")),
        name="attention_ctx" if n_ctx else "attention",
    )(*args)


def _rec_constants():
    C = REC_C
    t = np.arange(C)[:, None]
    j = np.arange(C)[None, :]
    blocks = [(j <= t), (j > t)]
    for l in range(REC_LEVELS):
        m = 1 << l
        seg0 = (t // m) * m
        odd = ((t // m) % 2) == 1
        blocks.append(np.where(odd, (j >= seg0) & (j <= t), (j > t) & (j <= seg0 + m - 1)))
    mf = np.concatenate(blocks, axis=0).astype(np.float32)
    x = t ^ j
    lv = np.where(x == 0, REC_LEVELS, np.floor(np.log2(np.maximum(x, 1))).astype(np.int64))
    lvf = np.where(j <= t, lv, REC_LEVELS + 1).astype(np.int32)
    mb = mf.reshape(-1, C, C)[:, ::-1, ::-1].reshape(-1, C)
    lvb = lvf[::-1, ::-1]
    mf, mb = np.concatenate([mf, mf], axis=1), np.concatenate([mb, mb], axis=1)
    return (jnp.asarray(mf, BF16), jnp.asarray(np.ascontiguousarray(mb), BF16),
            jnp.asarray(lvf), jnp.asarray(np.ascontiguousarray(lvb)))


def _rec_pair(rq, ri, fx, lb, m_ref, lv, sts, backward):
    C = REC_C
    e = jnp.exp(-jnp.abs(fx))
    r = 1.0 / (1.0 + e)
    pos = fx >= 0
    sig = jnp.where(pos, r, e * r)
    nsig = jnp.where(pos, e * r, r)
    logf = jnp.log(lb + (1.0 - lb) * sig)
    kk = (1.0 - lb) * nsig
    q = rq * _sigmoid(rq)
    hi = logf.astype(BF16)
    mid = (logf - hi.astype(F32)).astype(BF16)
    lf2 = jnp.concatenate([hi, mid], axis=0)

    def expo(blk):
        return _dot(m_ref[blk * C:(blk + 1) * C, :], lf2)

    row = lax.broadcasted_iota(jnp.int32, (C, 2 * HEAD_W), 0)
    if backward:
        row = (C - 1) - row
    qb = q.astype(BF16)
    kb = kk.astype(BF16)
    ws = []
    for l in range(REC_LEVELS):
        odd = ((row >> l) & 1) == 1
        ws.append((jnp.exp(expo(2 + l)) * jnp.where(odd, q, kk)).astype(BF16))
    b_incl = expo(0)
    qi = (q * jnp.exp(b_incl)).astype(BF16)
    ki = (kk * jnp.exp(expo(1))).astype(BF16)
    dec = jnp.exp(b_incl[0:1, :] if backward else b_incl[C - 1:C, :])
    outs, new_sts = [], []
    for g in range(2):
        ls = slice(g * HEAD_W, (g + 1) * HEAD_W)
        a = jnp.where(lv == REC_LEVELS, _dot_nt(qb[:, ls], kb[:, ls]), 0.0)
        for l in range(REC_LEVELS):
            a = jnp.where(lv == l, _dot_nt(ws[l][:, ls], ws[l][:, ls]), a)
        rig = ri[:, ls]
        outs.append(_dot(a.astype(BF16), rig.astype(BF16)) + _dot_nt(qi[:, ls], sts[g].astype(BF16)))
        new_sts.append(sts[g] * dec[:, ls] + _dot(rig.T.astype(BF16), ki[:, ls]))
    return jnp.concatenate(outs, axis=1), new_sts


def _hgrn_body(mf_ref, mb_ref, lvf_ref, lvb_ref, rqf_ref, rif_ref, ff_ref, rqb_ref, rib_ref, fb_ref, lb_ref,
               *refs, has_s0):
    if has_s0:
        s0_ref, of_ref, ob_ref, so_ref, sf_scr, sb_scr = refs
    else:
        of_ref, ob_ref, so_ref, sf_scr, sb_scr = refs
    c = pl.program_id(2)

    @pl.when(c == 0)
    def _():
        for g in range(REC_HEADS_PER_STEP):
            if has_s0:
                sf_scr[g] = s0_ref[0, 0, g].T
                sb_scr[g] = s0_ref[0, 1, g].T
            else:
                sf_scr[g] = jnp.zeros((HEAD_W, HEAD_W), F32)
                sb_scr[g] = jnp.zeros((HEAD_W, HEAD_W), F32)

    for g in range(0, REC_HEADS_PER_STEP, 2):
        ls = slice(g * HEAD_W, (g + 2) * HEAD_W)
        o, st = _rec_pair(rqf_ref[0, :, ls], rif_ref[0, :, ls], ff_ref[0, :, ls], lb_ref[0:1, ls], mf_ref,
                          lvf_ref[...], [sf_scr[g], sf_scr[g + 1]], False)
        of_ref[0, :, ls] = o
        sf_scr[g] = st[0]
        sf_scr[g + 1] = st[1]
        o, st = _rec_pair(rqb_ref[0, :, ls], rib_ref[0, :, ls], fb_ref[0, :, ls], lb_ref[1:2, ls], mb_ref,
                          lvb_ref[...], [sb_scr[g], sb_scr[g + 1]], True)
        ob_ref[0, :, ls] = o
        sb_scr[g] = st[0]
        sb_scr[g + 1] = st[1]

    @pl.when(c == pl.num_programs(2) - 1)
    def _():
        for g in range(REC_HEADS_PER_STEP):
            so_ref[0, 0, g] = sf_scr[g].T
            so_ref[0, 1, g] = sb_scr[g].T


def _hgrn(z, lb, s0, layer, consts):
    B, T, _ = z.shape
    n = T // REC_C
    C = REC_C
    G = REC_HEADS_PER_STEP
    ng = N_HEADS // G
    W = G * HEAD_W

    def fwd(col):
        return pl.BlockSpec((1, C, W), lambda b, h, c: (b, c, col * ng + h))

    def bwd(col):
        return pl.BlockSpec((1, C, W), lambda b, h, c: (b, n - 1 - c, col * ng + h))

    const2 = lambda b, h, c: (0, 0)
    in_specs = [pl.BlockSpec(((2 + REC_LEVELS) * C, 2 * C), const2), pl.BlockSpec(((2 + REC_LEVELS) * C, 2 * C), const2),
                pl.BlockSpec((C, C), const2), pl.BlockSpec((C, C), const2),
                fwd(COL_RQ), fwd(COL_RI), fwd(COL_RFF), bwd(COL_RQ), bwd(COL_RI), bwd(COL_RFB),
                pl.BlockSpec((2, W), lambda b, h, c: (0, h))]
    args = list(consts) + [z] * 6 + [lb]
    if s0 is not None:
        s0v = s0.reshape(B, DEPTH * 2, N_HEADS, HEAD_W, HEAD_W)
        in_specs.append(pl.BlockSpec((1, 2, G, HEAD_W, HEAD_W), lambda b, h, c: (b, layer, h, 0, 0)))
        args.append(s0v)
    return pl.pallas_call(
        functools.partial(_hgrn_body, has_s0=s0 is not None),
        grid=(B, ng, n),
        in_specs=in_specs,
        out_specs=[pl.BlockSpec((1, C, W), lambda b, h, c: (b, c, h)),
                   pl.BlockSpec((1, C, W), lambda b, h, c: (b, n - 1 - c, h)),
                   pl.BlockSpec((1, 2, G, HEAD_W, HEAD_W), lambda b, h, c: (b, 0, h, 0, 0))],
        out_shape=[jax.ShapeDtypeStruct((B, T, D_MODEL), F32), jax.ShapeDtypeStruct((B, T, D_MODEL), F32),
                   jax.ShapeDtypeStruct((B, 2, N_HEADS, HEAD_W, HEAD_W), F32)],
        scratch_shapes=[pltpu.VMEM((G, HEAD_W, HEAD_W), F32), pltpu.VMEM((G, HEAD_W, HEAD_W), F32)],
        compiler_params=_cparams(("parallel", "parallel", "arbitrary")),
        name="hgrn_ctx" if s0 is not None else "hgrn",
    )(*args)


def _merge_body(oatt_ref, zb_ref, zc_ref, zx_ref, zcp_ref, zxp_ref, zcn_ref, zxn_ref, of_ref, ob_ref, rg_ref,
                g0_ref, g1_ref, g2_ref, x_ref, mod_ref, wb_ref, wo_ref, cw_ref, cb_ref, rn_ref, ln_ref,
                x1_ref, h2_ref, *, tm):
    i = pl.program_id(1)
    u = zc_ref[0] * zx_ref[0]
    row = lax.broadcasted_iota(jnp.int32, (tm, D_MODEL), 0)
    prev_ok = (i > 0).astype(F32)
    next_ok = (i < pl.num_programs(1) - 1).astype(F32)
    u_prev_edge = zcp_ref[0, 7:8, :] * zxp_ref[0, 7:8, :] * prev_ok
    u_next_edge = zcn_ref[0, 0:1, :] * zxn_ref[0, 0:1, :] * next_ok
    up = jnp.where(row == 0, u_prev_edge, pltpu.roll(u, 1, 0))
    un = jnp.where(row == tm - 1, u_next_edge, pltpu.roll(u, tm - 1, 0))
    conv = up * cw_ref[0:1, :] + u * cw_ref[1:2, :] + un * cw_ref[2:3, :] + cb_ref[...]
    o_conv = (zb_ref[0] * conv).astype(BF16)
    s = of_ref[0] + ob_ref[0]
    rg = rg_ref[0]
    parts = []
    for h in range(N_HEADS):
        sh = s[:, h * HEAD_W:(h + 1) * HEAD_W]
        parts.append(sh * lax.rsqrt(jnp.mean(sh * sh, axis=-1, keepdims=True) + LN_EPS))
    o_rec = (jnp.concatenate(parts, axis=1) * rn_ref[...] * (rg * _sigmoid(rg))).astype(BF16)
    merged = (_sigmoid(g0_ref[0]) * _dot(oatt_ref[0], wb_ref[0])
              + _sigmoid(g1_ref[0]) * _dot(o_conv, wb_ref[1])
              + _sigmoid(g2_ref[0]) * _dot(o_rec, wb_ref[2]))
    mix = _dot(merged.astype(BF16), wo_ref[...])
    y = DEEPNORM_ALPHA * x_ref[0] + mod_ref[0, 2:3, :] * mix
    x1 = _layer_norm(y, ln_ref[0:1, :], ln_ref[1:2, :])
    x1_ref[0] = x1
    h2_ref[0] = (x1 * (1.0 + mod_ref[0, 4:5, :]) + mod_ref[0, 3:4, :]).astype(BF16)


def _merge(x, z, oatt, o_f, o_b, mod, wb_bf16, wo_bf16, conv_w, conv_b, rec_norm_t, ln_gb, tm):
    B, T, _ = x.shape
    per_b = mod.shape[0] > 1
    nb8 = tm // 8
    last8 = T // 8 - 1

    def col(c):
        return pl.BlockSpec((1, tm, D_MODEL), lambda b, i: (b, i, c))

    def prev8(c):
        return pl.BlockSpec((1, 8, D_MODEL), lambda b, i: (b, jnp.maximum(i * nb8 - 1, 0), c))

    def next8(c):
        return pl.BlockSpec((1, 8, D_MODEL), lambda b, i: (b, jnp.minimum((i + 1) * nb8, last8), c))

    tile = pl.BlockSpec((1, tm, D_MODEL), lambda b, i: (b, i, 0))
    full2 = lambda b, i: (0, 0)
    in_specs = [tile, col(COL_CB), col(COL_CC), col(COL_CX), prev8(COL_CC), prev8(COL_CX), next8(COL_CC), next8(COL_CX),
                tile, tile, col(COL_RG), col(COL_G), col(COL_G + 1), col(COL_G + 2), tile,
                pl.BlockSpec((1, 6, D_MODEL), (lambda b, i: (b, 0, 0)) if per_b else (lambda b, i: (0, 0, 0))),
                pl.BlockSpec((3, D_MODEL, D_MODEL), lambda b, i: (0, 0, 0)),
                pl.BlockSpec((D_MODEL, D_MODEL), full2),
                pl.BlockSpec((3, D_MODEL), full2), pl.BlockSpec((1, D_MODEL), full2),
                pl.BlockSpec((1, D_MODEL), full2), pl.BlockSpec((2, D_MODEL), full2)]
    return pl.pallas_call(
        functools.partial(_merge_body, tm=tm),
        grid=(B, T // tm),
        in_specs=in_specs,
        out_specs=[tile, tile],
        out_shape=[jax.ShapeDtypeStruct((B, T, D_MODEL), F32), jax.ShapeDtypeStruct((B, T, D_MODEL), BF16)],
        compiler_params=_cparams(("parallel", "arbitrary")),
        name="merge",
    )(oatt, z, z, z, z, z, z, z, o_f, o_b, z, z, z, z, x, mod, wb_bf16, wo_bf16, conv_w, conv_b, rec_norm_t, ln_gb)


def _top_rows(s, k):
    n, w = s.shape
    rid = lax.broadcasted_iota(jnp.int32, (n, w), 0).astype(F32)
    rank = jnp.full((n, w), float(k), F32)
    vals = []
    for r in range(k):
        m = jnp.max(s, axis=0, keepdims=True)
        first = jnp.min(jnp.where(s == m, rid, float(n)), axis=0, keepdims=True)
        hit = rid == first
        s = jnp.where(hit, -jnp.inf, s)
        rank = jnp.where(hit, float(r), rank)
        vals.append(m)
    return vals, rank


def _route_body(h_ref, wq_ref, keys_ref, n1_ref, e1_ref, r2_ref, e2_ref, q_scr, s_scr, *, tt):
    q_scr[...] = _dot(h_ref[0], wq_ref[...]).astype(BF16)

    def head(h, carry):
        for p in range(2):
            off = pl.multiple_of((2 * h + p) * HEAD_W, HEAD_W)
            s_scr[p] = _dot_nt(keys_ref[2 * h + p], q_scr[:, pl.ds(off, HEAD_W)])
        for lc in range(tt // HEAD_W):
            ls = slice(lc * HEAD_W, (lc + 1) * HEAD_W)
            s1 = s_scr[0, :, ls]
            s2 = s_scr[1, :, ls]
            v1, r1 = _top_rows(s1, PEER_TOPK)
            v2, r2 = _top_rows(s2, PEER_TOPK)
            v2a = jnp.concatenate(v2, axis=0)
            cands = [v1[0] + v2a] + [v1[a] + v2a[:8] for a in range(1, 8)] + [jnp.concatenate(v1[8:], axis=0) + v2[0]]
            cand = jnp.concatenate(cands, axis=0)
            _, rc = _top_rows(cand, PEER_TOPK)
            sel = rc < PEER_TOPK
            zsum = jnp.sum(jnp.where(sel, jnp.exp(cand - (v1[0] + v2[0])), 0.0), axis=0, keepdims=True)
            selc = jnp.where(sel, 1.0, 0.0)
            n_a = [jnp.sum(selc[0:16], axis=0, keepdims=True)]
            n_a += [jnp.sum(selc[8 + 8 * a:16 + 8 * a], axis=0, keepdims=True) for a in range(1, 8)]
            n_a += [selc[72 + a:73 + a] for a in range(8)]
            n1 = jnp.zeros_like(s1)
            for a in range(PEER_TOPK):
                n1 = jnp.where(r1 == a, n_a[a], n1)
            n1_ref[h, :, ls] = n1
            e1_ref[h, :, ls] = jnp.exp(s1 - v1[0]) * (0.5 / zsum)
            r2_ref[h, :, ls] = r2.astype(BF16)
            e2_ref[h, :, ls] = jnp.exp(s2 - v2[0]).astype(BF16)
        return carry

    lax.fori_loop(0, N_HEADS, head, 0)


def _peer_route(h2, wq_bf16, keys_bf16, tt):
    B, T, _ = h2.shape
    big = pl.BlockSpec((None, N_HEADS, PEER_N_KEYS, tt), lambda b, i: (b, 0, 0, i))
    f32_shape = jax.ShapeDtypeStruct((B, N_HEADS, PEER_N_KEYS, T), F32)
    bf16_shape = jax.ShapeDtypeStruct((B, N_HEADS, PEER_N_KEYS, T), BF16)
    return pl.pallas_call(
        functools.partial(_route_body, tt=tt),
        grid=(B, T // tt),
        in_specs=[pl.BlockSpec((1, tt, D_MODEL), lambda b, i: (b, i, 0)),
                  pl.BlockSpec((D_MODEL, 2 * N_HEADS * HEAD_W), lambda b, i: (0, 0)),
                  pl.BlockSpec((2 * N_HEADS, PEER_N_KEYS, HEAD_W), lambda b, i: (0, 0, 0))],
        out_specs=[big, big, big, big],
        out_shape=[f32_shape, f32_shape, bf16_shape, bf16_shape],
        scratch_shapes=[pltpu.VMEM((tt, 2 * N_HEADS * HEAD_W), BF16), pltpu.VMEM((2, PEER_N_KEYS, tt), F32)],
        compiler_params=_cparams(("parallel", "parallel")),
        name="peer_route",
    )(h2, wq_bf16, keys_bf16)


PEER_STEP_KEYS = 8


def _dense_body(h_ref, u_ref, vt_ref, n1_ref, e1_ref, r2_ref, e2_ref, x_ref, mod_ref, ln_ref, o_ref,
                act_scr, g_scr, w_scr, acc_scr, r2_scr, e2_scr, *, tt):
    k = pl.program_id(2)

    @pl.when(k == 0)
    def _():
        acc_scr[...] = jnp.zeros_like(acc_scr)
        r2_scr[...] = r2_ref[...]
        e2_scr[...] = e2_ref[...]

    hb = h_ref[0]
    sub = 2 * PEER_N_KEYS

    def routing_weights(jj):
        for j in (2 * jj, 2 * jj + 1):
            for lc in range(tt // HEAD_W):
                ls = slice(lc * HEAD_W, (lc + 1) * HEAD_W)
                g = None
                for h in range(N_HEADS):
                    n_t = jnp.broadcast_to(n1_ref[h, j:j + 1, ls], (16, HEAD_W)).astype(BF16)[None]
                    e_t = jnp.broadcast_to(e1_ref[h, j:j + 1, ls], (16, HEAD_W)).astype(BF16)[None]
                    hit = r2_scr[h, :, ls].reshape(PEER_N_KEYS // 16, 16, HEAD_W) < n_t
                    term = jnp.where(hit, e2_scr[h, :, ls].reshape(PEER_N_KEYS // 16, 16, HEAD_W), 0.0) * e_t
                    g = term if g is None else g + term
                g_scr[j * PEER_N_KEYS:(j + 1) * PEER_N_KEYS, ls] = g.reshape(PEER_N_KEYS, HEAD_W)
        return lax.shift_right_logical(pltpu.bitcast(g[0], jnp.uint32), jnp.uint32(32))

    def after(x, zero_tile):
        xi = pltpu.bitcast(x, jnp.uint32)
        z = jnp.tile(zero_tile, (xi.shape[0] // zero_tile.shape[0], xi.shape[1] // zero_tile.shape[1]))
        return pltpu.bitcast(xi | z, x.dtype)

    nsub = PEER_STEP_KEYS // 2
    for jj in range(nsub):
        rows = slice(jj * sub, (jj + 1) * sub)
        zero_tile = routing_weights(jj)
        act_scr[rows, :] = _dot_nt(after(u_ref[rows, :], zero_tile), hb)
    for jj in range(nsub):
        rows = slice(jj * sub, (jj + 1) * sub)
        a = act_scr[rows, :]
        gelu2 = a * (1.0 + lax.erf(a * (2.0 ** -0.5)))
        w_scr[rows, :] = (gelu2 * g_scr[rows, :].astype(F32)).astype(BF16)
        acc_scr[...] += _dot(vt_ref[0, :, rows], w_scr[rows, :])

    @pl.when(k == pl.num_programs(2) - 1)
    def _():
        y = DEEPNORM_ALPHA * x_ref[0] + mod_ref[0, 5:6, :] * acc_scr[...].T
        o_ref[0] = _layer_norm(y, ln_ref[0:1, :], ln_ref[1:2, :])


def _peer_dense(h2, x1, mod, u_bf16, vt_bf16, route, ln_gb, tt):
    B, T, _ = h2.shape
    n1, e1, r2, e2 = route
    per_b = mod.shape[0] > 1
    ne = PEER_STEP_KEYS * PEER_N_KEYS
    rows = pl.BlockSpec((None, N_HEADS, PEER_STEP_KEYS, tt), lambda b, i, k: (b, 0, k, i))
    full = pl.BlockSpec((None, N_HEADS, PEER_N_KEYS, tt), lambda b, i, k: (b, 0, 0, i))
    return pl.pallas_call(
        functools.partial(_dense_body, tt=tt),
        grid=(B, T // tt, PEER_N_KEYS // PEER_STEP_KEYS),
        in_specs=[pl.BlockSpec((1, tt, D_MODEL), lambda b, i, k: (b, i, 0)),
                  pl.BlockSpec((ne, D_MODEL), lambda b, i, k: (k, 0)),
                  pl.BlockSpec((1, D_MODEL, ne), lambda b, i, k: (k, 0, 0)),
                  rows, rows, full, full,
                  pl.BlockSpec((1, tt, D_MODEL), lambda b, i, k: (b, i, 0)),
                  pl.BlockSpec((1, 6, D_MODEL), (lambda b, i, k: (b, 0, 0)) if per_b else (lambda b, i, k: (0, 0, 0))),
                  pl.BlockSpec((2, D_MODEL), lambda b, i, k: (0, 0))],
        out_specs=pl.BlockSpec((1, tt, D_MODEL), lambda b, i, k: (b, i, 0)),
        out_shape=jax.ShapeDtypeStruct((B, T, D_MODEL), F32),
        scratch_shapes=[pltpu.VMEM((ne, tt), F32), pltpu.VMEM((ne, tt), BF16), pltpu.VMEM((ne, tt), BF16),
                        pltpu.VMEM((D_MODEL, tt), F32),
                        pltpu.VMEM((N_HEADS, PEER_N_KEYS, tt), BF16), pltpu.VMEM((N_HEADS, PEER_N_KEYS, tt), BF16)],
        compiler_params=_cparams(("parallel", "parallel", "arbitrary")),
        name="peer_dense",
    )(h2, u_bf16, vt_bf16, n1, e1, r2, e2, x1, mod, ln_gb)


def _trunk_layer(x, mod, layer, p, ctx, rope_tabs, consts, flat_rows):
    B, T, _ = x.shape
    if flat_rows is not None:
        z = _in_proj(x.reshape(-1, flat_rows, D_MODEL), mod, p["w_in"], None, tm=flat_rows)
        z = z.reshape(B, T, W_IN_COLS)
    else:
        z = _in_proj(x, mod, p["w_in"], rope_tabs, tm=PROJ_ROWS)
    if ctx is None:
        oatt = _attention(z, p["lam"], p["subln"], None, layer, p["att_scale"], tq=min(ATT_Q_ROWS, T), tk=T)
        o_f, o_b, s_fin = _hgrn(z, p["lb"], None, layer, consts)
    else:
        oatt = _attention(z, p["lam"], p["subln"], (ctx[0], ctx[1]), layer, p["att_scale"], tq=ATT_Q_ROWS,
                          tk=ATT_KEY_CHUNK)
        o_f, o_b, s_fin = _hgrn(z, p["lb"], ctx[2], layer, consts)
    x1, h2 = _merge(x, z, oatt, o_f, o_b, mod, p["w_branch"], p["w_out"], p["conv_w"], p["conv_b"],
                    p["rec_norm"], p["ln0"], tm=MERGE_ROWS)
    if flat_rows is not None:
        h2r, x1r = h2.reshape(-1, flat_rows, D_MODEL), x1.reshape(-1, flat_rows, D_MODEL)
    else:
        h2r, x1r = h2, x1
    route = _peer_route(h2r, p["peer_wq"], p["peer_keys"], PEER_TOKENS)
    x2 = _peer_dense(h2r, x1r, mod, p["peer_u"], p["peer_vt"], route, p["ln1"], PEER_TOKENS).reshape(B, T, D_MODEL)
    return x2, z, s_fin


def _layer_params(l, lb_all, w_in, attn_lambda, attn_subln, conv_w, conv_b, rec_norm, w_branch, w_out, ln_g, ln_b,
                  peer_wq, peer_keys, peer_u, peer_v):
    lam_init = 0.8 - 0.6 * math.exp(-0.3 * l)
    lp = attn_lambda[l].astype(F32)
    lam = jnp.exp(jnp.sum(lp[0] * lp[1])) - jnp.exp(jnp.sum(lp[2] * lp[3])) + lam_init
    return {
        "w_in": w_in[l].astype(BF16), "lam": lam.reshape(1), "att_scale": 1.0 - lam_init,
        "subln": attn_subln[l].reshape(1, HEAD_W), "lb": lb_all[l],
        "w_branch": w_branch[l].astype(BF16), "w_out": w_out[l].astype(BF16),
        "conv_w": conv_w[l], "conv_b": conv_b[l].reshape(1, D_MODEL),
        "rec_norm": jnp.tile(rec_norm[l], N_HEADS).reshape(1, D_MODEL),
        "ln0": jnp.stack([ln_g[l, 0], ln_b[l, 0]]), "ln1": jnp.stack([ln_g[l, 1], ln_b[l, 1]]),
        "peer_wq": peer_wq[l].astype(BF16),
        "peer_keys": peer_keys[l].astype(BF16).reshape(2 * N_HEADS, PEER_N_KEYS, HEAD_W),
        "peer_u": peer_u[l].astype(BF16),
        "peer_vt": peer_v[l].astype(BF16).reshape(-1, PEER_STEP_KEYS * PEER_N_KEYS, D_MODEL).transpose(0, 2, 1),
    }


def kernel(x_prompt, x_sample, c, cache_attn_k, cache_attn_v, state_hgrn, c_ctx, mod_w, mod_b, w_in, attn_lambda,
           attn_subln, conv_w, conv_b, rec_lb, rec_norm, w_branch, w_out, ln_g, ln_b, peer_wq, peer_keys, peer_u,
           peer_v):
    B, T, _ = x_prompt.shape
    Bs, Ts, _ = x_sample.shape
    P = cache_attn_k.shape[2]
    lb_all = jnp.cumsum(jax.nn.softmax(rec_lb.astype(F32), axis=0), axis=0)
    lb_all = lb_all - lb_all[:1]
    cmat = jnp.concatenate([c_ctx[None, :], c, jnp.zeros((8 - 1 - Bs, D_MODEL), F32)], axis=0)
    mods = _mod_vectors(cmat, mod_w, mod_b).reshape(DEPTH, 8, 6, D_MODEL)
    rope_tabs = _rope_tables(Ts)
    consts = _rec_constants()
    ck = cache_attn_k.reshape(Bs, DEPTH, P, N_HEADS * HEAD_W)
    cv = cache_attn_v.reshape(Bs, DEPTH, P, N_HEADS * HEAD_W)

    y_p, y_s = x_prompt, x_sample
    ks, vs, ss = [], [], []
    for l in range(DEPTH):
        p = _layer_params(l, lb_all, w_in, attn_lambda, attn_subln, conv_w, conv_b, rec_norm, w_branch, w_out,
                          ln_g, ln_b, peer_wq, peer_keys, peer_u, peer_v)
        y_p, z_p, s_p = _trunk_layer(y_p, mods[l, 0:1], l, p, None, None, consts, flat_rows=PROJ_ROWS)
        ks.append(z_p[..., COL_K * D_MODEL:(COL_K + 1) * D_MODEL].reshape(B, T, N_HEADS, HEAD_W))
        vs.append(z_p[..., COL_V * D_MODEL:(COL_V + 1) * D_MODEL].reshape(B, T, N_HEADS, HEAD_W))
        ss.append(s_p)
        y_s, _, _ = _trunk_layer(y_s, mods[l, 1:1 + Bs], l, p, (ck, cv, state_hgrn), rope_tabs, consts, flat_rows=None)
    return (y_p, y_s, jnp.stack(ks, axis=1), jnp.stack(vs, axis=1), jnp.stack(ss, axis=1))
```

```python
import functools
import math

import numpy as np
import jax
import jax.numpy as jnp
from jax import lax
from jax.experimental import pallas as pl
from jax.experimental.pallas import tpu as pltpu

F32 = jnp.float32
BF16 = jnp.bfloat16

D_MODEL = 1024
DEPTH = 2
GRID_W = 64
N_HEADS = 8
HEAD_W = 128
ATT_QK_DIM = 64
ROPE_BASE = 10000.0
ROPE_AXIS_DIM = ATT_QK_DIM // 2
PEER_N_KEYS = 128
PEER_TOPK = 16
LN_EPS = 1e-5
DEEPNORM_ALPHA = (2 * DEPTH) ** 0.25
W_IN_COLS = 14 * D_MODEL
COL_Q, COL_K, COL_V, COL_CB, COL_CC, COL_CX, COL_RQ, COL_RFF, COL_RFB, COL_RI, COL_RG, COL_G = range(12)

VMEM_LIMIT = 56 * 1024 * 1024
PROJ_ROWS = 2048
PROJ_COLS = 512
ATT_Q_ROWS = 1024
ATT_KEY_CHUNK = 512
MERGE_ROWS = 256
PEER_TOKENS = 512
REC_C = 128
REC_LEVELS = 7
REC_HEADS_PER_STEP = 8


def _cparams(sem):
    return pltpu.CompilerParams(dimension_semantics=sem, vmem_limit_bytes=VMEM_LIMIT)


def _dot(a, b):
    return jnp.dot(a, b, preferred_element_type=F32)


def _dot_nt(a, b):
    return lax.dot_general(a, b, (((1,), (1,)), ((), ())), preferred_element_type=F32)


def _sigmoid(x):
    e = jnp.exp(-jnp.abs(x))
    r = 1.0 / (1.0 + e)
    return jnp.where(x >= 0, r, e * r)


def _layer_norm(y, g, b):
    mu = jnp.mean(y, axis=-1, keepdims=True)
    yc = y - mu
    var = jnp.mean(yc * yc, axis=-1, keepdims=True)
    return yc * lax.rsqrt(var + LN_EPS) * g + b


def _mod_body(c_ref, w_ref, b_ref, o_ref):
    c = c_ref[...]
    s = (c * _sigmoid(c)).astype(BF16)
    o_ref[0] = _dot(s, w_ref[0].astype(BF16)) + b_ref[0]


def _mod_vectors(cmat, mod_w, mod_b):
    tn = 1536
    return pl.pallas_call(
        _mod_body,
        grid=(DEPTH, 6 * D_MODEL // tn),
        in_specs=[pl.BlockSpec((8, D_MODEL), lambda l, j: (0, 0)),
                  pl.BlockSpec((1, D_MODEL, tn), lambda l, j: (l, 0, j)),
                  pl.BlockSpec((1, 1, tn), lambda l, j: (l, 0, j))],
        out_specs=pl.BlockSpec((1, 8, tn), lambda l, j: (l, 0, j)),
        out_shape=jax.ShapeDtypeStruct((DEPTH, 8, 6 * D_MODEL), F32),
        compiler_params=_cparams(("parallel", "parallel")),
        name="mod_vectors",
    )(cmat, mod_w, mod_b.reshape(DEPTH, 1, 6 * D_MODEL))


def _in_proj_body(x_ref, mod_ref, w_ref, *refs, rope, tm):
    if rope:
        cos_ref, sin_ref, o_ref, h_ref = refs
    else:
        o_ref, h_ref = refs
    j = pl.program_id(2)

    @pl.when(j == 0)
    def _():
        h_ref[...] = (x_ref[0] * (1.0 + mod_ref[0, 1:2, :]) + mod_ref[0, 0:1, :]).astype(BF16)

    z = _dot(h_ref[...], w_ref[...])
    if not rope:
        o_ref[0] = z
        return
    rope_tiles = COL_V * D_MODEL // PROJ_COLS

    @pl.when(j < rope_tiles)
    def _():
        cos = cos_ref[...]
        sin = sin_ref[...]
        lane = lax.broadcasted_iota(jnp.int32, (tm, HEAD_W), 1)
        first = (lane % ROPE_AXIS_DIM) < (ROPE_AXIS_DIM // 2)
        for g in range(PROJ_COLS // HEAD_W):
            zg = z[:, g * HEAD_W:(g + 1) * HEAD_W]
            partner = jnp.where(first, pltpu.roll(zg, HEAD_W - ROPE_AXIS_DIM // 2, 1),
                                pltpu.roll(zg, ROPE_AXIS_DIM // 2, 1))
            o_ref[0, :, g * HEAD_W:(g + 1) * HEAD_W] = zg * cos + partner * sin

    @pl.when(j >= rope_tiles)
    def _():
        o_ref[0] = z


def _in_proj(x, mod, w_bf16, rope_tabs, tm):
    B, T, _ = x.shape
    per_b = mod.shape[0] > 1
    rope = rope_tabs is not None
    in_specs = [pl.BlockSpec((1, tm, D_MODEL), lambda b, i, j: (b, i, 0)),
                pl.BlockSpec((1, 6, D_MODEL), (lambda b, i, j: (b, 0, 0)) if per_b else (lambda b, i, j: (0, 0, 0))),
                pl.BlockSpec((D_MODEL, PROJ_COLS), lambda b, i, j: (0, j))]
    args = [x, mod, w_bf16]
    if rope:
        in_specs += [pl.BlockSpec((tm, HEAD_W), lambda b, i, j: (i, 0))] * 2
        args += list(rope_tabs)
    return pl.pallas_call(
        functools.partial(_in_proj_body, rope=rope, tm=tm),
        grid=(B, T // tm, W_IN_COLS // PROJ_COLS),
        in_specs=in_specs,
        out_specs=pl.BlockSpec((1, tm, PROJ_COLS), lambda b, i, j: (b, i, j)),
        out_shape=jax.ShapeDtypeStruct((B, T, W_IN_COLS), F32),
        scratch_shapes=[pltpu.VMEM((tm, D_MODEL), BF16)],
        compiler_params=_cparams(("parallel", "parallel", "arbitrary")),
        name="in_proj_rope" if rope else "in_proj",
    )(*args)


def _rope_tables(T):
    t = np.arange(T)
    pos = np.stack([t // GRID_W, t % GRID_W], axis=1).astype(np.float32)
    lane = np.arange(HEAD_W)
    axis = (lane % ATT_QK_DIM) // ROPE_AXIS_DIM
    r = lane % ROPE_AXIS_DIM
    half = ROPE_AXIS_DIM // 2
    freqs = ROPE_BASE ** (-jnp.arange(0, ROPE_AXIS_DIM, 2, dtype=F32) / ROPE_AXIS_DIM)
    ang = jnp.asarray(pos)[:, axis] * freqs[r % half][None, :]
    sign = jnp.asarray(np.where(r < half, -1.0, 1.0).astype(np.float32))[None, :]
    return jnp.cos(ang), jnp.sin(ang) * sign


def _attn_body(lam_ref, q_ref, k_ref, v_ref, *refs, tq, tk, n_self, n_ctx, out_scale):
    if n_ctx:
        kc_ref, vc_ref, g_ref, o_ref, kb_ref, vb_ref = refs
    else:
        g_ref, o_ref, kb_ref, vb_ref = refs
    t_self = n_self * tk

    @pl.when(pl.program_id(2) == 0)
    def _():
        kb_ref[0:t_self, :] = k_ref[0].astype(BF16)
        vb_ref[0:t_self, :] = v_ref[0].astype(BF16)
        if n_ctx:
            kb_ref[t_self:, :] = kc_ref[0, 0].astype(BF16)
            vb_ref[t_self:, :] = vc_ref[0, 0].astype(BF16)

    q = q_ref[0] * (ATT_QK_DIM ** -0.5 * math.log2(math.e))
    lane = lax.broadcasted_iota(jnp.int32, (tq, HEAD_W), 1)
    lo = lane < ATT_QK_DIM
    qs = jnp.concatenate([jnp.where(lo, q, 0.0), jnp.where(lo, 0.0, q)], axis=0).astype(BF16)

    m = jnp.full((2 * tq, 1), -jnp.inf, F32)
    l = jnp.zeros((2 * tq, 1), F32)
    acc = jnp.zeros((2 * tq, HEAD_W), F32)
    for c in range(n_self + n_ctx):
        s = _dot_nt(qs, kb_ref[c * tk:(c + 1) * tk, :])
        m_new = jnp.maximum(m, jnp.max(s, axis=-1, keepdims=True))
        a = jnp.exp2(m - m_new)
        p = jnp.exp2(s - m_new)
        l = a * l + jnp.sum(p, axis=-1, keepdims=True)
        acc = a * acc + _dot(p.astype(BF16), vb_ref[c * tk:(c + 1) * tk, :])
        m = m_new
    o = acc / l
    o = o[:tq] - lam_ref[0] * o[tq:]
    o = o * lax.rsqrt(jnp.mean(o * o, axis=-1, keepdims=True) + LN_EPS) * g_ref[...] * out_scale
    o_ref[0] = o.astype(BF16)


def _attention(z, lam, subln, ctx_kv, layer, out_scale, tq, tk):
    B, T, _ = z.shape
    n_self = T // tk
    in_specs = [pl.BlockSpec(memory_space=pltpu.SMEM),
                pl.BlockSpec((1, tq, HEAD_W), lambda b, h, i: (b, i, COL_Q * N_HEADS + h)),
                pl.BlockSpec((1, T, HEAD_W), lambda b, h, i: (b, 0, COL_K * N_HEADS + h)),
                pl.BlockSpec((1, T, HEAD_W), lambda b, h, i: (b, 0, COL_V * N_HEADS + h))]
    args = [lam, z, z, z]
    t_all = T
    n_ctx = 0
    if ctx_kv is not None:
        P = ctx_kv[0].shape[2]
        assert P == tk
        n_ctx = 1
        t_all = T + P
        in_specs += [pl.BlockSpec((1, 1, P, HEAD_W), lambda b, h, i: (b, layer, 0, h))] * 2
        args += list(ctx_kv)
    in_specs.append(pl.BlockSpec((1, HEAD_W), lambda b, h, i: (0, 0)))
    args.append(subln)
    return pl.pallas_call(
        functools.partial(_attn_body, tq=tq, tk=tk, n_self=n_self, n_ctx=n_ctx, out_scale=out_scale),
        grid=(B, N_HEADS, T // tq),
        in_specs=in_specs,
        out_specs=pl.BlockSpec((1, tq, HEAD_W), lambda b, h, i: (b, i, h)),
        out_shape=jax.ShapeDtypeStruct((B, T, D_MODEL), BF16),
        scratch_shapes=[pltpu.VMEM((t_all, HEAD_W), BF16), pltpu.VMEM((t_all, HEAD_W), BF16)],
        compiler_params=_cparams(("parallel", "parallel", "arbitrary")),
        name="attention_ctx" if n_ctx else "attention",
    )(*args)


def _rec_constants():
    C = REC_C
    t = np.arange(C)[:, None]
    j = np.arange(C)[None, :]
    blocks = [(j <= t), (j > t)]
    for l in range(REC_LEVELS):
        m = 1 << l
        seg0 = (t // m) * m
        odd = ((t // m) % 2) == 1
        blocks.append(np.where(odd, (j >= seg0) & (j <= t), (j > t) & (j <= seg0 + m - 1)))
    mf = np.concatenate(blocks, axis=0).astype(np.float32)
    x = t ^ j
    lv = np.where(x == 0, REC_LEVELS, np.floor(np.log2(np.maximum(x, 1))).astype(np.int64))
    lvf = np.where(j <= t, lv, REC_LEVELS + 1).astype(np.int32)
    mb = mf.reshape(-1, C, C)[:, ::-1, ::-1].reshape(-1, C)
    lvb = lvf[::-1, ::-1]
    mf, mb = np.concatenate([mf, mf], axis=1), np.concatenate([mb, mb], axis=1)
    return (jnp.asarray(mf, BF16), jnp.asarray(np.ascontiguousarray(mb), BF16),
            jnp.asarray(lvf), jnp.asarray(np.ascontiguousarray(lvb)))


def _rec_pair(rq, ri, fx, lb, m_ref, lv, sts, backward):
    C = REC_C
    e = jnp.exp(-jnp.abs(fx))
    r = 1.0 / (1.0 + e)
    pos = fx >= 0
    sig = jnp.where(pos, r, e * r)
    nsig = jnp.where(pos, e * r, r)
    logf = jnp.log(lb + (1.0 - lb) * sig)
    kk = (1.0 - lb) * nsig
    q = rq * _sigmoid(rq)
    hi = logf.astype(BF16)
    mid = (logf - hi.astype(F32)).astype(BF16)
    lf2 = jnp.concatenate([hi, mid], axis=0)

    def expo(blk):
        return _dot(m_ref[blk * C:(blk + 1) * C, :], lf2)

    row = lax.broadcasted_iota(jnp.int32, (C, 2 * HEAD_W), 0)
    if backward:
        row = (C - 1) - row
    qb = q.astype(BF16)
    kb = kk.astype(BF16)
    ws = []
    for l in range(REC_LEVELS):
        odd = ((row >> l) & 1) == 1
        ws.append((jnp.exp(expo(2 + l)) * jnp.where(odd, q, kk)).astype(BF16))
    b_incl = expo(0)
    qi = (q * jnp.exp(b_incl)).astype(BF16)
    ki = (kk * jnp.exp(expo(1))).astype(BF16)
    dec = jnp.exp(b_incl[0:1, :] if backward else b_incl[C - 1:C, :])
    outs, new_sts = [], []
    for g in range(2):
        ls = slice(g * HEAD_W, (g + 1) * HEAD_W)
        a = jnp.where(lv == REC_LEVELS, _dot_nt(qb[:, ls], kb[:, ls]), 0.0)
        for l in range(REC_LEVELS):
            a = jnp.where(lv == l, _dot_nt(ws[l][:, ls], ws[l][:, ls]), a)
        rig = ri[:, ls]
        outs.append(_dot(a.astype(BF16), rig.astype(BF16)) + _dot_nt(qi[:, ls], sts[g].astype(BF16)))
        new_sts.append(sts[g] * dec[:, ls] + _dot(rig.T.astype(BF16), ki[:, ls]))
    return jnp.concatenate(outs, axis=1), new_sts


def _hgrn_body(mf_ref, mb_ref, lvf_ref, lvb_ref, rqf_ref, rif_ref, ff_ref, rqb_ref, rib_ref, fb_ref, lb_ref,
               *refs, has_s0):
    if has_s0:
        s0_ref, of_ref, ob_ref, so_ref, sf_scr, sb_scr = refs
    else:
        of_ref, ob_ref, so_ref, sf_scr, sb_scr = refs
    c = pl.program_id(2)

    @pl.when(c == 0)
    def _():
        for g in range(REC_HEADS_PER_STEP):
            if has_s0:
                sf_scr[g] = s0_ref[0, 0, g].T
                sb_scr[g] = s0_ref[0, 1, g].T
            else:
                sf_scr[g] = jnp.zeros((HEAD_W, HEAD_W), F32)
                sb_scr[g] = jnp.zeros((HEAD_W, HEAD_W), F32)

    for g in range(0, REC_HEADS_PER_STEP, 2):
        ls = slice(g * HEAD_W, (g + 2) * HEAD_W)
        o, st = _rec_pair(rqf_ref[0, :, ls], rif_ref[0, :, ls], ff_ref[0, :, ls], lb_ref[0:1, ls], mf_ref,
                          lvf_ref[...], [sf_scr[g], sf_scr[g + 1]], False)
        of_ref[0, :, ls] = o
        sf_scr[g] = st[0]
        sf_scr[g + 1] = st[1]
        o, st = _rec_pair(rqb_ref[0, :, ls], rib_ref[0, :, ls], fb_ref[0, :, ls], lb_ref[1:2, ls], mb_ref,
                          lvb_ref[...], [sb_scr[g], sb_scr[g + 1]], True)
        ob_ref[0, :, ls] = o
        sb_scr[g] = st[0]
        sb_scr[g + 1] = st[1]

    @pl.when(c == pl.num_programs(2) - 1)
    def _():
        for g in range(REC_HEADS_PER_STEP):
            so_ref[0, 0, g] = sf_scr[g].T
            so_ref[0, 1, g] = sb_scr[g].T


def _hgrn(z, lb, s0, layer, consts):
    B, T, _ = z.shape
    n = T // REC_C
    C = REC_C
    G = REC_HEADS_PER_STEP
    ng = N_HEADS // G
    W = G * HEAD_W

    def fwd(col):
        return pl.BlockSpec((1, C, W), lambda b, h, c: (b, c, col * ng + h))

    def bwd(col):
        return pl.BlockSpec((1, C, W), lambda b, h, c: (b, n - 1 - c, col * ng + h))

    const2 = lambda b, h, c: (0, 0)
    in_specs = [pl.BlockSpec(((2 + REC_LEVELS) * C, 2 * C), const2), pl.BlockSpec(((2 + REC_LEVELS) * C, 2 * C), const2),
                pl.BlockSpec((C, C), const2), pl.BlockSpec((C, C), const2),
                fwd(COL_RQ), fwd(COL_RI), fwd(COL_RFF), bwd(COL_RQ), bwd(COL_RI), bwd(COL_RFB),
                pl.BlockSpec((2, W), lambda b, h, c: (0, h))]
    args = list(consts) + [z] * 6 + [lb]
    if s0 is not None:
        s0v = s0.reshape(B, DEPTH * 2, N_HEADS, HEAD_W, HEAD_W)
        in_specs.append(pl.BlockSpec((1, 2, G, HEAD_W, HEAD_W), lambda b, h, c: (b, layer, h, 0, 0)))
        args.append(s0v)
    return pl.pallas_call(
        functools.partial(_hgrn_body, has_s0=s0 is not None),
        grid=(B, ng, n),
        in_specs=in_specs,
        out_specs=[pl.BlockSpec((1, C, W), lambda b, h, c: (b, c, h)),
                   pl.BlockSpec((1, C, W), lambda b, h, c: (b, n - 1 - c, h)),
                   pl.BlockSpec((1, 2, G, HEAD_W, HEAD_W), lambda b, h, c: (b, 0, h, 0, 0))],
        out_shape=[jax.ShapeDtypeStruct((B, T, D_MODEL), F32), jax.ShapeDtypeStruct((B, T, D_MODEL), F32),
                   jax.ShapeDtypeStruct((B, 2, N_HEADS, HEAD_W, HEAD_W), F32)],
        scratch_shapes=[pltpu.VMEM((G, HEAD_W, HEAD_W), F32), pltpu.VMEM((G, HEAD_W, HEAD_W), F32)],
        compiler_params=_cparams(("parallel", "parallel", "arbitrary")),
        name="hgrn_ctx" if s0 is not None else "hgrn",
    )(*args)


def _merge_body(oatt_ref, zb_ref, zc_ref, zx_ref, zcp_ref, zxp_ref, zcn_ref, zxn_ref, of_ref, ob_ref, rg_ref,
                g0_ref, g1_ref, g2_ref, x_ref, mod_ref, wb_ref, wo_ref, cw_ref, cb_ref, rn_ref, ln_ref,
                x1_ref, h2_ref, *, tm):
    i = pl.program_id(1)
    u = zc_ref[0] * zx_ref[0]
    row = lax.broadcasted_iota(jnp.int32, (tm, D_MODEL), 0)
    prev_ok = (i > 0).astype(F32)
    next_ok = (i < pl.num_programs(1) - 1).astype(F32)
    u_prev_edge = zcp_ref[0, 7:8, :] * zxp_ref[0, 7:8, :] * prev_ok
    u_next_edge = zcn_ref[0, 0:1, :] * zxn_ref[0, 0:1, :] * next_ok
    up = jnp.where(row == 0, u_prev_edge, pltpu.roll(u, 1, 0))
    un = jnp.where(row == tm - 1, u_next_edge, pltpu.roll(u, tm - 1, 0))
    conv = up * cw_ref[0:1, :] + u * cw_ref[1:2, :] + un * cw_ref[2:3, :] + cb_ref[...]
    o_conv = (zb_ref[0] * conv).astype(BF16)
    s = of_ref[0] + ob_ref[0]
    rg = rg_ref[0]
    parts = []
    for h in range(N_HEADS):
        sh = s[:, h * HEAD_W:(h + 1) * HEAD_W]
        parts.append(sh * lax.rsqrt(jnp.mean(sh * sh, axis=-1, keepdims=True) + LN_EPS))
    o_rec = (jnp.concatenate(parts, axis=1) * rn_ref[...] * (rg * _sigmoid(rg))).astype(BF16)
    merged = (_sigmoid(g0_ref[0]) * _dot(oatt_ref[0], wb_ref[0])
              + _sigmoid(g1_ref[0]) * _dot(o_conv, wb_ref[1])
              + _sigmoid(g2_ref[0]) * _dot(o_rec, wb_ref[2]))
    mix = _dot(merged.astype(BF16), wo_ref[...])
    y = DEEPNORM_ALPHA * x_ref[0] + mod_ref[0, 2:3, :] * mix
    x1 = _layer_norm(y, ln_ref[0:1, :], ln_ref[1:2, :])
    x1_ref[0] = x1
    h2_ref[0] = (x1 * (1.0 + mod_ref[0, 4:5, :]) + mod_ref[0, 3:4, :]).astype(BF16)


def _merge(x, z, oatt, o_f, o_b, mod, wb_bf16, wo_bf16, conv_w, conv_b, rec_norm_t, ln_gb, tm):
    B, T, _ = x.shape
    per_b = mod.shape[0] > 1
    nb8 = tm // 8
    last8 = T // 8 - 1

    def col(c):
        return pl.BlockSpec((1, tm, D_MODEL), lambda b, i: (b, i, c))

    def prev8(c):
        return pl.BlockSpec((1, 8, D_MODEL), lambda b, i: (b, jnp.maximum(i * nb8 - 1, 0), c))

    def next8(c):
        return pl.BlockSpec((1, 8, D_MODEL), lambda b, i: (b, jnp.minimum((i + 1) * nb8, last8), c))

    tile = pl.BlockSpec((1, tm, D_MODEL), lambda b, i: (b, i, 0))
    full2 = lambda b, i: (0, 0)
    in_specs = [tile, col(COL_CB), col(COL_CC), col(COL_CX), prev8(COL_CC), prev8(COL_CX), next8(COL_CC), next8(COL_CX),
                tile, tile, col(COL_RG), col(COL_G), col(COL_G + 1), col(COL_G + 2), tile,
                pl.BlockSpec((1, 6, D_MODEL), (lambda b, i: (b, 0, 0)) if per_b else (lambda b, i: (0, 0, 0))),
                pl.BlockSpec((3, D_MODEL, D_MODEL), lambda b, i: (0, 0, 0)),
                pl.BlockSpec((D_MODEL, D_MODEL), full2),
                pl.BlockSpec((3, D_MODEL), full2), pl.BlockSpec((1, D_MODEL), full2),
                pl.BlockSpec((1, D_MODEL), full2), pl.BlockSpec((2, D_MODEL), full2)]
    return pl.pallas_call(
        functools.partial(_merge_body, tm=tm),
        grid=(B, T // tm),
        in_specs=in_specs,
        out_specs=[tile, tile],
        out_shape=[jax.ShapeDtypeStruct((B, T, D_MODEL), F32), jax.ShapeDtypeStruct((B, T, D_MODEL), BF16)],
        compiler_params=_cparams(("parallel", "arbitrary")),
        name="merge",
    )(oatt, z, z, z, z, z, z, z, o_f, o_b, z, z, z, z, x, mod, wb_bf16, wo_bf16, conv_w, conv_b, rec_norm_t, ln_gb)


def _top_rows(s, k):
    n, w = s.shape
    rid = lax.broadcasted_iota(jnp.int32, (n, w), 0).astype(F32)
    rank = jnp.full((n, w), float(k), F32)
    vals = []
    for r in range(k):
        m = jnp.max(s, axis=0, keepdims=True)
        first = jnp.min(jnp.where(s == m, rid, float(n)), axis=0, keepdims=True)
        hit = rid == first
        s = jnp.where(hit, -jnp.inf, s)
        rank = jnp.where(hit, float(r), rank)
        vals.append(m)
    return vals, rank


def _route_body(h_ref, wq_ref, keys_ref, n1_ref, e1_ref, r2_ref, e2_ref, q_scr, s_scr, *, tt):
    q_scr[...] = _dot(h_ref[0], wq_ref[...]).astype(BF16)

    def head(h, carry):
        for p in range(2):
            off = pl.multiple_of((2 * h + p) * HEAD_W, HEAD_W)
            s_scr[p] = _dot_nt(keys_ref[2 * h + p], q_scr[:, pl.ds(off, HEAD_W)])
        for lc in range(tt // HEAD_W):
            ls = slice(lc * HEAD_W, (lc + 1) * HEAD_W)
            s1 = s_scr[0, :, ls]
            s2 = s_scr[1, :, ls]
            v1, r1 = _top_rows(s1, PEER_TOPK)
            v2, r2 = _top_rows(s2, PEER_TOPK)
            v2a = jnp.concatenate(v2, axis=0)
            cands = [v1[0] + v2a] + [v1[a] + v2a[:8] for a in range(1, 8)] + [jnp.concatenate(v1[8:], axis=0) + v2[0]]
            cand = jnp.concatenate(cands, axis=0)
            _, rc = _top_rows(cand, PEER_TOPK)
            sel = rc < PEER_TOPK
            zsum = jnp.sum(jnp.where(sel, jnp.exp(cand - (v1[0] + v2[0])), 0.0), axis=0, keepdims=True)
            selc = jnp.where(sel, 1.0, 0.0)
            n_a = [jnp.sum(selc[0:16], axis=0, keepdims=True)]
            n_a += [jnp.sum(selc[8 + 8 * a:16 + 8 * a], axis=0, keepdims=True) for a in range(1, 8)]
            n_a += [selc[72 + a:73 + a] for a in range(8)]
            n1 = jnp.zeros_like(s1)
            for a in range(PEER_TOPK):
                n1 = jnp.where(r1 == a, n_a[a], n1)
            n1_ref[h, :, ls] = n1
            e1_ref[h, :, ls] = jnp.exp(s1 - v1[0]) * (0.5 / zsum)
            r2_ref[h, :, ls] = r2.astype(BF16)
            e2_ref[h, :, ls] = jnp.exp(s2 - v2[0]).astype(BF16)
        return carry

    lax.fori_loop(0, N_HEADS, head, 0)


def _peer_route(h2, wq_bf16, keys_bf16, tt):
    B, T, _ = h2.shape
    big = pl.BlockSpec((None, N_HEADS, PEER_N_KEYS, tt), lambda b, i: (b, 0, 0, i))
    f32_shape = jax.ShapeDtypeStruct((B, N_HEADS, PEER_N_KEYS, T), F32)
    bf16_shape = jax.ShapeDtypeStruct((B, N_HEADS, PEER_N_KEYS, T), BF16)
    return pl.pallas_call(
        functools.partial(_route_body, tt=tt),
        grid=(B, T // tt),
        in_specs=[pl.BlockSpec((1, tt, D_MODEL), lambda b, i: (b, i, 0)),
                  pl.BlockSpec((D_MODEL, 2 * N_HEADS * HEAD_W), lambda b, i: (0, 0)),
                  pl.BlockSpec((2 * N_HEADS, PEER_N_KEYS, HEAD_W), lambda b, i: (0, 0, 0))],
        out_specs=[big, big, big, big],
        out_shape=[f32_shape, f32_shape, bf16_shape, bf16_shape],
        scratch_shapes=[pltpu.VMEM((tt, 2 * N_HEADS * HEAD_W), BF16), pltpu.VMEM((2, PEER_N_KEYS, tt), F32)],
        compiler_params=_cparams(("parallel", "parallel")),
        name="peer_route",
    )(h2, wq_bf16, keys_bf16)


PEER_STEP_KEYS = 16
PEER_SUB_KEYS = 4


def _dense_body(h_ref, u_ref, vt_ref, n1_ref, e1_ref, r2_ref, e2_ref, x_ref, mod_ref, ln_ref, o_ref,
                act_scr, g_scr, w_scr, acc_scr, r2_scr, e2_scr, *, tt):
    k = pl.program_id(2)

    @pl.when(k == 0)
    def _():
        acc_scr[...] = jnp.zeros_like(acc_scr)
        r2_scr[...] = r2_ref[...]
        e2_scr[...] = e2_ref[...]

    hb = h_ref[0]
    sub = PEER_SUB_KEYS * PEER_N_KEYS

    def routing_weights(jj):
        for j in range(PEER_SUB_KEYS * jj, PEER_SUB_KEYS * (jj + 1)):
            for lc in range(tt // HEAD_W):
                ls = slice(lc * HEAD_W, (lc + 1) * HEAD_W)
                g = None
                for h in range(N_HEADS):
                    n_t = jnp.broadcast_to(n1_ref[h, j:j + 1, ls], (16, HEAD_W)).astype(BF16)[None]
                    e_t = jnp.broadcast_to(e1_ref[h, j:j + 1, ls], (16, HEAD_W)).astype(BF16)[None]
                    hit = r2_scr[h, :, ls].reshape(PEER_N_KEYS // 16, 16, HEAD_W) < n_t
                    term = jnp.where(hit, e2_scr[h, :, ls].reshape(PEER_N_KEYS // 16, 16, HEAD_W), 0.0) * e_t
                    g = term if g is None else g + term
                g_scr[j * PEER_N_KEYS:(j + 1) * PEER_N_KEYS, ls] = g.reshape(PEER_N_KEYS, HEAD_W)
        return lax.shift_right_logical(pltpu.bitcast(g[0], jnp.uint32), jnp.uint32(32))

    def after(x, zero_tile):
        xi = pltpu.bitcast(x, jnp.uint32)
        z = jnp.tile(zero_tile, (xi.shape[0] // zero_tile.shape[0], xi.shape[1] // zero_tile.shape[1]))
        return pltpu.bitcast(xi | z, x.dtype)

    nsub = PEER_STEP_KEYS // PEER_SUB_KEYS
    for jj in range(nsub):
        rows = slice(jj * sub, (jj + 1) * sub)
        zero_tile = routing_weights(jj)
        act_scr[rows, :] = _dot_nt(after(u_ref[rows, :], zero_tile), hb)
    for jj in range(nsub):
        rows = slice(jj * sub, (jj + 1) * sub)
        a = act_scr[rows, :]
        gelu2 = a * (1.0 + lax.erf(a * (2.0 ** -0.5)))
        w_scr[rows, :] = (gelu2 * g_scr[rows, :].astype(F32)).astype(BF16)
        acc_scr[...] += _dot(vt_ref[0, :, rows], w_scr[rows, :])

    @pl.when(k == pl.num_programs(2) - 1)
    def _():
        y = DEEPNORM_ALPHA * x_ref[0] + mod_ref[0, 5:6, :] * acc_scr[...].T
        o_ref[0] = _layer_norm(y, ln_ref[0:1, :], ln_ref[1:2, :])


def _peer_dense(h2, x1, mod, u_bf16, vt_bf16, route, ln_gb, tt):
    B, T, _ = h2.shape
    n1, e1, r2, e2 = route
    per_b = mod.shape[0] > 1
    ne = PEER_STEP_KEYS * PEER_N_KEYS
    rows = pl.BlockSpec((None, N_HEADS, PEER_STEP_KEYS, tt), lambda b, i, k: (b, 0, k, i))
    full = pl.BlockSpec((None, N_HEADS, PEER_N_KEYS, tt), lambda b, i, k: (b, 0, 0, i))
    return pl.pallas_call(
        functools.partial(_dense_body, tt=tt),
        grid=(B, T // tt, PEER_N_KEYS // PEER_STEP_KEYS),
        in_specs=[pl.BlockSpec((1, tt, D_MODEL), lambda b, i, k: (b, i, 0)),
                  pl.BlockSpec((ne, D_MODEL), lambda b, i, k: (k, 0)),
                  pl.BlockSpec((1, D_MODEL, ne), lambda b, i, k: (k, 0, 0)),
                  rows, rows, full, full,
                  pl.BlockSpec((1, tt, D_MODEL), lambda b, i, k: (b, i, 0)),
                  pl.BlockSpec((1, 6, D_MODEL), (lambda b, i, k: (b, 0, 0)) if per_b else (lambda b, i, k: (0, 0, 0))),
                  pl.BlockSpec((2, D_MODEL), lambda b, i, k: (0, 0))],
        out_specs=pl.BlockSpec((1, tt, D_MODEL), lambda b, i, k: (b, i, 0)),
        out_shape=jax.ShapeDtypeStruct((B, T, D_MODEL), F32),
        scratch_shapes=[pltpu.VMEM((ne, tt), F32), pltpu.VMEM((ne, tt), BF16), pltpu.VMEM((ne, tt), BF16),
                        pltpu.VMEM((D_MODEL, tt), F32),
                        pltpu.VMEM((N_HEADS, PEER_N_KEYS, tt), BF16), pltpu.VMEM((N_HEADS, PEER_N_KEYS, tt), BF16)],
        compiler_params=_cparams(("parallel", "parallel", "arbitrary")),
        name="peer_dense",
    )(h2, u_bf16, vt_bf16, n1, e1, r2, e2, x1, mod, ln_gb)


def _trunk_layer(x, mod, layer, p, ctx, rope_tabs, consts, flat_rows):
    B, T, _ = x.shape
    if flat_rows is not None:
        z = _in_proj(x.reshape(-1, flat_rows, D_MODEL), mod, p["w_in"], None, tm=flat_rows)
        z = z.reshape(B, T, W_IN_COLS)
    else:
        z = _in_proj(x, mod, p["w_in"], rope_tabs, tm=PROJ_ROWS)
    if ctx is None:
        oatt = _attention(z, p["lam"], p["subln"], None, layer, p["att_scale"], tq=min(ATT_Q_ROWS, T), tk=T)
        o_f, o_b, s_fin = _hgrn(z, p["lb"], None, layer, consts)
    else:
        oatt = _attention(z, p["lam"], p["subln"], (ctx[0], ctx[1]), layer, p["att_scale"], tq=ATT_Q_ROWS,
                          tk=ATT_KEY_CHUNK)
        o_f, o_b, s_fin = _hgrn(z, p["lb"], ctx[2], layer, consts)
    x1, h2 = _merge(x, z, oatt, o_f, o_b, mod, p["w_branch"], p["w_out"], p["conv_w"], p["conv_b"],
                    p["rec_norm"], p["ln0"], tm=MERGE_ROWS)
    if flat_rows is not None:
        h2r, x1r = h2.reshape(-1, flat_rows, D_MODEL), x1.reshape(-1, flat_rows, D_MODEL)
    else:
        h2r, x1r = h2, x1
    route = _peer_route(h2r, p["peer_wq"], p["peer_keys"], PEER_TOKENS)
    x2 = _peer_dense(h2r, x1r, mod, p["peer_u"], p["peer_vt"], route, p["ln1"], PEER_TOKENS).reshape(B, T, D_MODEL)
    return x2, z, s_fin


def _layer_params(l, lb_all, w_in, attn_lambda, attn_subln, conv_w, conv_b, rec_norm, w_branch, w_out, ln_g, ln_b,
                  peer_wq, peer_keys, peer_u, peer_v):
    lam_init = 0.8 - 0.6 * math.exp(-0.3 * l)
    lp = attn_lambda[l].astype(F32)
    lam = jnp.exp(jnp.sum(lp[0] * lp[1])) - jnp.exp(jnp.sum(lp[2] * lp[3])) + lam_init
    return {
        "w_in": w_in[l].astype(BF16), "lam": lam.reshape(1), "att_scale": 1.0 - lam_init,
        "subln": attn_subln[l].reshape(1, HEAD_W), "lb": lb_all[l],
        "w_branch": w_branch[l].astype(BF16), "w_out": w_out[l].astype(BF16),
        "conv_w": conv_w[l], "conv_b": conv_b[l].reshape(1, D_MODEL),
        "rec_norm": jnp.tile(rec_norm[l], N_HEADS).reshape(1, D_MODEL),
        "ln0": jnp.stack([ln_g[l, 0], ln_b[l, 0]]), "ln1": jnp.stack([ln_g[l, 1], ln_b[l, 1]]),
        "peer_wq": peer_wq[l].astype(BF16),
        "peer_keys": peer_keys[l].astype(BF16).reshape(2 * N_HEADS, PEER_N_KEYS, HEAD_W),
        "peer_u": peer_u[l].astype(BF16),
        "peer_vt": peer_v[l].astype(BF16).reshape(-1, PEER_STEP_KEYS * PEER_N_KEYS, D_MODEL).transpose(0, 2, 1),
    }


def kernel(x_prompt, x_sample, c, cache_attn_k, cache_attn_v, state_hgrn, c_ctx, mod_w, mod_b, w_in, attn_lambda,
           attn_subln, conv_w, conv_b, rec_lb, rec_norm, w_branch, w_out, ln_g, ln_b, peer_wq, peer_keys, peer_u,
           peer_v):
    B, T, _ = x_prompt.shape
    Bs, Ts, _ = x_sample.shape
    P = cache_attn_k.shape[2]
    lb_all = jnp.cumsum(jax.nn.softmax(rec_lb.astype(F32), axis=0), axis=0)
    lb_all = lb_all - lb_all[:1]
    cmat = jnp.concatenate([c_ctx[None, :], c, jnp.zeros((8 - 1 - Bs, D_MODEL), F32)], axis=0)
    mods = _mod_vectors(cmat, mod_w, mod_b).reshape(DEPTH, 8, 6, D_MODEL)
    rope_tabs = _rope_tables(Ts)
    consts = _rec_constants()
    ck = cache_attn_k.reshape(Bs, DEPTH, P, N_HEADS * HEAD_W)
    cv = cache_attn_v.reshape(Bs, DEPTH, P, N_HEADS * HEAD_W)

    y_p, y_s = x_prompt, x_sample
    ks, vs, ss = [], [], []
    for l in range(DEPTH):
        p = _layer_params(l, lb_all, w_in, attn_lambda, attn_subln, conv_w, conv_b, rec_norm, w_branch, w_out,
                          ln_g, ln_b, peer_wq, peer_keys, peer_u, peer_v)
        y_p, z_p, s_p = _trunk_layer(y_p, mods[l, 0:1], l, p, None, None, consts, flat_rows=PROJ_ROWS)
        ks.append(z_p[..., COL_K * D_MODEL:(COL_K + 1) * D_MODEL].reshape(B, T, N_HEADS, HEAD_W))
        vs.append(z_p[..., COL_V * D_MODEL:(COL_V + 1) * D_MODEL].reshape(B, T, N_HEADS, HEAD_W))
        ss.append(s_p)
        y_s, _, _ = _trunk_layer(y_s, mods[l, 1:1 + Bs], l, p, (ck, cv, state_hgrn), rope_tabs, consts, flat_rows=None)
    return (y_p, y_s, jnp.stack(ks, axis=1), jnp.stack(vs, axis=1), jnp.stack(ss, axis=1))
```

```python
import functools
import math

import numpy as np
import jax
import jax.numpy as jnp
from jax import lax
from jax.experimental import pallas as pl
from jax.experimental.pallas import tpu as pltpu

F32 = jnp.float32
BF16 = jnp.bfloat16

D_MODEL = 1024
DEPTH = 2
GRID_W = 64
N_HEADS = 8
HEAD_W = 128
ATT_QK_DIM = 64
ROPE_BASE = 10000.0
ROPE_AXIS_DIM = ATT_QK_DIM // 2
PEER_N_KEYS = 128
PEER_TOPK = 16
LN_EPS = 1e-5
DEEPNORM_ALPHA = (2 * DEPTH) ** 0.25
W_IN_COLS = 14 * D_MODEL
COL_Q, COL_K, COL_V, COL_CB, COL_CC, COL_CX, COL_RQ, COL_RFF, COL_RFB, COL_RI, COL_RG, COL_G = range(12)

VMEM_LIMIT = 56 * 1024 * 1024
PROJ_ROWS = 2048
PROJ_COLS = 512
ATT_Q_ROWS = 1024
ATT_KEY_CHUNK = 512
MERGE_ROWS = 256
PEER_TOKENS = 512
REC_C = 128
REC_LEVELS = 7
REC_HEADS_PER_STEP = 8


def _cparams(sem):
    return pltpu.CompilerParams(dimension_semantics=sem, vmem_limit_bytes=VMEM_LIMIT)


def _dot(a, b):
    return jnp.dot(a, b, preferred_element_type=F32)


def _dot_nt(a, b):
    return lax.dot_general(a, b, (((1,), (1,)), ((), ())), preferred_element_type=F32)


def _sigmoid(x):
    e = jnp.exp(-jnp.abs(x))
    r = 1.0 / (1.0 + e)
    return jnp.where(x >= 0, r, e * r)


def _layer_norm(y, g, b):
    mu = jnp.mean(y, axis=-1, keepdims=True)
    yc = y - mu
    var = jnp.mean(yc * yc, axis=-1, keepdims=True)
    return yc * lax.rsqrt(var + LN_EPS) * g + b


def _mod_body(c_ref, w_ref, b_ref, o_ref):
    c = c_ref[...]
    s = (c * _sigmoid(c)).astype(BF16)
    o_ref[0] = _dot(s, w_ref[0].astype(BF16)) + b_ref[0]


def _mod_vectors(cmat, mod_w, mod_b):
    tn = 1536
    return pl.pallas_call(
        _mod_body,
        grid=(DEPTH, 6 * D_MODEL // tn),
        in_specs=[pl.BlockSpec((8, D_MODEL), lambda l, j: (0, 0)),
                  pl.BlockSpec((1, D_MODEL, tn), lambda l, j: (l, 0, j)),
                  pl.BlockSpec((1, 1, tn), lambda l, j: (l, 0, j))],
        out_specs=pl.BlockSpec((1, 8, tn), lambda l, j: (l, 0, j)),
        out_shape=jax.ShapeDtypeStruct((DEPTH, 8, 6 * D_MODEL), F32),
        compiler_params=_cparams(("parallel", "parallel")),
        name="mod_vectors",
    )(cmat, mod_w, mod_b.reshape(DEPTH, 1, 6 * D_MODEL))


def _in_proj_body(x_ref, mod_ref, w_ref, *refs, rope, tm):
    if rope:
        cos_ref, sin_ref, o_ref, h_ref = refs
    else:
        o_ref, h_ref = refs
    j = pl.program_id(2)

    @pl.when(j == 0)
    def _():
        h_ref[...] = (x_ref[0] * (1.0 + mod_ref[0, 1:2, :]) + mod_ref[0, 0:1, :]).astype(BF16)

    z = _dot(h_ref[...], w_ref[...])
    if not rope:
        o_ref[0] = z
        return
    rope_tiles = COL_V * D_MODEL // PROJ_COLS

    @pl.when(j < rope_tiles)
    def _():
        cos = cos_ref[...]
        sin = sin_ref[...]
        lane = lax.broadcasted_iota(jnp.int32, (tm, HEAD_W), 1)
        first = (lane % ROPE_AXIS_DIM) < (ROPE_AXIS_DIM // 2)
        for g in range(PROJ_COLS // HEAD_W):
            zg = z[:, g * HEAD_W:(g + 1) * HEAD_W]
            partner = jnp.where(first, pltpu.roll(zg, HEAD_W - ROPE_AXIS_DIM // 2, 1),
                                pltpu.roll(zg, ROPE_AXIS_DIM // 2, 1))
            o_ref[0, :, g * HEAD_W:(g + 1) * HEAD_W] = zg * cos + partner * sin

    @pl.when(j >= rope_tiles)
    def _():
        o_ref[0] = z


def _in_proj(x, mod, w_bf16, rope_tabs, tm):
    B, T, _ = x.shape
    per_b = mod.shape[0] > 1
    rope = rope_tabs is not None
    in_specs = [pl.BlockSpec((1, tm, D_MODEL), lambda b, i, j: (b, i, 0)),
                pl.BlockSpec((1, 6, D_MODEL), (lambda b, i, j: (b, 0, 0)) if per_b else (lambda b, i, j: (0, 0, 0))),
                pl.BlockSpec((D_MODEL, PROJ_COLS), lambda b, i, j: (0, j))]
    args = [x, mod, w_bf16]
    if rope:
        in_specs += [pl.BlockSpec((tm, HEAD_W), lambda b, i, j: (i, 0))] * 2
        args += list(rope_tabs)
    return pl.pallas_call(
        functools.partial(_in_proj_body, rope=rope, tm=tm),
        grid=(B, T // tm, W_IN_COLS // PROJ_COLS),
        in_specs=in_specs,
        out_specs=pl.BlockSpec((1, tm, PROJ_COLS), lambda b, i, j: (b, i, j)),
        out_shape=jax.ShapeDtypeStruct((B, T, W_IN_COLS), F32),
        scratch_shapes=[pltpu.VMEM((tm, D_MODEL), BF16)],
        compiler_params=_cparams(("parallel", "parallel", "arbitrary")),
        name="in_proj_rope" if rope else "in_proj",
    )(*args)


def _rope_tables(T):
    t = np.arange(T)
    pos = np.stack([t // GRID_W, t % GRID_W], axis=1).astype(np.float32)
    lane = np.arange(HEAD_W)
    axis = (lane % ATT_QK_DIM) // ROPE_AXIS_DIM
    r = lane % ROPE_AXIS_DIM
    half = ROPE_AXIS_DIM // 2
    freqs = ROPE_BASE ** (-jnp.arange(0, ROPE_AXIS_DIM, 2, dtype=F32) / ROPE_AXIS_DIM)
    ang = jnp.asarray(pos)[:, axis] * freqs[r % half][None, :]
    sign = jnp.asarray(np.where(r < half, -1.0, 1.0).astype(np.float32))[None, :]
    return jnp.cos(ang), jnp.sin(ang) * sign


def _attn_body(lam_ref, q_ref, k_ref, v_ref, *refs, tq, tk, n_self, n_ctx, out_scale):
    if n_ctx:
        kc_ref, vc_ref, g_ref, o_ref, kb_ref, vb_ref = refs
    else:
        g_ref, o_ref, kb_ref, vb_ref = refs
    t_self = n_self * tk

    @pl.when(pl.program_id(2) == 0)
    def _():
        kb_ref[0:t_self, :] = k_ref[0].astype(BF16)
        vb_ref[0:t_self, :] = v_ref[0].astype(BF16)
        if n_ctx:
            kb_ref[t_self:, :] = kc_ref[0, 0].astype(BF16)
            vb_ref[t_self:, :] = vc_ref[0, 0].astype(BF16)

    q = q_ref[0] * (ATT_QK_DIM ** -0.5 * math.log2(math.e))
    lane = lax.broadcasted_iota(jnp.int32, (tq, HEAD_W), 1)
    lo = lane < ATT_QK_DIM
    qs = jnp.concatenate([jnp.where(lo, q, 0.0), jnp.where(lo, 0.0, q)], axis=0).astype(BF16)

    m = jnp.full((2 * tq, 1), -jnp.inf, F32)
    l = jnp.zeros((2 * tq, 1), F32)
    acc = jnp.zeros((2 * tq, HEAD_W), F32)
    for c in range(n_self + n_ctx):
        s = _dot_nt(qs, kb_ref[c * tk:(c + 1) * tk, :])
        m_new = jnp.maximum(m, jnp.max(s, axis=-1, keepdims=True))
        a = jnp.exp2(m - m_new)
        p = jnp.exp2(s - m_new)
        l = a * l + jnp.sum(p, axis=-1, keepdims=True)
        acc = a * acc + _dot(p.astype(BF16), vb_ref[c * tk:(c + 1) * tk, :])
        m = m_new
    o = acc / l
    o = o[:tq] - lam_ref[0] * o[tq:]
    o = o * lax.rsqrt(jnp.mean(o * o, axis=-1, keepdims=True) + LN_EPS) * g_ref[...] * out_scale
    o_ref[0] = o.astype(BF16)


def _attention(z, lam, subln, ctx_kv, layer, out_scale, tq, tk):
    B, T, _ = z.shape
    n_self = T // tk
    in_specs = [pl.BlockSpec(memory_space=pltpu.SMEM),
                pl.BlockSpec((1, tq, HEAD_W), lambda b, h, i: (b, i, COL_Q * N_HEADS + h)),
                pl.BlockSpec((1, T, HEAD_W), lambda b, h, i: (b, 0, COL_K * N_HEADS + h)),
                pl.BlockSpec((1, T, HEAD_W), lambda b, h, i: (b, 0, COL_V * N_HEADS + h))]
    args = [lam, z, z, z]
    t_all = T
    n_ctx = 0
    if ctx_kv is not None:
        P = ctx_kv[0].shape[2]
        assert P == tk
        n_ctx = 1
        t_all = T + P
        in_specs += [pl.BlockSpec((1, 1, P, HEAD_W), lambda b, h, i: (b, layer, 0, h))] * 2
        args += list(ctx_kv)
    in_specs.append(pl.BlockSpec((1, HEAD_W), lambda b, h, i: (0, 0)))
    args.append(subln)
    return pl.pallas_call(
        functools.partial(_attn_body, tq=tq, tk=tk, n_self=n_self, n_ctx=n_ctx, out_scale=out_scale),
        grid=(B, N_HEADS, T // tq),
        in_specs=in_specs,
        out_specs=pl.BlockSpec((1, tq, HEAD_W), lambda b, h, i: (b, i, h)),
        out_shape=jax.ShapeDtypeStruct((B, T, D_MODEL), BF16),
        scratch_shapes=[pltpu.VMEM((t_all, HEAD_W), BF16), pltpu.VMEM((t_all, HEAD_W), BF16)],
        compiler_params=_cparams(("parallel", "parallel", "arbitrary")),
        name="attention_ctx" if n_ctx else "attention",
    )(*args)


def _rec_constants():
    C = REC_C
    t = np.arange(C)[:, None]
    j = np.arange(C)[None, :]
    blocks = [(j <= t), (j > t)]
    for l in range(REC_LEVELS):
        m = 1 << l
        seg0 = (t // m) * m
        odd = ((t // m) % 2) == 1
        blocks.append(np.where(odd, (j >= seg0) & (j <= t), (j > t) & (j <= seg0 + m - 1)))
    mf = np.concatenate(blocks, axis=0).astype(np.float32)
    x = t ^ j
    lv = np.where(x == 0, REC_LEVELS, np.floor(np.log2(np.maximum(x, 1))).astype(np.int64))
    lvf = np.where(j <= t, lv, REC_LEVELS + 1).astype(np.int32)
    mb = mf.reshape(-1, C, C)[:, ::-1, ::-1].reshape(-1, C)
    lvb = lvf[::-1, ::-1]
    mf, mb = np.concatenate([mf, mf], axis=1), np.concatenate([mb, mb], axis=1)
    return (jnp.asarray(mf, BF16), jnp.asarray(np.ascontiguousarray(mb), BF16),
            jnp.asarray(lvf), jnp.asarray(np.ascontiguousarray(lvb)))


def _rec_pair(rq, ri, fx, lb, m_ref, lv, sts, backward):
    C = REC_C
    e = jnp.exp(-jnp.abs(fx))
    r = 1.0 / (1.0 + e)
    pos = fx >= 0
    sig = jnp.where(pos, r, e * r)
    nsig = jnp.where(pos, e * r, r)
    logf = jnp.log(lb + (1.0 - lb) * sig)
    kk = (1.0 - lb) * nsig
    q = rq * _sigmoid(rq)
    hi = logf.astype(BF16)
    mid = (logf - hi.astype(F32)).astype(BF16)
    lf2 = jnp.concatenate([hi, mid], axis=0)

    def expo(blk):
        return _dot(m_ref[blk * C:(blk + 1) * C, :], lf2)

    row = lax.broadcasted_iota(jnp.int32, (C, 2 * HEAD_W), 0)
    if backward:
        row = (C - 1) - row
    qb = q.astype(BF16)
    kb = kk.astype(BF16)
    ws = []
    for l in range(REC_LEVELS):
        odd = ((row >> l) & 1) == 1
        ws.append((jnp.exp(expo(2 + l)) * jnp.where(odd, q, kk)).astype(BF16))
    b_incl = expo(0)
    qi = (q * jnp.exp(b_incl)).astype(BF16)
    ki = (kk * jnp.exp(expo(1))).astype(BF16)
    dec = jnp.exp(b_incl[0:1, :] if backward else b_incl[C - 1:C, :])
    outs, new_sts = [], []
    for g in range(2):
        ls = slice(g * HEAD_W, (g + 1) * HEAD_W)
        a = jnp.where(lv == REC_LEVELS, _dot_nt(qb[:, ls], kb[:, ls]), 0.0)
        for l in range(REC_LEVELS):
            a = jnp.where(lv == l, _dot_nt(ws[l][:, ls], ws[l][:, ls]), a)
        rig = ri[:, ls]
        outs.append(_dot(a.astype(BF16), rig.astype(BF16)) + _dot_nt(qi[:, ls], sts[g].astype(BF16)))
        new_sts.append(sts[g] * dec[:, ls] + _dot(rig.T.astype(BF16), ki[:, ls]))
    return jnp.concatenate(outs, axis=1), new_sts


def _hgrn_body(mf_ref, mb_ref, lvf_ref, lvb_ref, rqf_ref, rif_ref, ff_ref, rqb_ref, rib_ref, fb_ref, lb_ref,
               *refs, has_s0):
    if has_s0:
        s0_ref, of_ref, ob_ref, so_ref, sf_scr, sb_scr = refs
    else:
        of_ref, ob_ref, so_ref, sf_scr, sb_scr = refs
    c = pl.program_id(2)

    @pl.when(c == 0)
    def _():
        for g in range(REC_HEADS_PER_STEP):
            if has_s0:
                sf_scr[g] = s0_ref[0, 0, g].T
                sb_scr[g] = s0_ref[0, 1, g].T
            else:
                sf_scr[g] = jnp.zeros((HEAD_W, HEAD_W), F32)
                sb_scr[g] = jnp.zeros((HEAD_W, HEAD_W), F32)

    for g in range(0, REC_HEADS_PER_STEP, 2):
        ls = slice(g * HEAD_W, (g + 2) * HEAD_W)
        o, st = _rec_pair(rqf_ref[0, :, ls], rif_ref[0, :, ls], ff_ref[0, :, ls], lb_ref[0:1, ls], mf_ref,
                          lvf_ref[...], [sf_scr[g], sf_scr[g + 1]], False)
        of_ref[0, :, ls] = o
        sf_scr[g] = st[0]
        sf_scr[g + 1] = st[1]
        o, st = _rec_pair(rqb_ref[0, :, ls], rib_ref[0, :, ls], fb_ref[0, :, ls], lb_ref[1:2, ls], mb_ref,
                          lvb_ref[...], [sb_scr[g], sb_scr[g + 1]], True)
        ob_ref[0, :, ls] = o
        sb_scr[g] = st[0]
        sb_scr[g + 1] = st[1]

    @pl.when(c == pl.num_programs(2) - 1)
    def _():
        for g in range(REC_HEADS_PER_STEP):
            so_ref[0, 0, g] = sf_scr[g].T
            so_ref[0, 1, g] = sb_scr[g].T


def _hgrn(z, lb, s0, layer, consts):
    B, T, _ = z.shape
    n = T // REC_C
    C = REC_C
    G = REC_HEADS_PER_STEP
    ng = N_HEADS // G
    W = G * HEAD_W

    def fwd(col):
        return pl.BlockSpec((1, C, W), lambda b, h, c: (b, c, col * ng + h))

    def bwd(col):
        return pl.BlockSpec((1, C, W), lambda b, h, c: (b, n - 1 - c, col * ng + h))

    const2 = lambda b, h, c: (0, 0)
    in_specs = [pl.BlockSpec(((2 + REC_LEVELS) * C, 2 * C), const2), pl.BlockSpec(((2 + REC_LEVELS) * C, 2 * C), const2),
                pl.BlockSpec((C, C), const2), pl.BlockSpec((C, C), const2),
                fwd(COL_RQ), fwd(COL_RI), fwd(COL_RFF), bwd(COL_RQ), bwd(COL_RI), bwd(COL_RFB),
                pl.BlockSpec((2, W), lambda b, h, c: (0, h))]
    args = list(consts) + [z] * 6 + [lb]
    if s0 is not None:
        s0v = s0.reshape(B, DEPTH * 2, N_HEADS, HEAD_W, HEAD_W)
        in_specs.append(pl.BlockSpec((1, 2, G, HEAD_W, HEAD_W), lambda b, h, c: (b, layer, h, 0, 0)))
        args.append(s0v)
    return pl.pallas_call(
        functools.partial(_hgrn_body, has_s0=s0 is not None),
        grid=(B, ng, n),
        in_specs=in_specs,
        out_specs=[pl.BlockSpec((1, C, W), lambda b, h, c: (b, c, h)),
                   pl.BlockSpec((1, C, W), lambda b, h, c: (b, n - 1 - c, h)),
                   pl.BlockSpec((1, 2, G, HEAD_W, HEAD_W), lambda b, h, c: (b, 0, h, 0, 0))],
        out_shape=[jax.ShapeDtypeStruct((B, T, D_MODEL), F32), jax.ShapeDtypeStruct((B, T, D_MODEL), F32),
                   jax.ShapeDtypeStruct((B, 2, N_HEADS, HEAD_W, HEAD_W), F32)],
        scratch_shapes=[pltpu.VMEM((G, HEAD_W, HEAD_W), F32), pltpu.VMEM((G, HEAD_W, HEAD_W), F32)],
        compiler_params=_cparams(("parallel", "parallel", "arbitrary")),
        name="hgrn_ctx" if s0 is not None else "hgrn",
    )(*args)


def _merge_body(oatt_ref, zb_ref, zc_ref, zx_ref, zcp_ref, zxp_ref, zcn_ref, zxn_ref, of_ref, ob_ref, rg_ref,
                g0_ref, g1_ref, g2_ref, x_ref, mod_ref, wb_ref, wo_ref, cw_ref, cb_ref, rn_ref, ln_ref,
                x1_ref, h2_ref, *, tm):
    i = pl.program_id(1)
    u = zc_ref[0] * zx_ref[0]
    row = lax.broadcasted_iota(jnp.int32, (tm, D_MODEL), 0)
    prev_ok = (i > 0).astype(F32)
    next_ok = (i < pl.num_programs(1) - 1).astype(F32)
    u_prev_edge = zcp_ref[0, 7:8, :] * zxp_ref[0, 7:8, :] * prev_ok
    u_next_edge = zcn_ref[0, 0:1, :] * zxn_ref[0, 0:1, :] * next_ok
    up = jnp.where(row == 0, u_prev_edge, pltpu.roll(u, 1, 0))
    un = jnp.where(row == tm - 1, u_next_edge, pltpu.roll(u, tm - 1, 0))
    conv = up * cw_ref[0:1, :] + u * cw_ref[1:2, :] + un * cw_ref[2:3, :] + cb_ref[...]
    o_conv = (zb_ref[0] * conv).astype(BF16)
    s = of_ref[0] + ob_ref[0]
    rg = rg_ref[0]
    parts = []
    for h in range(N_HEADS):
        sh = s[:, h * HEAD_W:(h + 1) * HEAD_W]
        parts.append(sh * lax.rsqrt(jnp.mean(sh * sh, axis=-1, keepdims=True) + LN_EPS))
    o_rec = (jnp.concatenate(parts, axis=1) * rn_ref[...] * (rg * _sigmoid(rg))).astype(BF16)
    merged = (_sigmoid(g0_ref[0]) * _dot(oatt_ref[0], wb_ref[0])
              + _sigmoid(g1_ref[0]) * _dot(o_conv, wb_ref[1])
              + _sigmoid(g2_ref[0]) * _dot(o_rec, wb_ref[2]))
    mix = _dot(merged.astype(BF16), wo_ref[...])
    y = DEEPNORM_ALPHA * x_ref[0] + mod_ref[0, 2:3, :] * mix
    x1 = _layer_norm(y, ln_ref[0:1, :], ln_ref[1:2, :])
    x1_ref[0] = x1
    h2_ref[0] = (x1 * (1.0 + mod_ref[0, 4:5, :]) + mod_ref[0, 3:4, :]).astype(BF16)


def _merge(x, z, oatt, o_f, o_b, mod, wb_bf16, wo_bf16, conv_w, conv_b, rec_norm_t, ln_gb, tm):
    B, T, _ = x.shape
    per_b = mod.shape[0] > 1
    nb8 = tm // 8
    last8 = T // 8 - 1

    def col(c):
        return pl.BlockSpec((1, tm, D_MODEL), lambda b, i: (b, i, c))

    def prev8(c):
        return pl.BlockSpec((1, 8, D_MODEL), lambda b, i: (b, jnp.maximum(i * nb8 - 1, 0), c))

    def next8(c):
        return pl.BlockSpec((1, 8, D_MODEL), lambda b, i: (b, jnp.minimum((i + 1) * nb8, last8), c))

    tile = pl.BlockSpec((1, tm, D_MODEL), lambda b, i: (b, i, 0))
    full2 = lambda b, i: (0, 0)
    in_specs = [tile, col(COL_CB), col(COL_CC), col(COL_CX), prev8(COL_CC), prev8(COL_CX), next8(COL_CC), next8(COL_CX),
                tile, tile, col(COL_RG), col(COL_G), col(COL_G + 1), col(COL_G + 2), tile,
                pl.BlockSpec((1, 6, D_MODEL), (lambda b, i: (b, 0, 0)) if per_b else (lambda b, i: (0, 0, 0))),
                pl.BlockSpec((3, D_MODEL, D_MODEL), lambda b, i: (0, 0, 0)),
                pl.BlockSpec((D_MODEL, D_MODEL), full2),
                pl.BlockSpec((3, D_MODEL), full2), pl.BlockSpec((1, D_MODEL), full2),
                pl.BlockSpec((1, D_MODEL), full2), pl.BlockSpec((2, D_MODEL), full2)]
    return pl.pallas_call(
        functools.partial(_merge_body, tm=tm),
        grid=(B, T // tm),
        in_specs=in_specs,
        out_specs=[tile, tile],
        out_shape=[jax.ShapeDtypeStruct((B, T, D_MODEL), F32), jax.ShapeDtypeStruct((B, T, D_MODEL), BF16)],
        compiler_params=_cparams(("parallel", "arbitrary")),
        name="merge",
    )(oatt, z, z, z, z, z, z, z, o_f, o_b, z, z, z, z, x, mod, wb_bf16, wo_bf16, conv_w, conv_b, rec_norm_t, ln_gb)


def _top_rows(s, k, one_per_round):
    n, w = s.shape
    rid = lax.broadcasted_iota(jnp.int32, (n, w), 0).astype(F32)
    rank = jnp.full((n, w), float(k), F32)
    vals = []
    for r in range(k):
        m = jnp.max(s, axis=0, keepdims=True)
        hit = s == m
        if one_per_round:
            first = jnp.min(jnp.where(hit, rid, float(n)), axis=0, keepdims=True)
            hit = rid == first
        s = jnp.where(hit, -jnp.inf, s)
        rank = jnp.where(hit, float(r), rank)
        vals.append(m)
    excess = jnp.sum(jnp.where(rank < k, 1.0, 0.0), axis=0, keepdims=True) - k
    return vals, rank, excess


def _route_body(h_ref, wq_ref, keys_ref, n1_ref, e1_ref, r2_ref, e2_ref, q_scr, s_scr, *, tt):
    q_scr[...] = _dot(h_ref[0], wq_ref[...]).astype(BF16)

    def head(h, carry):
        for p in range(2):
            off = pl.multiple_of((2 * h + p) * HEAD_W, HEAD_W)
            s_scr[p] = _dot_nt(keys_ref[2 * h + p], q_scr[:, pl.ds(off, HEAD_W)])
        excess = route_head(h, False)

        @pl.when(jnp.max(excess) > 0.0)
        def _():
            route_head(h, True)

        return carry

    def route_head(h, one_per_round):
        excess = jnp.zeros((1, HEAD_W), F32)
        for lc in range(tt // HEAD_W):
            ls = slice(lc * HEAD_W, (lc + 1) * HEAD_W)
            s1 = s_scr[0, :, ls]
            s2 = s_scr[1, :, ls]
            v1, r1, x1 = _top_rows(s1, PEER_TOPK, one_per_round)
            v2, r2, x2 = _top_rows(s2, PEER_TOPK, one_per_round)
            v2a = jnp.concatenate(v2, axis=0)
            cands = [v1[0] + v2a] + [v1[a] + v2a[:8] for a in range(1, 8)] + [jnp.concatenate(v1[8:], axis=0) + v2[0]]
            cand = jnp.concatenate(cands, axis=0)
            _, rc, xc = _top_rows(cand, PEER_TOPK, one_per_round)
            excess = jnp.maximum(excess, jnp.maximum(jnp.maximum(x1, x2), xc))
            sel = rc < PEER_TOPK
            zsum = jnp.sum(jnp.where(sel, jnp.exp(cand - (v1[0] + v2[0])), 0.0), axis=0, keepdims=True)
            selc = jnp.where(sel, 1.0, 0.0)
            n_a = [jnp.sum(selc[0:16], axis=0, keepdims=True)]
            n_a += [jnp.sum(selc[8 + 8 * a:16 + 8 * a], axis=0, keepdims=True) for a in range(1, 8)]
            n_a += [selc[72 + a:73 + a] for a in range(8)]
            n1 = jnp.zeros_like(s1)
            for a in range(PEER_TOPK):
                n1 = jnp.where(r1 == a, n_a[a], n1)
            n1_ref[h, :, ls] = n1
            e1_ref[h, :, ls] = jnp.exp(s1 - v1[0]) * (0.5 / zsum)
            r2_ref[h, :, ls] = r2.astype(BF16)
            e2_ref[h, :, ls] = jnp.exp(s2 - v2[0]).astype(BF16)
        return excess

    lax.fori_loop(0, N_HEADS, head, 0)


def _peer_route(h2, wq_bf16, keys_bf16, tt):
    B, T, _ = h2.shape
    big = pl.BlockSpec((None, N_HEADS, PEER_N_KEYS, tt), lambda b, i: (b, 0, 0, i))
    f32_shape = jax.ShapeDtypeStruct((B, N_HEADS, PEER_N_KEYS, T), F32)
    bf16_shape = jax.ShapeDtypeStruct((B, N_HEADS, PEER_N_KEYS, T), BF16)
    return pl.pallas_call(
        functools.partial(_route_body, tt=tt),
        grid=(B, T // tt),
        in_specs=[pl.BlockSpec((1, tt, D_MODEL), lambda b, i: (b, i, 0)),
                  pl.BlockSpec((D_MODEL, 2 * N_HEADS * HEAD_W), lambda b, i: (0, 0)),
                  pl.BlockSpec((2 * N_HEADS, PEER_N_KEYS, HEAD_W), lambda b, i: (0, 0, 0))],
        out_specs=[big, big, big, big],
        out_shape=[f32_shape, f32_shape, bf16_shape, bf16_shape],
        scratch_shapes=[pltpu.VMEM((tt, 2 * N_HEADS * HEAD_W), BF16), pltpu.VMEM((2, PEER_N_KEYS, tt), F32)],
        compiler_params=_cparams(("parallel", "parallel")),
        name="peer_route",
    )(h2, wq_bf16, keys_bf16)


PEER_STEP_KEYS = 16
PEER_SUB_KEYS = 4


def _dense_body(h_ref, u_ref, vt_ref, n1_ref, e1_ref, r2_ref, e2_ref, x_ref, mod_ref, ln_ref, o_ref,
                act_scr, g_scr, w_scr, acc_scr, r2_scr, e2_scr, *, tt):
    k = pl.program_id(2)

    @pl.when(k == 0)
    def _():
        acc_scr[...] = jnp.zeros_like(acc_scr)
        r2_scr[...] = r2_ref[...]
        e2_scr[...] = e2_ref[...]

    hb = h_ref[0]
    sub = PEER_SUB_KEYS * PEER_N_KEYS

    def routing_weights(jj):
        for j in range(PEER_SUB_KEYS * jj, PEER_SUB_KEYS * (jj + 1)):
            for lc in range(tt // HEAD_W):
                ls = slice(lc * HEAD_W, (lc + 1) * HEAD_W)
                g = None
                for h in range(N_HEADS):
                    n_t = jnp.broadcast_to(n1_ref[h, j:j + 1, ls], (16, HEAD_W)).astype(BF16)[None]
                    e_t = jnp.broadcast_to(e1_ref[h, j:j + 1, ls], (16, HEAD_W)).astype(BF16)[None]
                    hit = r2_scr[h, :, ls].reshape(PEER_N_KEYS // 16, 16, HEAD_W) < n_t
                    term = jnp.where(hit, e2_scr[h, :, ls].reshape(PEER_N_KEYS // 16, 16, HEAD_W), 0.0) * e_t
                    g = term if g is None else g + term
                g_scr[j * PEER_N_KEYS:(j + 1) * PEER_N_KEYS, ls] = g.reshape(PEER_N_KEYS, HEAD_W)
        return lax.shift_right_logical(pltpu.bitcast(g[0], jnp.uint32), jnp.uint32(32))

    def after(x, zero_tile):
        xi = pltpu.bitcast(x, jnp.uint32)
        z = jnp.tile(zero_tile, (xi.shape[0] // zero_tile.shape[0], xi.shape[1] // zero_tile.shape[1]))
        return pltpu.bitcast(xi | z, x.dtype)

    nsub = PEER_STEP_KEYS // PEER_SUB_KEYS
    for jj in range(nsub):
        rows = slice(jj * sub, (jj + 1) * sub)
        zero_tile = routing_weights(jj)
        act_scr[rows, :] = _dot_nt(after(u_ref[rows, :], zero_tile), hb)
    for jj in range(nsub):
        rows = slice(jj * sub, (jj + 1) * sub)
        a = act_scr[rows, :]
        gelu2 = a * (1.0 + lax.erf(a * (2.0 ** -0.5)))
        w_scr[rows, :] = (gelu2 * g_scr[rows, :].astype(F32)).astype(BF16)
        acc_scr[...] += _dot(vt_ref[0, :, rows], w_scr[rows, :])

    @pl.when(k == pl.num_programs(2) - 1)
    def _():
        y = DEEPNORM_ALPHA * x_ref[0] + mod_ref[0, 5:6, :] * acc_scr[...].T
        o_ref[0] = _layer_norm(y, ln_ref[0:1, :], ln_ref[1:2, :])


def _peer_dense(h2, x1, mod, u_bf16, vt_bf16, route, ln_gb, tt):
    B, T, _ = h2.shape
    n1, e1, r2, e2 = route
    per_b = mod.shape[0] > 1
    ne = PEER_STEP_KEYS * PEER_N_KEYS
    rows = pl.BlockSpec((None, N_HEADS, PEER_STEP_KEYS, tt), lambda b, i, k: (b, 0, k, i))
    full = pl.BlockSpec((None, N_HEADS, PEER_N_KEYS, tt), lambda b, i, k: (b, 0, 0, i))
    return pl.pallas_call(
        functools.partial(_dense_body, tt=tt),
        grid=(B, T // tt, PEER_N_KEYS // PEER_STEP_KEYS),
        in_specs=[pl.BlockSpec((1, tt, D_MODEL), lambda b, i, k: (b, i, 0)),
                  pl.BlockSpec((ne, D_MODEL), lambda b, i, k: (k, 0)),
                  pl.BlockSpec((1, D_MODEL, ne), lambda b, i, k: (k, 0, 0)),
                  rows, rows, full, full,
                  pl.BlockSpec((1, tt, D_MODEL), lambda b, i, k: (b, i, 0)),
                  pl.BlockSpec((1, 6, D_MODEL), (lambda b, i, k: (b, 0, 0)) if per_b else (lambda b, i, k: (0, 0, 0))),
                  pl.BlockSpec((2, D_MODEL), lambda b, i, k: (0, 0))],
        out_specs=pl.BlockSpec((1, tt, D_MODEL), lambda b, i, k: (b, i, 0)),
        out_shape=jax.ShapeDtypeStruct((B, T, D_MODEL), F32),
        scratch_shapes=[pltpu.VMEM((ne, tt), F32), pltpu.VMEM((ne, tt), BF16), pltpu.VMEM((ne, tt), BF16),
                        pltpu.VMEM((D_MODEL, tt), F32),
                        pltpu.VMEM((N_HEADS, PEER_N_KEYS, tt), BF16), pltpu.VMEM((N_HEADS, PEER_N_KEYS, tt), BF16)],
        compiler_params=_cparams(("parallel", "parallel", "arbitrary")),
        name="peer_dense",
    )(h2, u_bf16, vt_bf16, n1, e1, r2, e2, x1, mod, ln_gb)


def _trunk_layer(x, mod, layer, p, ctx, rope_tabs, consts, flat_rows):
    B, T, _ = x.shape
    if flat_rows is not None:
        z = _in_proj(x.reshape(-1, flat_rows, D_MODEL), mod, p["w_in"], None, tm=flat_rows)
        z = z.reshape(B, T, W_IN_COLS)
    else:
        z = _in_proj(x, mod, p["w_in"], rope_tabs, tm=PROJ_ROWS)
    if ctx is None:
        oatt = _attention(z, p["lam"], p["subln"], None, layer, p["att_scale"], tq=min(ATT_Q_ROWS, T), tk=T)
        o_f, o_b, s_fin = _hgrn(z, p["lb"], None, layer, consts)
    else:
        oatt = _attention(z, p["lam"], p["subln"], (ctx[0], ctx[1]), layer, p["att_scale"], tq=ATT_Q_ROWS,
                          tk=ATT_KEY_CHUNK)
        o_f, o_b, s_fin = _hgrn(z, p["lb"], ctx[2], layer, consts)
    x1, h2 = _merge(x, z, oatt, o_f, o_b, mod, p["w_branch"], p["w_out"], p["conv_w"], p["conv_b"],
                    p["rec_norm"], p["ln0"], tm=MERGE_ROWS)
    if flat_rows is not None:
        h2r, x1r = h2.reshape(-1, flat_rows, D_MODEL), x1.reshape(-1, flat_rows, D_MODEL)
    else:
        h2r, x1r = h2, x1
    route = _peer_route(h2r, p["peer_wq"], p["peer_keys"], PEER_TOKENS)
    x2 = _peer_dense(h2r, x1r, mod, p["peer_u"], p["peer_vt"], route, p["ln1"], PEER_TOKENS).reshape(B, T, D_MODEL)
    return x2, z, s_fin


def _layer_params(l, lb_all, w_in, attn_lambda, attn_subln, conv_w, conv_b, rec_norm, w_branch, w_out, ln_g, ln_b,
                  peer_wq, peer_keys, peer_u, peer_v):
    lam_init = 0.8 - 0.6 * math.exp(-0.3 * l)
    lp = attn_lambda[l].astype(F32)
    lam = jnp.exp(jnp.sum(lp[0] * lp[1])) - jnp.exp(jnp.sum(lp[2] * lp[3])) + lam_init
    return {
        "w_in": w_in[l].astype(BF16), "lam": lam.reshape(1), "att_scale": 1.0 - lam_init,
        "subln": attn_subln[l].reshape(1, HEAD_W), "lb": lb_all[l],
        "w_branch": w_branch[l].astype(BF16), "w_out": w_out[l].astype(BF16),
        "conv_w": conv_w[l], "conv_b": conv_b[l].reshape(1, D_MODEL),
        "rec_norm": jnp.tile(rec_norm[l], N_HEADS).reshape(1, D_MODEL),
        "ln0": jnp.stack([ln_g[l, 0], ln_b[l, 0]]), "ln1": jnp.stack([ln_g[l, 1], ln_b[l, 1]]),
        "peer_wq": peer_wq[l].astype(BF16),
        "peer_keys": peer_keys[l].astype(BF16).reshape(2 * N_HEADS, PEER_N_KEYS, HEAD_W),
        "peer_u": peer_u[l].astype(BF16),
        "peer_vt": peer_v[l].astype(BF16).reshape(-1, PEER_STEP_KEYS * PEER_N_KEYS, D_MODEL).transpose(0, 2, 1),
    }


def kernel(x_prompt, x_sample, c, cache_attn_k, cache_attn_v, state_hgrn, c_ctx, mod_w, mod_b, w_in, attn_lambda,
           attn_subln, conv_w, conv_b, rec_lb, rec_norm, w_branch, w_out, ln_g, ln_b, peer_wq, peer_keys, peer_u,
           peer_v):
    B, T, _ = x_prompt.shape
    Bs, Ts, _ = x_sample.shape
    P = cache_attn_k.shape[2]
    lb_all = jnp.cumsum(jax.nn.softmax(rec_lb.astype(F32), axis=0), axis=0)
    lb_all = lb_all - lb_all[:1]
    cmat = jnp.concatenate([c_ctx[None, :], c, jnp.zeros((8 - 1 - Bs, D_MODEL), F32)], axis=0)
    mods = _mod_vectors(cmat, mod_w, mod_b).reshape(DEPTH, 8, 6, D_MODEL)
    rope_tabs = _rope_tables(Ts)
    consts = _rec_constants()
    ck = cache_attn_k.reshape(Bs, DEPTH, P, N_HEADS * HEAD_W)
    cv = cache_attn_v.reshape(Bs, DEPTH, P, N_HEADS * HEAD_W)

    y_p, y_s = x_prompt, x_sample
    ks, vs, ss = [], [], []
    for l in range(DEPTH):
        p = _layer_params(l, lb_all, w_in, attn_lambda, attn_subln, conv_w, conv_b, rec_norm, w_branch, w_out,
                          ln_g, ln_b, peer_wq, peer_keys, peer_u, peer_v)
        y_p, z_p, s_p = _trunk_layer(y_p, mods[l, 0:1], l, p, None, None, consts, flat_rows=PROJ_ROWS)
        ks.append(z_p[..., COL_K * D_MODEL:(COL_K + 1) * D_MODEL].reshape(B, T, N_HEADS, HEAD_W))
        vs.append(z_p[..., COL_V * D_MODEL:(COL_V + 1) * D_MODEL].reshape(B, T, N_HEADS, HEAD_W))
        ss.append(s_p)
        y_s, _, _ = _trunk_layer(y_s, mods[l, 1:1 + Bs], l, p, (ck, cv, state_hgrn), rope_tabs, consts, flat_rows=None)
    return (y_p, y_s, jnp.stack(ks, axis=1), jnp.stack(vs, axis=1), jnp.stack(ss, axis=1))
```

```python
import functools
import math

import numpy as np
import jax
import jax.numpy as jnp
from jax import lax
from jax.experimental import pallas as pl
from jax.experimental.pallas import tpu as pltpu

F32 = jnp.float32
BF16 = jnp.bfloat16

D_MODEL = 1024
DEPTH = 2
GRID_W = 64
N_HEADS = 8
HEAD_W = 128
ATT_QK_DIM = 64
ROPE_BASE = 10000.0
ROPE_AXIS_DIM = ATT_QK_DIM // 2
PEER_N_KEYS = 128
PEER_TOPK = 16
LN_EPS = 1e-5
DEEPNORM_ALPHA = (2 * DEPTH) ** 0.25
W_IN_COLS = 14 * D_MODEL
COL_Q, COL_K, COL_V, COL_CB, COL_CC, COL_CX, COL_RQ, COL_RFF, COL_RFB, COL_RI, COL_RG, COL_G = range(12)

VMEM_LIMIT = 56 * 1024 * 1024
PROJ_ROWS = 2048
PROJ_COLS = 512
ATT_Q_ROWS = 1024
ATT_KEY_CHUNK = 512
MERGE_ROWS = 256
PEER_TOKENS = 512
REC_C = 128
REC_LEVELS = 7
REC_HEADS_PER_STEP = 8


def _cparams(sem):
    return pltpu.CompilerParams(dimension_semantics=sem, vmem_limit_bytes=VMEM_LIMIT)


def _dot(a, b):
    return jnp.dot(a, b, preferred_element_type=F32)


def _dot_nt(a, b):
    return lax.dot_general(a, b, (((1,), (1,)), ((), ())), preferred_element_type=F32)


def _sigmoid(x):
    e = jnp.exp(-jnp.abs(x))
    r = 1.0 / (1.0 + e)
    return jnp.where(x >= 0, r, e * r)


def _layer_norm(y, g, b):
    mu = jnp.mean(y, axis=-1, keepdims=True)
    yc = y - mu
    var = jnp.mean(yc * yc, axis=-1, keepdims=True)
    return yc * lax.rsqrt(var + LN_EPS) * g + b


def _mod_body(c_ref, w_ref, b_ref, o_ref):
    c = c_ref[...]
    s = (c * _sigmoid(c)).astype(BF16)
    o_ref[0] = _dot(s, w_ref[0].astype(BF16)) + b_ref[0]


def _mod_vectors(cmat, mod_w, mod_b):
    tn = 1536
    return pl.pallas_call(
        _mod_body,
        grid=(DEPTH, 6 * D_MODEL // tn),
        in_specs=[pl.BlockSpec((8, D_MODEL), lambda l, j: (0, 0)),
                  pl.BlockSpec((1, D_MODEL, tn), lambda l, j: (l, 0, j)),
                  pl.BlockSpec((1, 1, tn), lambda l, j: (l, 0, j))],
        out_specs=pl.BlockSpec((1, 8, tn), lambda l, j: (l, 0, j)),
        out_shape=jax.ShapeDtypeStruct((DEPTH, 8, 6 * D_MODEL), F32),
        compiler_params=_cparams(("parallel", "parallel")),
        name="mod_vectors",
    )(cmat, mod_w, mod_b.reshape(DEPTH, 1, 6 * D_MODEL))


def _in_proj_body(x_ref, mod_ref, w_ref, *refs, rope, tm):
    if rope:
        cos_ref, sin_ref, o_ref, h_ref = refs
    else:
        o_ref, h_ref = refs
    j = pl.program_id(2)

    @pl.when(j == 0)
    def _():
        h_ref[...] = (x_ref[0] * (1.0 + mod_ref[0, 1:2, :]) + mod_ref[0, 0:1, :]).astype(BF16)

    z = _dot(h_ref[...], w_ref[...])
    if not rope:
        o_ref[0] = z
        return
    rope_tiles = COL_V * D_MODEL // PROJ_COLS

    @pl.when(j < rope_tiles)
    def _():
        cos = cos_ref[...]
        sin = sin_ref[...]
        lane = lax.broadcasted_iota(jnp.int32, (tm, HEAD_W), 1)
        first = (lane % ROPE_AXIS_DIM) < (ROPE_AXIS_DIM // 2)
        for g in range(PROJ_COLS // HEAD_W):
            zg = z[:, g * HEAD_W:(g + 1) * HEAD_W]
            partner = jnp.where(first, pltpu.roll(zg, HEAD_W - ROPE_AXIS_DIM // 2, 1),
                                pltpu.roll(zg, ROPE_AXIS_DIM // 2, 1))
            o_ref[0, :, g * HEAD_W:(g + 1) * HEAD_W] = zg * cos + partner * sin

    @pl.when(j >= rope_tiles)
    def _():
        o_ref[0] = z


def _in_proj(x, mod, w_bf16, rope_tabs, tm):
    B, T, _ = x.shape
    per_b = mod.shape[0] > 1
    rope = rope_tabs is not None
    in_specs = [pl.BlockSpec((1, tm, D_MODEL), lambda b, i, j: (b, i, 0)),
                pl.BlockSpec((1, 6, D_MODEL), (lambda b, i, j: (b, 0, 0)) if per_b else (lambda b, i, j: (0, 0, 0))),
                pl.BlockSpec((D_MODEL, PROJ_COLS), lambda b, i, j: (0, j))]
    args = [x, mod, w_bf16]
    if rope:
        in_specs += [pl.BlockSpec((tm, HEAD_W), lambda b, i, j: (i, 0))] * 2
        args += list(rope_tabs)
    return pl.pallas_call(
        functools.partial(_in_proj_body, rope=rope, tm=tm),
        grid=(B, T // tm, W_IN_COLS // PROJ_COLS),
        in_specs=in_specs,
        out_specs=pl.BlockSpec((1, tm, PROJ_COLS), lambda b, i, j: (b, i, j)),
        out_shape=jax.ShapeDtypeStruct((B, T, W_IN_COLS), F32),
        scratch_shapes=[pltpu.VMEM((tm, D_MODEL), BF16)],
        compiler_params=_cparams(("parallel", "parallel", "arbitrary")),
        name="in_proj_rope" if rope else "in_proj",
    )(*args)


def _rope_tables(T):
    t = np.arange(T)
    pos = np.stack([t // GRID_W, t % GRID_W], axis=1).astype(np.float32)
    lane = np.arange(HEAD_W)
    axis = (lane % ATT_QK_DIM) // ROPE_AXIS_DIM
    r = lane % ROPE_AXIS_DIM
    half = ROPE_AXIS_DIM // 2
    freqs = ROPE_BASE ** (-jnp.arange(0, ROPE_AXIS_DIM, 2, dtype=F32) / ROPE_AXIS_DIM)
    ang = jnp.asarray(pos)[:, axis] * freqs[r % half][None, :]
    sign = jnp.asarray(np.where(r < half, -1.0, 1.0).astype(np.float32))[None, :]
    return jnp.cos(ang), jnp.sin(ang) * sign


def _attn_body(lam_ref, q_ref, k_ref, v_ref, *refs, tq, tk, n_self, n_ctx, out_scale):
    if n_ctx:
        kc_ref, vc_ref, g_ref, o_ref, kb_ref, vb_ref = refs
    else:
        g_ref, o_ref, kb_ref, vb_ref = refs
    t_self = n_self * tk

    @pl.when(pl.program_id(2) == 0)
    def _():
        kb_ref[0:t_self, :] = k_ref[0].astype(BF16)
        vb_ref[0:t_self, :] = v_ref[0].astype(BF16)
        if n_ctx:
            kb_ref[t_self:, :] = kc_ref[0, 0].astype(BF16)
            vb_ref[t_self:, :] = vc_ref[0, 0].astype(BF16)

    q = q_ref[0] * (ATT_QK_DIM ** -0.5 * math.log2(math.e))
    lane = lax.broadcasted_iota(jnp.int32, (tq, HEAD_W), 1)
    lo = lane < ATT_QK_DIM
    qs = jnp.concatenate([jnp.where(lo, q, 0.0), jnp.where(lo, 0.0, q)], axis=0).astype(BF16)

    m = jnp.full((2 * tq, 1), -jnp.inf, F32)
    l = jnp.zeros((2 * tq, 1), F32)
    acc = jnp.zeros((2 * tq, HEAD_W), F32)
    for c in range(n_self + n_ctx):
        s = _dot_nt(qs, kb_ref[c * tk:(c + 1) * tk, :])
        m_new = jnp.maximum(m, jnp.max(s, axis=-1, keepdims=True))
        a = jnp.exp2(m - m_new)
        p = jnp.exp2(s - m_new)
        l = a * l + jnp.sum(p, axis=-1, keepdims=True)
        acc = a * acc + _dot(p.astype(BF16), vb_ref[c * tk:(c + 1) * tk, :])
        m = m_new
    o = acc / l
    o = o[:tq] - lam_ref[0] * o[tq:]
    o = o * lax.rsqrt(jnp.mean(o * o, axis=-1, keepdims=True) + LN_EPS) * g_ref[...] * out_scale
    o_ref[0] = o.astype(BF16)


def _attention(z, lam, subln, ctx_kv, layer, out_scale, tq, tk):
    B, T, _ = z.shape
    n_self = T // tk
    in_specs = [pl.BlockSpec(memory_space=pltpu.SMEM),
                pl.BlockSpec((1, tq, HEAD_W), lambda b, h, i: (b, i, COL_Q * N_HEADS + h)),
                pl.BlockSpec((1, T, HEAD_W), lambda b, h, i: (b, 0, COL_K * N_HEADS + h)),
                pl.BlockSpec((1, T, HEAD_W), lambda b, h, i: (b, 0, COL_V * N_HEADS + h))]
    args = [lam, z, z, z]
    t_all = T
    n_ctx = 0
    if ctx_kv is not None:
        P = ctx_kv[0].shape[2]
        assert P == tk
        n_ctx = 1
        t_all = T + P
        in_specs += [pl.BlockSpec((1, 1, P, HEAD_W), lambda b, h, i: (b, layer, 0, h))] * 2
        args += list(ctx_kv)
    in_specs.append(pl.BlockSpec((1, HEAD_W), lambda b, h, i: (0, 0)))
    args.append(subln)
    return pl.pallas_call(
        functools.partial(_attn_body, tq=tq, tk=tk, n_self=n_self, n_ctx=n_ctx, out_scale=out_scale),
        grid=(B, N_HEADS, T // tq),
        in_specs=in_specs,
        out_specs=pl.BlockSpec((1, tq, HEAD_W), lambda b, h, i: (b, i, h)),
        out_shape=jax.ShapeDtypeStruct((B, T, D_MODEL), BF16),
        scratch_shapes=[pltpu.VMEM((t_all, HEAD_W), BF16), pltpu.VMEM((t_all, HEAD_W), BF16)],
        compiler_params=_cparams(("parallel", "parallel", "arbitrary")),
        name="attention_ctx" if n_ctx else "attention",
    )(*args)


def _rec_constants():
    C = REC_C
    t = np.arange(C)[:, None]
    j = np.arange(C)[None, :]
    blocks = [(j <= t), (j > t)]
    for l in range(REC_LEVELS):
        m = 1 << l
        seg0 = (t // m) * m
        odd = ((t // m) % 2) == 1
        blocks.append(np.where(odd, (j >= seg0) & (j <= t), (j > t) & (j <= seg0 + m - 1)))
    mf = np.concatenate(blocks, axis=0).astype(np.float32)
    x = t ^ j
    lv = np.where(x == 0, REC_LEVELS, np.floor(np.log2(np.maximum(x, 1))).astype(np.int64))
    lvf = np.where(j <= t, lv, REC_LEVELS + 1).astype(np.int32)
    mb = mf.reshape(-1, C, C)[:, ::-1, ::-1].reshape(-1, C)
    lvb = lvf[::-1, ::-1]
    mf, mb = np.concatenate([mf, mf], axis=1), np.concatenate([mb, mb], axis=1)
    return (jnp.asarray(mf, BF16), jnp.asarray(np.ascontiguousarray(mb), BF16),
            jnp.asarray(lvf), jnp.asarray(np.ascontiguousarray(lvb)))


def _rec_pair(rq, ri, fx, lb, m_ref, lv, sts, backward):
    C = REC_C
    e = jnp.exp(-jnp.abs(fx))
    r = 1.0 / (1.0 + e)
    pos = fx >= 0
    sig = jnp.where(pos, r, e * r)
    nsig = jnp.where(pos, e * r, r)
    logf = jnp.log(lb + (1.0 - lb) * sig)
    kk = (1.0 - lb) * nsig
    q = rq * _sigmoid(rq)
    hi = logf.astype(BF16)
    mid = (logf - hi.astype(F32)).astype(BF16)
    lf2 = jnp.concatenate([hi, mid], axis=0)

    def expo(blk):
        return _dot(m_ref[blk * C:(blk + 1) * C, :], lf2)

    row = lax.broadcasted_iota(jnp.int32, (C, 2 * HEAD_W), 0)
    if backward:
        row = (C - 1) - row
    qb = q.astype(BF16)
    kb = kk.astype(BF16)
    ws = []
    for l in range(REC_LEVELS):
        odd = ((row >> l) & 1) == 1
        ws.append((jnp.exp(expo(2 + l)) * jnp.where(odd, q, kk)).astype(BF16))
    b_incl = expo(0)
    qi = (q * jnp.exp(b_incl)).astype(BF16)
    ki = (kk * jnp.exp(expo(1))).astype(BF16)
    dec = jnp.exp(b_incl[0:1, :] if backward else b_incl[C - 1:C, :])
    outs, new_sts = [], []
    for g in range(2):
        ls = slice(g * HEAD_W, (g + 1) * HEAD_W)
        a = jnp.where(lv == REC_LEVELS, _dot_nt(qb[:, ls], kb[:, ls]), 0.0)
        for l in range(REC_LEVELS):
            a = jnp.where(lv == l, _dot_nt(ws[l][:, ls], ws[l][:, ls]), a)
        rig = ri[:, ls]
        outs.append(_dot(a.astype(BF16), rig.astype(BF16)) + _dot_nt(qi[:, ls], sts[g].astype(BF16)))
        new_sts.append(sts[g] * dec[:, ls] + _dot(rig.T.astype(BF16), ki[:, ls]))
    return jnp.concatenate(outs, axis=1), new_sts


def _hgrn_body(mf_ref, mb_ref, lvf_ref, lvb_ref, rqf_ref, rif_ref, ff_ref, rqb_ref, rib_ref, fb_ref, lb_ref,
               *refs, has_s0):
    if has_s0:
        s0_ref, of_ref, ob_ref, so_ref, sf_scr, sb_scr = refs
    else:
        of_ref, ob_ref, so_ref, sf_scr, sb_scr = refs
    c = pl.program_id(2)

    @pl.when(c == 0)
    def _():
        for g in range(REC_HEADS_PER_STEP):
            if has_s0:
                sf_scr[g] = s0_ref[0, 0, g].T
                sb_scr[g] = s0_ref[0, 1, g].T
            else:
                sf_scr[g] = jnp.zeros((HEAD_W, HEAD_W), F32)
                sb_scr[g] = jnp.zeros((HEAD_W, HEAD_W), F32)

    for g in range(0, REC_HEADS_PER_STEP, 2):
        ls = slice(g * HEAD_W, (g + 2) * HEAD_W)
        o, st = _rec_pair(rqf_ref[0, :, ls], rif_ref[0, :, ls], ff_ref[0, :, ls], lb_ref[0:1, ls], mf_ref,
                          lvf_ref[...], [sf_scr[g], sf_scr[g + 1]], False)
        of_ref[0, :, ls] = o
        sf_scr[g] = st[0]
        sf_scr[g + 1] = st[1]
        o, st = _rec_pair(rqb_ref[0, :, ls], rib_ref[0, :, ls], fb_ref[0, :, ls], lb_ref[1:2, ls], mb_ref,
                          lvb_ref[...], [sb_scr[g], sb_scr[g + 1]], True)
        ob_ref[0, :, ls] = o
        sb_scr[g] = st[0]
        sb_scr[g + 1] = st[1]

    @pl.when(c == pl.num_programs(2) - 1)
    def _():
        for g in range(REC_HEADS_PER_STEP):
            so_ref[0, 0, g] = sf_scr[g].T
            so_ref[0, 1, g] = sb_scr[g].T


def _hgrn(z, lb, s0, layer, consts):
    B, T, _ = z.shape
    n = T // REC_C
    C = REC_C
    G = REC_HEADS_PER_STEP
    ng = N_HEADS // G
    W = G * HEAD_W

    def fwd(col):
        return pl.BlockSpec((1, C, W), lambda b, h, c: (b, c, col * ng + h))

    def bwd(col):
        return pl.BlockSpec((1, C, W), lambda b, h, c: (b, n - 1 - c, col * ng + h))

    const2 = lambda b, h, c: (0, 0)
    in_specs = [pl.BlockSpec(((2 + REC_LEVELS) * C, 2 * C), const2), pl.BlockSpec(((2 + REC_LEVELS) * C, 2 * C), const2),
                pl.BlockSpec((C, C), const2), pl.BlockSpec((C, C), const2),
                fwd(COL_RQ), fwd(COL_RI), fwd(COL_RFF), bwd(COL_RQ), bwd(COL_RI), bwd(COL_RFB),
                pl.BlockSpec((2, W), lambda b, h, c: (0, h))]
    args = list(consts) + [z] * 6 + [lb]
    if s0 is not None:
        s0v = s0.reshape(B, DEPTH * 2, N_HEADS, HEAD_W, HEAD_W)
        in_specs.append(pl.BlockSpec((1, 2, G, HEAD_W, HEAD_W), lambda b, h, c: (b, layer, h, 0, 0)))
        args.append(s0v)
    return pl.pallas_call(
        functools.partial(_hgrn_body, has_s0=s0 is not None),
        grid=(B, ng, n),
        in_specs=in_specs,
        out_specs=[pl.BlockSpec((1, C, W), lambda b, h, c: (b, c, h)),
                   pl.BlockSpec((1, C, W), lambda b, h, c: (b, n - 1 - c, h)),
                   pl.BlockSpec((1, 2, G, HEAD_W, HEAD_W), lambda b, h, c: (b, 0, h, 0, 0))],
        out_shape=[jax.ShapeDtypeStruct((B, T, D_MODEL), F32), jax.ShapeDtypeStruct((B, T, D_MODEL), F32),
                   jax.ShapeDtypeStruct((B, 2, N_HEADS, HEAD_W, HEAD_W), F32)],
        scratch_shapes=[pltpu.VMEM((G, HEAD_W, HEAD_W), F32), pltpu.VMEM((G, HEAD_W, HEAD_W), F32)],
        compiler_params=_cparams(("parallel", "parallel", "arbitrary")),
        name="hgrn_ctx" if s0 is not None else "hgrn",
    )(*args)


def _merge_body(oatt_ref, zb_ref, zc_ref, zx_ref, zcp_ref, zxp_ref, zcn_ref, zxn_ref, of_ref, ob_ref, rg_ref,
                g0_ref, g1_ref, g2_ref, x_ref, mod_ref, wb_ref, wo_ref, cw_ref, cb_ref, rn_ref, ln_ref,
                x1_ref, h2_ref, *, tm):
    i = pl.program_id(1)
    u = zc_ref[0] * zx_ref[0]
    row = lax.broadcasted_iota(jnp.int32, (tm, D_MODEL), 0)
    prev_ok = (i > 0).astype(F32)
    next_ok = (i < pl.num_programs(1) - 1).astype(F32)
    u_prev_edge = zcp_ref[0, 7:8, :] * zxp_ref[0, 7:8, :] * prev_ok
    u_next_edge = zcn_ref[0, 0:1, :] * zxn_ref[0, 0:1, :] * next_ok
    up = jnp.where(row == 0, u_prev_edge, pltpu.roll(u, 1, 0))
    un = jnp.where(row == tm - 1, u_next_edge, pltpu.roll(u, tm - 1, 0))
    conv = up * cw_ref[0:1, :] + u * cw_ref[1:2, :] + un * cw_ref[2:3, :] + cb_ref[...]
    o_conv = (zb_ref[0] * conv).astype(BF16)
    s = of_ref[0] + ob_ref[0]
    rg = rg_ref[0]
    parts = []
    for h in range(N_HEADS):
        sh = s[:, h * HEAD_W:(h + 1) * HEAD_W]
        parts.append(sh * lax.rsqrt(jnp.mean(sh * sh, axis=-1, keepdims=True) + LN_EPS))
    o_rec = (jnp.concatenate(parts, axis=1) * rn_ref[...] * (rg * _sigmoid(rg))).astype(BF16)
    merged = (_sigmoid(g0_ref[0]) * _dot(oatt_ref[0], wb_ref[0])
              + _sigmoid(g1_ref[0]) * _dot(o_conv, wb_ref[1])
              + _sigmoid(g2_ref[0]) * _dot(o_rec, wb_ref[2]))
    mix = _dot(merged.astype(BF16), wo_ref[...])
    y = DEEPNORM_ALPHA * x_ref[0] + mod_ref[0, 2:3, :] * mix
    x1 = _layer_norm(y, ln_ref[0:1, :], ln_ref[1:2, :])
    x1_ref[0] = x1
    h2_ref[0] = (x1 * (1.0 + mod_ref[0, 4:5, :]) + mod_ref[0, 3:4, :]).astype(BF16)


def _merge(x, z, oatt, o_f, o_b, mod, wb_bf16, wo_bf16, conv_w, conv_b, rec_norm_t, ln_gb, tm):
    B, T, _ = x.shape
    per_b = mod.shape[0] > 1
    nb8 = tm // 8
    last8 = T // 8 - 1

    def col(c):
        return pl.BlockSpec((1, tm, D_MODEL), lambda b, i: (b, i, c))

    def prev8(c):
        return pl.BlockSpec((1, 8, D_MODEL), lambda b, i: (b, jnp.maximum(i * nb8 - 1, 0), c))

    def next8(c):
        return pl.BlockSpec((1, 8, D_MODEL), lambda b, i: (b, jnp.minimum((i + 1) * nb8, last8), c))

    tile = pl.BlockSpec((1, tm, D_MODEL), lambda b, i: (b, i, 0))
    full2 = lambda b, i: (0, 0)
    in_specs = [tile, col(COL_CB), col(COL_CC), col(COL_CX), prev8(COL_CC), prev8(COL_CX), next8(COL_CC), next8(COL_CX),
                tile, tile, col(COL_RG), col(COL_G), col(COL_G + 1), col(COL_G + 2), tile,
                pl.BlockSpec((1, 6, D_MODEL), (lambda b, i: (b, 0, 0)) if per_b else (lambda b, i: (0, 0, 0))),
                pl.BlockSpec((3, D_MODEL, D_MODEL), lambda b, i: (0, 0, 0)),
                pl.BlockSpec((D_MODEL, D_MODEL), full2),
                pl.BlockSpec((3, D_MODEL), full2), pl.BlockSpec((1, D_MODEL), full2),
                pl.BlockSpec((1, D_MODEL), full2), pl.BlockSpec((2, D_MODEL), full2)]
    return pl.pallas_call(
        functools.partial(_merge_body, tm=tm),
        grid=(B, T // tm),
        in_specs=in_specs,
        out_specs=[tile, tile],
        out_shape=[jax.ShapeDtypeStruct((B, T, D_MODEL), F32), jax.ShapeDtypeStruct((B, T, D_MODEL), BF16)],
        compiler_params=_cparams(("parallel", "arbitrary")),
        name="merge",
    )(oatt, z, z, z, z, z, z, z, o_f, o_b, z, z, z, z, x, mod, wb_bf16, wo_bf16, conv_w, conv_b, rec_norm_t, ln_gb)


def _top_rows(s, k, one_per_round, want_rank=True):
    n, w = s.shape
    s_in = s
    track = one_per_round or want_rank
    rid = lax.broadcasted_iota(jnp.int32, (n, w), 0).astype(F32)
    rank = jnp.full((n, w), float(k), F32) if track else None
    vals = []
    for r in range(k):
        m = jnp.max(s, axis=0, keepdims=True)
        hit = s == m
        if one_per_round:
            first = jnp.min(jnp.where(hit, rid, float(n)), axis=0, keepdims=True)
            hit = rid == first
        s = jnp.where(hit, -jnp.inf, s)
        if track:
            rank = jnp.where(hit, float(r), rank)
        vals.append(m)
    taken = (rank < k) if track else (s_in >= vals[-1])
    excess = jnp.sum(jnp.where(taken, 1.0, 0.0), axis=0, keepdims=True) - k
    return vals, rank, excess


def _route_body(h_ref, wq_ref, keys_ref, n1_ref, e1_ref, r2_ref, e2_ref, q_scr, s_scr, *, tt):
    q_scr[...] = _dot(h_ref[0], wq_ref[...]).astype(BF16)

    def head(h, carry):
        for p in range(2):
            off = pl.multiple_of((2 * h + p) * HEAD_W, HEAD_W)
            s_scr[p] = _dot_nt(keys_ref[2 * h + p], q_scr[:, pl.ds(off, HEAD_W)])
        excess = route_head(h, False)

        @pl.when(jnp.max(excess) > 0.0)
        def _():
            route_head(h, True)

        return carry

    def route_head(h, one_per_round):
        excess = jnp.zeros((1, HEAD_W), F32)
        for lc in range(tt // HEAD_W):
            ls = slice(lc * HEAD_W, (lc + 1) * HEAD_W)
            s1 = s_scr[0, :, ls]
            s2 = s_scr[1, :, ls]
            v1, r1, x1 = _top_rows(s1, PEER_TOPK, one_per_round, want_rank=False)
            v2, r2, x2 = _top_rows(s2, PEER_TOPK, one_per_round)
            v2a = jnp.concatenate(v2, axis=0)
            cands = [v1[0] + v2a] + [v1[a] + v2a[:8] for a in range(1, 8)] + [jnp.concatenate(v1[8:], axis=0) + v2[0]]
            cand = jnp.concatenate(cands, axis=0)
            vc, rc, xc = _top_rows(cand, PEER_TOPK, one_per_round, want_rank=False)
            excess = jnp.maximum(excess, jnp.maximum(jnp.maximum(x1, x2), xc))
            sel = (rc < PEER_TOPK) if one_per_round else (cand >= vc[-1])
            zsum = jnp.sum(jnp.where(sel, jnp.exp(cand - (v1[0] + v2[0])), 0.0), axis=0, keepdims=True)
            selc = jnp.where(sel, 1.0, 0.0)
            n_a = [jnp.sum(selc[0:16], axis=0, keepdims=True)]
            n_a += [jnp.sum(selc[8 + 8 * a:16 + 8 * a], axis=0, keepdims=True) for a in range(1, 8)]
            n_a += [selc[72 + a:73 + a] for a in range(8)]
            n1 = jnp.zeros_like(s1)
            for a in range(PEER_TOPK):
                n1 = jnp.where((r1 == a) if one_per_round else (s1 == v1[a]), n_a[a], n1)
            n1_ref[h, :, ls] = n1
            e1_ref[h, :, ls] = jnp.exp(s1 - v1[0]) * (0.5 / zsum)
            r2_ref[h, :, ls] = r2.astype(BF16)
            e2_ref[h, :, ls] = jnp.exp(s2 - v2[0]).astype(BF16)
        return excess

    lax.fori_loop(0, N_HEADS, head, 0)


def _peer_route(h2, wq_bf16, keys_bf16, tt):
    B, T, _ = h2.shape
    big = pl.BlockSpec((None, N_HEADS, PEER_N_KEYS, tt), lambda b, i: (b, 0, 0, i))
    f32_shape = jax.ShapeDtypeStruct((B, N_HEADS, PEER_N_KEYS, T), F32)
    bf16_shape = jax.ShapeDtypeStruct((B, N_HEADS, PEER_N_KEYS, T), BF16)
    return pl.pallas_call(
        functools.partial(_route_body, tt=tt),
        grid=(B, T // tt),
        in_specs=[pl.BlockSpec((1, tt, D_MODEL), lambda b, i: (b, i, 0)),
                  pl.BlockSpec((D_MODEL, 2 * N_HEADS * HEAD_W), lambda b, i: (0, 0)),
                  pl.BlockSpec((2 * N_HEADS, PEER_N_KEYS, HEAD_W), lambda b, i: (0, 0, 0))],
        out_specs=[big, big, big, big],
        out_shape=[f32_shape, f32_shape, bf16_shape, bf16_shape],
        scratch_shapes=[pltpu.VMEM((tt, 2 * N_HEADS * HEAD_W), BF16), pltpu.VMEM((2, PEER_N_KEYS, tt), F32)],
        compiler_params=_cparams(("parallel", "parallel")),
        name="peer_route",
    )(h2, wq_bf16, keys_bf16)


PEER_STEP_KEYS = 16
PEER_SUB_KEYS = 4


def _dense_body(h_ref, u_ref, vt_ref, n1_ref, e1_ref, r2_ref, e2_ref, x_ref, mod_ref, ln_ref, o_ref,
                act_scr, g_scr, w_scr, acc_scr, r2_scr, e2_scr, *, tt):
    k = pl.program_id(2)

    @pl.when(k == 0)
    def _():
        acc_scr[...] = jnp.zeros_like(acc_scr)
        r2_scr[...] = r2_ref[...]
        e2_scr[...] = e2_ref[...]

    hb = h_ref[0]
    sub = PEER_SUB_KEYS * PEER_N_KEYS

    def routing_weights(jj):
        for j in range(PEER_SUB_KEYS * jj, PEER_SUB_KEYS * (jj + 1)):
            for lc in range(tt // HEAD_W):
                ls = slice(lc * HEAD_W, (lc + 1) * HEAD_W)
                g = None
                for h in range(N_HEADS):
                    n_t = jnp.broadcast_to(n1_ref[h, j:j + 1, ls], (16, HEAD_W)).astype(BF16)[None]
                    e_t = jnp.broadcast_to(e1_ref[h, j:j + 1, ls], (16, HEAD_W)).astype(BF16)[None]
                    hit = r2_scr[h, :, ls].reshape(PEER_N_KEYS // 16, 16, HEAD_W) < n_t
                    term = jnp.where(hit, e2_scr[h, :, ls].reshape(PEER_N_KEYS // 16, 16, HEAD_W), 0.0) * e_t
                    g = term if g is None else g + term
                g_scr[j * PEER_N_KEYS:(j + 1) * PEER_N_KEYS, ls] = g.reshape(PEER_N_KEYS, HEAD_W)
        return lax.shift_right_logical(pltpu.bitcast(g[0], jnp.uint32), jnp.uint32(32))

    def after(x, zero_tile):
        xi = pltpu.bitcast(x, jnp.uint32)
        z = jnp.tile(zero_tile, (xi.shape[0] // zero_tile.shape[0], xi.shape[1] // zero_tile.shape[1]))
        return pltpu.bitcast(xi | z, x.dtype)

    nsub = PEER_STEP_KEYS // PEER_SUB_KEYS
    for jj in range(nsub):
        rows = slice(jj * sub, (jj + 1) * sub)
        zero_tile = routing_weights(jj)
        act_scr[rows, :] = _dot_nt(after(u_ref[rows, :], zero_tile), hb)
    for jj in range(nsub):
        rows = slice(jj * sub, (jj + 1) * sub)
        a = act_scr[rows, :]
        gelu2 = a * (1.0 + lax.erf(a * (2.0 ** -0.5)))
        w_scr[rows, :] = (gelu2 * g_scr[rows, :].astype(F32)).astype(BF16)
        acc_scr[...] += _dot(vt_ref[0, :, rows], w_scr[rows, :])

    @pl.when(k == pl.num_programs(2) - 1)
    def _():
        y = DEEPNORM_ALPHA * x_ref[0] + mod_ref[0, 5:6, :] * acc_scr[...].T
        o_ref[0] = _layer_norm(y, ln_ref[0:1, :], ln_ref[1:2, :])


def _peer_dense(h2, x1, mod, u_bf16, vt_bf16, route, ln_gb, tt):
    B, T, _ = h2.shape
    n1, e1, r2, e2 = route
    per_b = mod.shape[0] > 1
    ne = PEER_STEP_KEYS * PEER_N_KEYS
    rows = pl.BlockSpec((None, N_HEADS, PEER_STEP_KEYS, tt), lambda b, i, k: (b, 0, k, i))
    full = pl.BlockSpec((None, N_HEADS, PEER_N_KEYS, tt), lambda b, i, k: (b, 0, 0, i))
    return pl.pallas_call(
        functools.partial(_dense_body, tt=tt),
        grid=(B, T // tt, PEER_N_KEYS // PEER_STEP_KEYS),
        in_specs=[pl.BlockSpec((1, tt, D_MODEL), lambda b, i, k: (b, i, 0)),
                  pl.BlockSpec((ne, D_MODEL), lambda b, i, k: (k, 0)),
                  pl.BlockSpec((1, D_MODEL, ne), lambda b, i, k: (k, 0, 0)),
                  rows, rows, full, full,
                  pl.BlockSpec((1, tt, D_MODEL), lambda b, i, k: (b, i, 0)),
                  pl.BlockSpec((1, 6, D_MODEL), (lambda b, i, k: (b, 0, 0)) if per_b else (lambda b, i, k: (0, 0, 0))),
                  pl.BlockSpec((2, D_MODEL), lambda b, i, k: (0, 0))],
        out_specs=pl.BlockSpec((1, tt, D_MODEL), lambda b, i, k: (b, i, 0)),
        out_shape=jax.ShapeDtypeStruct((B, T, D_MODEL), F32),
        scratch_shapes=[pltpu.VMEM((ne, tt), F32), pltpu.VMEM((ne, tt), BF16), pltpu.VMEM((ne, tt), BF16),
                        pltpu.VMEM((D_MODEL, tt), F32),
                        pltpu.VMEM((N_HEADS, PEER_N_KEYS, tt), BF16), pltpu.VMEM((N_HEADS, PEER_N_KEYS, tt), BF16)],
        compiler_params=_cparams(("parallel", "parallel", "arbitrary")),
        name="peer_dense",
    )(h2, u_bf16, vt_bf16, n1, e1, r2, e2, x1, mod, ln_gb)


def _trunk_layer(x, mod, layer, p, ctx, rope_tabs, consts, flat_rows):
    B, T, _ = x.shape
    if flat_rows is not None:
        z = _in_proj(x.reshape(-1, flat_rows, D_MODEL), mod, p["w_in"], None, tm=flat_rows)
        z = z.reshape(B, T, W_IN_COLS)
    else:
        z = _in_proj(x, mod, p["w_in"], rope_tabs, tm=PROJ_ROWS)
    if ctx is None:
        oatt = _attention(z, p["lam"], p["subln"], None, layer, p["att_scale"], tq=min(ATT_Q_ROWS, T), tk=T)
        o_f, o_b, s_fin = _hgrn(z, p["lb"], None, layer, consts)
    else:
        oatt = _attention(z, p["lam"], p["subln"], (ctx[0], ctx[1]), layer, p["att_scale"], tq=ATT_Q_ROWS,
                          tk=ATT_KEY_CHUNK)
        o_f, o_b, s_fin = _hgrn(z, p["lb"], ctx[2], layer, consts)
    x1, h2 = _merge(x, z, oatt, o_f, o_b, mod, p["w_branch"], p["w_out"], p["conv_w"], p["conv_b"],
                    p["rec_norm"], p["ln0"], tm=MERGE_ROWS)
    if flat_rows is not None:
        h2r, x1r = h2.reshape(-1, flat_rows, D_MODEL), x1.reshape(-1, flat_rows, D_MODEL)
    else:
        h2r, x1r = h2, x1
    route = _peer_route(h2r, p["peer_wq"], p["peer_keys"], PEER_TOKENS)
    x2 = _peer_dense(h2r, x1r, mod, p["peer_u"], p["peer_vt"], route, p["ln1"], PEER_TOKENS).reshape(B, T, D_MODEL)
    return x2, z, s_fin


def _layer_params(l, lb_all, w_in, attn_lambda, attn_subln, conv_w, conv_b, rec_norm, w_branch, w_out, ln_g, ln_b,
                  peer_wq, peer_keys, peer_u, peer_v):
    lam_init = 0.8 - 0.6 * math.exp(-0.3 * l)
    lp = attn_lambda[l].astype(F32)
    lam = jnp.exp(jnp.sum(lp[0] * lp[1])) - jnp.exp(jnp.sum(lp[2] * lp[3])) + lam_init
    return {
        "w_in": w_in[l].astype(BF16), "lam": lam.reshape(1), "att_scale": 1.0 - lam_init,
        "subln": attn_subln[l].reshape(1, HEAD_W), "lb": lb_all[l],
        "w_branch": w_branch[l].astype(BF16), "w_out": w_out[l].astype(BF16),
        "conv_w": conv_w[l], "conv_b": conv_b[l].reshape(1, D_MODEL),
        "rec_norm": jnp.tile(rec_norm[l], N_HEADS).reshape(1, D_MODEL),
        "ln0": jnp.stack([ln_g[l, 0], ln_b[l, 0]]), "ln1": jnp.stack([ln_g[l, 1], ln_b[l, 1]]),
        "peer_wq": peer_wq[l].astype(BF16),
        "peer_keys": peer_keys[l].astype(BF16).reshape(2 * N_HEADS, PEER_N_KEYS, HEAD_W),
        "peer_u": peer_u[l].astype(BF16),
        "peer_vt": peer_v[l].astype(BF16).reshape(-1, PEER_STEP_KEYS * PEER_N_KEYS, D_MODEL).transpose(0, 2, 1),
    }


def kernel(x_prompt, x_sample, c, cache_attn_k, cache_attn_v, state_hgrn, c_ctx, mod_w, mod_b, w_in, attn_lambda,
           attn_subln, conv_w, conv_b, rec_lb, rec_norm, w_branch, w_out, ln_g, ln_b, peer_wq, peer_keys, peer_u,
           peer_v):
    B, T, _ = x_prompt.shape
    Bs, Ts, _ = x_sample.shape
    P = cache_attn_k.shape[2]
    lb_all = jnp.cumsum(jax.nn.softmax(rec_lb.astype(F32), axis=0), axis=0)
    lb_all = lb_all - lb_all[:1]
    cmat = jnp.concatenate([c_ctx[None, :], c, jnp.zeros((8 - 1 - Bs, D_MODEL), F32)], axis=0)
    mods = _mod_vectors(cmat, mod_w, mod_b).reshape(DEPTH, 8, 6, D_MODEL)
    rope_tabs = _rope_tables(Ts)
    consts = _rec_constants()
    ck = cache_attn_k.reshape(Bs, DEPTH, P, N_HEADS * HEAD_W)
    cv = cache_attn_v.reshape(Bs, DEPTH, P, N_HEADS * HEAD_W)

    y_p, y_s = x_prompt, x_sample
    ks, vs, ss = [], [], []
    for l in range(DEPTH):
        p = _layer_params(l, lb_all, w_in, attn_lambda, attn_subln, conv_w, conv_b, rec_norm, w_branch, w_out,
                          ln_g, ln_b, peer_wq, peer_keys, peer_u, peer_v)
        y_p, z_p, s_p = _trunk_layer(y_p, mods[l, 0:1], l, p, None, None, consts, flat_rows=PROJ_ROWS)
        ks.append(z_p[..., COL_K * D_MODEL:(COL_K + 1) * D_MODEL].reshape(B, T, N_HEADS, HEAD_W))
        vs.append(z_p[..., COL_V * D_MODEL:(COL_V + 1) * D_MODEL].reshape(B, T, N_HEADS, HEAD_W))
        ss.append(s_p)
        y_s, _, _ = _trunk_layer(y_s, mods[l, 1:1 + Bs], l, p, (ck, cv, state_hgrn), rope_tabs, consts, flat_rows=None)
    return (y_p, y_s, jnp.stack(ks, axis=1), jnp.stack(vs, axis=1), jnp.stack(ss, axis=1))
```

```python
import functools
import math

import numpy as np
import jax
import jax.numpy as jnp
from jax import lax
from jax.experimental import pallas as pl
from jax.experimental.pallas import tpu as pltpu

F32 = jnp.float32
BF16 = jnp.bfloat16

D_MODEL = 1024
DEPTH = 2
GRID_W = 64
N_HEADS = 8
HEAD_W = 128
ATT_QK_DIM = 64
ROPE_BASE = 10000.0
ROPE_AXIS_DIM = ATT_QK_DIM // 2
PEER_N_KEYS = 128
PEER_TOPK = 16
LN_EPS = 1e-5
DEEPNORM_ALPHA = (2 * DEPTH) ** 0.25
W_IN_COLS = 14 * D_MODEL
COL_Q, COL_K, COL_V, COL_CB, COL_CC, COL_CX, COL_RQ, COL_RFF, COL_RFB, COL_RI, COL_RG, COL_G = range(12)

VMEM_LIMIT = 56 * 1024 * 1024
PROJ_ROWS = 2048
PROJ_COLS = 512
ATT_Q_ROWS = 1024
ATT_KEY_CHUNK = 512
MERGE_ROWS = 256
PEER_TOKENS = 512
REC_C = 128
REC_LEVELS = 7
REC_HEADS_PER_STEP = 8


def _cparams(sem):
    return pltpu.CompilerParams(dimension_semantics=sem, vmem_limit_bytes=VMEM_LIMIT)


def _dot(a, b):
    return jnp.dot(a, b, preferred_element_type=F32)


def _dot_nt(a, b):
    return lax.dot_general(a, b, (((1,), (1,)), ((), ())), preferred_element_type=F32)


def _sigmoid(x):
    e = jnp.exp(-jnp.abs(x))
    r = 1.0 / (1.0 + e)
    return jnp.where(x >= 0, r, e * r)


def _layer_norm(y, g, b):
    mu = jnp.mean(y, axis=-1, keepdims=True)
    yc = y - mu
    var = jnp.mean(yc * yc, axis=-1, keepdims=True)
    return yc * lax.rsqrt(var + LN_EPS) * g + b


def _mod_body(c_ref, w_ref, b_ref, o_ref):
    c = c_ref[...]
    s = (c * _sigmoid(c)).astype(BF16)
    o_ref[0] = _dot(s, w_ref[0].astype(BF16)) + b_ref[0]


def _mod_vectors(cmat, mod_w, mod_b):
    tn = 1536
    return pl.pallas_call(
        _mod_body,
        grid=(DEPTH, 6 * D_MODEL // tn),
        in_specs=[pl.BlockSpec((8, D_MODEL), lambda l, j: (0, 0)),
                  pl.BlockSpec((1, D_MODEL, tn), lambda l, j: (l, 0, j)),
                  pl.BlockSpec((1, 1, tn), lambda l, j: (l, 0, j))],
        out_specs=pl.BlockSpec((1, 8, tn), lambda l, j: (l, 0, j)),
        out_shape=jax.ShapeDtypeStruct((DEPTH, 8, 6 * D_MODEL), F32),
        compiler_params=_cparams(("parallel", "parallel")),
        name="mod_vectors",
    )(cmat, mod_w, mod_b.reshape(DEPTH, 1, 6 * D_MODEL))


def _in_proj_body(x_ref, mod_ref, w_ref, *refs, rope, tm):
    if rope:
        cos_ref, sin_ref, o_ref, h_ref = refs
    else:
        o_ref, h_ref = refs
    j = pl.program_id(2)

    @pl.when(j == 0)
    def _():
        h_ref[...] = (x_ref[0] * (1.0 + mod_ref[0, 1:2, :]) + mod_ref[0, 0:1, :]).astype(BF16)

    z = _dot(h_ref[...], w_ref[...])
    if not rope:
        o_ref[0] = z
        return
    rope_tiles = COL_V * D_MODEL // PROJ_COLS

    @pl.when(j < rope_tiles)
    def _():
        cos = cos_ref[...]
        sin = sin_ref[...]
        lane = lax.broadcasted_iota(jnp.int32, (tm, HEAD_W), 1)
        first = (lane % ROPE_AXIS_DIM) < (ROPE_AXIS_DIM // 2)
        for g in range(PROJ_COLS // HEAD_W):
            zg = z[:, g * HEAD_W:(g + 1) * HEAD_W]
            partner = jnp.where(first, pltpu.roll(zg, HEAD_W - ROPE_AXIS_DIM // 2, 1),
                                pltpu.roll(zg, ROPE_AXIS_DIM // 2, 1))
            o_ref[0, :, g * HEAD_W:(g + 1) * HEAD_W] = zg * cos + partner * sin

    @pl.when(j >= rope_tiles)
    def _():
        o_ref[0] = z


def _in_proj(x, mod, w_bf16, rope_tabs, tm):
    B, T, _ = x.shape
    per_b = mod.shape[0] > 1
    rope = rope_tabs is not None
    in_specs = [pl.BlockSpec((1, tm, D_MODEL), lambda b, i, j: (b, i, 0)),
                pl.BlockSpec((1, 6, D_MODEL), (lambda b, i, j: (b, 0, 0)) if per_b else (lambda b, i, j: (0, 0, 0))),
                pl.BlockSpec((D_MODEL, PROJ_COLS), lambda b, i, j: (0, j))]
    args = [x, mod, w_bf16]
    if rope:
        in_specs += [pl.BlockSpec((tm, HEAD_W), lambda b, i, j: (i, 0))] * 2
        args += list(rope_tabs)
    return pl.pallas_call(
        functools.partial(_in_proj_body, rope=rope, tm=tm),
        grid=(B, T // tm, W_IN_COLS // PROJ_COLS),
        in_specs=in_specs,
        out_specs=pl.BlockSpec((1, tm, PROJ_COLS), lambda b, i, j: (b, i, j)),
        out_shape=jax.ShapeDtypeStruct((B, T, W_IN_COLS), F32),
        scratch_shapes=[pltpu.VMEM((tm, D_MODEL), BF16)],
        compiler_params=_cparams(("parallel", "parallel", "arbitrary")),
        name="in_proj_rope" if rope else "in_proj",
    )(*args)


def _rope_tables(T):
    t = np.arange(T)
    pos = np.stack([t // GRID_W, t % GRID_W], axis=1).astype(np.float32)
    lane = np.arange(HEAD_W)
    axis = (lane % ATT_QK_DIM) // ROPE_AXIS_DIM
    r = lane % ROPE_AXIS_DIM
    half = ROPE_AXIS_DIM // 2
    freqs = ROPE_BASE ** (-jnp.arange(0, ROPE_AXIS_DIM, 2, dtype=F32) / ROPE_AXIS_DIM)
    ang = jnp.asarray(pos)[:, axis] * freqs[r % half][None, :]
    sign = jnp.asarray(np.where(r < half, -1.0, 1.0).astype(np.float32))[None, :]
    return jnp.cos(ang), jnp.sin(ang) * sign


def _attn_body(lam_ref, q_ref, k_ref, v_ref, *refs, tq, tk, n_self, n_ctx, out_scale):
    if n_ctx:
        kc_ref, vc_ref, g_ref, o_ref, kb_ref, vb_ref = refs
    else:
        g_ref, o_ref, kb_ref, vb_ref = refs
    t_self = n_self * tk

    @pl.when(pl.program_id(2) == 0)
    def _():
        kb_ref[0:t_self, :] = k_ref[0].astype(BF16)
        vb_ref[0:t_self, :] = v_ref[0].astype(BF16)
        if n_ctx:
            kb_ref[t_self:, :] = kc_ref[0, 0].astype(BF16)
            vb_ref[t_self:, :] = vc_ref[0, 0].astype(BF16)

    q = q_ref[0] * (ATT_QK_DIM ** -0.5 * math.log2(math.e))
    lane = lax.broadcasted_iota(jnp.int32, (tq, HEAD_W), 1)
    lo = lane < ATT_QK_DIM
    qs = jnp.concatenate([jnp.where(lo, q, 0.0), jnp.where(lo, 0.0, q)], axis=0).astype(BF16)

    m = jnp.full((2 * tq, 1), -jnp.inf, F32)
    l = jnp.zeros((2 * tq, 1), F32)
    acc = jnp.zeros((2 * tq, HEAD_W), F32)
    for c in range(n_self + n_ctx):
        s = _dot_nt(qs, kb_ref[c * tk:(c + 1) * tk, :])
        m_new = jnp.maximum(m, jnp.max(s, axis=-1, keepdims=True))
        a = jnp.exp2(m - m_new)
        p = jnp.exp2(s - m_new)
        l = a * l + jnp.sum(p, axis=-1, keepdims=True)
        acc = a * acc + _dot(p.astype(BF16), vb_ref[c * tk:(c + 1) * tk, :])
        m = m_new
    o = acc / l
    o = o[:tq] - lam_ref[0] * o[tq:]
    o = o * lax.rsqrt(jnp.mean(o * o, axis=-1, keepdims=True) + LN_EPS) * g_ref[...] * out_scale
    o_ref[0] = o.astype(BF16)


def _attention(z, lam, subln, ctx_kv, layer, out_scale, tq, tk):
    B, T, _ = z.shape
    n_self = T // tk
    in_specs = [pl.BlockSpec(memory_space=pltpu.SMEM),
                pl.BlockSpec((1, tq, HEAD_W), lambda b, h, i: (b, i, COL_Q * N_HEADS + h)),
                pl.BlockSpec((1, T, HEAD_W), lambda b, h, i: (b, 0, COL_K * N_HEADS + h)),
                pl.BlockSpec((1, T, HEAD_W), lambda b, h, i: (b, 0, COL_V * N_HEADS + h))]
    args = [lam, z, z, z]
    t_all = T
    n_ctx = 0
    if ctx_kv is not None:
        P = ctx_kv[0].shape[2]
        assert P == tk
        n_ctx = 1
        t_all = T + P
        in_specs += [pl.BlockSpec((1, 1, P, HEAD_W), lambda b, h, i: (b, layer, 0, h))] * 2
        args += list(ctx_kv)
    in_specs.append(pl.BlockSpec((1, HEAD_W), lambda b, h, i: (0, 0)))
    args.append(subln)
    return pl.pallas_call(
        functools.partial(_attn_body, tq=tq, tk=tk, n_self=n_self, n_ctx=n_ctx, out_scale=out_scale),
        grid=(B, N_HEADS, T // tq),
        in_specs=in_specs,
        out_specs=pl.BlockSpec((1, tq, HEAD_W), lambda b, h, i: (b, i, h)),
        out_shape=jax.ShapeDtypeStruct((B, T, D_MODEL), BF16),
        scratch_shapes=[pltpu.VMEM((t_all, HEAD_W), BF16), pltpu.VMEM((t_all, HEAD_W), BF16)],
        compiler_params=_cparams(("parallel", "parallel", "arbitrary")),
        name="attention_ctx" if n_ctx else "attention",
    )(*args)


def _rec_constants():
    C = REC_C
    t = np.arange(C)[:, None]
    j = np.arange(C)[None, :]
    blocks = [(j <= t), (j > t)]
    for l in range(REC_LEVELS):
        m = 1 << l
        seg0 = (t // m) * m
        odd = ((t // m) % 2) == 1
        blocks.append(np.where(odd, (j >= seg0) & (j <= t), (j > t) & (j <= seg0 + m - 1)))
    mf = np.concatenate(blocks, axis=0).astype(np.float32)
    x = t ^ j
    lv = np.where(x == 0, REC_LEVELS, np.floor(np.log2(np.maximum(x, 1))).astype(np.int64))
    lvf = np.where(j <= t, lv, REC_LEVELS + 1).astype(np.int32)
    mb = mf.reshape(-1, C, C)[:, ::-1, ::-1].reshape(-1, C)
    lvb = lvf[::-1, ::-1]
    mf, mb = np.concatenate([mf, mf], axis=1), np.concatenate([mb, mb], axis=1)
    return (jnp.asarray(mf, BF16), jnp.asarray(np.ascontiguousarray(mb), BF16),
            jnp.asarray(lvf), jnp.asarray(np.ascontiguousarray(lvb)))


def _rec_pair(rq, ri, fx, lb, m_ref, lv, sts, backward):
    C = REC_C
    e = jnp.exp(-jnp.abs(fx))
    r = 1.0 / (1.0 + e)
    pos = fx >= 0
    sig = jnp.where(pos, r, e * r)
    nsig = jnp.where(pos, e * r, r)
    logf = jnp.log(lb + (1.0 - lb) * sig)
    kk = (1.0 - lb) * nsig
    q = rq * _sigmoid(rq)
    hi = logf.astype(BF16)
    mid = (logf - hi.astype(F32)).astype(BF16)
    lf2 = jnp.concatenate([hi, mid], axis=0)

    def expo(blk):
        return _dot(m_ref[blk * C:(blk + 1) * C, :], lf2)

    row = lax.broadcasted_iota(jnp.int32, (C, 2 * HEAD_W), 0)
    if backward:
        row = (C - 1) - row
    qb = q.astype(BF16)
    kb = kk.astype(BF16)
    ws = []
    for l in range(REC_LEVELS):
        odd = ((row >> l) & 1) == 1
        ws.append((jnp.exp(expo(2 + l)) * jnp.where(odd, q, kk)).astype(BF16))
    b_incl = expo(0)
    qi = (q * jnp.exp(b_incl)).astype(BF16)
    ki = (kk * jnp.exp(expo(1))).astype(BF16)
    dec = jnp.exp(b_incl[0:1, :] if backward else b_incl[C - 1:C, :])
    outs, new_sts = [], []
    for g in range(2):
        ls = slice(g * HEAD_W, (g + 1) * HEAD_W)
        a = jnp.where(lv == REC_LEVELS, _dot_nt(qb[:, ls], kb[:, ls]), 0.0)
        for l in range(REC_LEVELS):
            a = jnp.where(lv == l, _dot_nt(ws[l][:, ls], ws[l][:, ls]), a)
        rig = ri[:, ls]
        outs.append(_dot(a.astype(BF16), rig.astype(BF16)) + _dot_nt(qi[:, ls], sts[g].astype(BF16)))
        new_sts.append(sts[g] * dec[:, ls] + _dot(rig.T.astype(BF16), ki[:, ls]))
    return jnp.concatenate(outs, axis=1), new_sts


def _hgrn_body(mf_ref, mb_ref, lvf_ref, lvb_ref, rqf_ref, rif_ref, ff_ref, rqb_ref, rib_ref, fb_ref, lb_ref,
               *refs, has_s0):
    if has_s0:
        s0_ref, of_ref, ob_ref, so_ref, sf_scr, sb_scr = refs
    else:
        of_ref, ob_ref, so_ref, sf_scr, sb_scr = refs
    c = pl.program_id(2)

    @pl.when(c == 0)
    def _():
        for g in range(REC_HEADS_PER_STEP):
            if has_s0:
                sf_scr[g] = s0_ref[0, 0, g].T
                sb_scr[g] = s0_ref[0, 1, g].T
            else:
                sf_scr[g] = jnp.zeros((HEAD_W, HEAD_W), F32)
                sb_scr[g] = jnp.zeros((HEAD_W, HEAD_W), F32)

    for g in range(0, REC_HEADS_PER_STEP, 2):
        ls = slice(g * HEAD_W, (g + 2) * HEAD_W)
        o, st = _rec_pair(rqf_ref[0, :, ls], rif_ref[0, :, ls], ff_ref[0, :, ls], lb_ref[0:1, ls], mf_ref,
                          lvf_ref[...], [sf_scr[g], sf_scr[g + 1]], False)
        of_ref[0, :, ls] = o
        sf_scr[g] = st[0]
        sf_scr[g + 1] = st[1]
        o, st = _rec_pair(rqb_ref[0, :, ls], rib_ref[0, :, ls], fb_ref[0, :, ls], lb_ref[1:2, ls], mb_ref,
                          lvb_ref[...], [sb_scr[g], sb_scr[g + 1]], True)
        ob_ref[0, :, ls] = o
        sb_scr[g] = st[0]
        sb_scr[g + 1] = st[1]

    @pl.when(c == pl.num_programs(2) - 1)
    def _():
        for g in range(REC_HEADS_PER_STEP):
            so_ref[0, 0, g] = sf_scr[g].T
            so_ref[0, 1, g] = sb_scr[g].T


def _hgrn(z, lb, s0, layer, consts):
    B, T, _ = z.shape
    n = T // REC_C
    C = REC_C
    G = REC_HEADS_PER_STEP
    ng = N_HEADS // G
    W = G * HEAD_W

    def fwd(col):
        return pl.BlockSpec((1, C, W), lambda b, h, c: (b, c, col * ng + h))

    def bwd(col):
        return pl.BlockSpec((1, C, W), lambda b, h, c: (b, n - 1 - c, col * ng + h))

    const2 = lambda b, h, c: (0, 0)
    in_specs = [pl.BlockSpec(((2 + REC_LEVELS) * C, 2 * C), const2), pl.BlockSpec(((2 + REC_LEVELS) * C, 2 * C), const2),
                pl.BlockSpec((C, C), const2), pl.BlockSpec((C, C), const2),
                fwd(COL_RQ), fwd(COL_RI), fwd(COL_RFF), bwd(COL_RQ), bwd(COL_RI), bwd(COL_RFB),
                pl.BlockSpec((2, W), lambda b, h, c: (0, h))]
    args = list(consts) + [z] * 6 + [lb]
    if s0 is not None:
        s0v = s0.reshape(B, DEPTH * 2, N_HEADS, HEAD_W, HEAD_W)
        in_specs.append(pl.BlockSpec((1, 2, G, HEAD_W, HEAD_W), lambda b, h, c: (b, layer, h, 0, 0)))
        args.append(s0v)
    return pl.pallas_call(
        functools.partial(_hgrn_body, has_s0=s0 is not None),
        grid=(B, ng, n),
        in_specs=in_specs,
        out_specs=[pl.BlockSpec((1, C, W), lambda b, h, c: (b, c, h)),
                   pl.BlockSpec((1, C, W), lambda b, h, c: (b, n - 1 - c, h)),
                   pl.BlockSpec((1, 2, G, HEAD_W, HEAD_W), lambda b, h, c: (b, 0, h, 0, 0))],
        out_shape=[jax.ShapeDtypeStruct((B, T, D_MODEL), F32), jax.ShapeDtypeStruct((B, T, D_MODEL), F32),
                   jax.ShapeDtypeStruct((B, 2, N_HEADS, HEAD_W, HEAD_W), F32)],
        scratch_shapes=[pltpu.VMEM((G, HEAD_W, HEAD_W), F32), pltpu.VMEM((G, HEAD_W, HEAD_W), F32)],
        compiler_params=_cparams(("parallel", "parallel", "arbitrary")),
        name="hgrn_ctx" if s0 is not None else "hgrn",
    )(*args)


def _merge_body(oatt_ref, zb_ref, zc_ref, zx_ref, zcp_ref, zxp_ref, zcn_ref, zxn_ref, of_ref, ob_ref, rg_ref,
                g0_ref, g1_ref, g2_ref, x_ref, mod_ref, wb_ref, wo_ref, cw_ref, cb_ref, rn_ref, ln_ref,
                x1_ref, h2_ref, *, tm):
    i = pl.program_id(1)
    u = zc_ref[0] * zx_ref[0]
    row = lax.broadcasted_iota(jnp.int32, (tm, D_MODEL), 0)
    prev_ok = (i > 0).astype(F32)
    next_ok = (i < pl.num_programs(1) - 1).astype(F32)
    u_prev_edge = zcp_ref[0, 7:8, :] * zxp_ref[0, 7:8, :] * prev_ok
    u_next_edge = zcn_ref[0, 0:1, :] * zxn_ref[0, 0:1, :] * next_ok
    up = jnp.where(row == 0, u_prev_edge, pltpu.roll(u, 1, 0))
    un = jnp.where(row == tm - 1, u_next_edge, pltpu.roll(u, tm - 1, 0))
    conv = up * cw_ref[0:1, :] + u * cw_ref[1:2, :] + un * cw_ref[2:3, :] + cb_ref[...]
    o_conv = (zb_ref[0] * conv).astype(BF16)
    s = of_ref[0] + ob_ref[0]
    rg = rg_ref[0]
    parts = []
    for h in range(N_HEADS):
        sh = s[:, h * HEAD_W:(h + 1) * HEAD_W]
        parts.append(sh * lax.rsqrt(jnp.mean(sh * sh, axis=-1, keepdims=True) + LN_EPS))
    o_rec = (jnp.concatenate(parts, axis=1) * rn_ref[...] * (rg * _sigmoid(rg))).astype(BF16)
    merged = (_sigmoid(g0_ref[0]) * _dot(oatt_ref[0], wb_ref[0])
              + _sigmoid(g1_ref[0]) * _dot(o_conv, wb_ref[1])
              + _sigmoid(g2_ref[0]) * _dot(o_rec, wb_ref[2]))
    mix = _dot(merged.astype(BF16), wo_ref[...])
    y = DEEPNORM_ALPHA * x_ref[0] + mod_ref[0, 2:3, :] * mix
    x1 = _layer_norm(y, ln_ref[0:1, :], ln_ref[1:2, :])
    x1_ref[0] = x1
    h2_ref[0] = (x1 * (1.0 + mod_ref[0, 4:5, :]) + mod_ref[0, 3:4, :]).astype(BF16)


def _merge(x, z, oatt, o_f, o_b, mod, wb_bf16, wo_bf16, conv_w, conv_b, rec_norm_t, ln_gb, tm):
    B, T, _ = x.shape
    per_b = mod.shape[0] > 1
    nb8 = tm // 8
    last8 = T // 8 - 1

    def col(c):
        return pl.BlockSpec((1, tm, D_MODEL), lambda b, i: (b, i, c))

    def prev8(c):
        return pl.BlockSpec((1, 8, D_MODEL), lambda b, i: (b, jnp.maximum(i * nb8 - 1, 0), c))

    def next8(c):
        return pl.BlockSpec((1, 8, D_MODEL), lambda b, i: (b, jnp.minimum((i + 1) * nb8, last8), c))

    tile = pl.BlockSpec((1, tm, D_MODEL), lambda b, i: (b, i, 0))
    full2 = lambda b, i: (0, 0)
    in_specs = [tile, col(COL_CB), col(COL_CC), col(COL_CX), prev8(COL_CC), prev8(COL_CX), next8(COL_CC), next8(COL_CX),
                tile, tile, col(COL_RG), col(COL_G), col(COL_G + 1), col(COL_G + 2), tile,
                pl.BlockSpec((1, 6, D_MODEL), (lambda b, i: (b, 0, 0)) if per_b else (lambda b, i: (0, 0, 0))),
                pl.BlockSpec((3, D_MODEL, D_MODEL), lambda b, i: (0, 0, 0)),
                pl.BlockSpec((D_MODEL, D_MODEL), full2),
                pl.BlockSpec((3, D_MODEL), full2), pl.BlockSpec((1, D_MODEL), full2),
                pl.BlockSpec((1, D_MODEL), full2), pl.BlockSpec((2, D_MODEL), full2)]
    return pl.pallas_call(
        functools.partial(_merge_body, tm=tm),
        grid=(B, T // tm),
        in_specs=in_specs,
        out_specs=[tile, tile],
        out_shape=[jax.ShapeDtypeStruct((B, T, D_MODEL), F32), jax.ShapeDtypeStruct((B, T, D_MODEL), BF16)],
        compiler_params=_cparams(("parallel", "arbitrary")),
        name="merge",
    )(oatt, z, z, z, z, z, z, z, o_f, o_b, z, z, z, z, x, mod, wb_bf16, wo_bf16, conv_w, conv_b, rec_norm_t, ln_gb)


def _top_rows(s, k, one_per_round, want_rank=True):
    n, w = s.shape
    s_in = s
    track = one_per_round or want_rank
    rid = lax.broadcasted_iota(jnp.int32, (n, w), 0).astype(F32)
    rank = jnp.full((n, w), float(k), F32) if track else None
    vals = []
    for r in range(k):
        m = jnp.max(s, axis=0, keepdims=True)
        hit = s == m
        if one_per_round:
            first = jnp.min(jnp.where(hit, rid, float(n)), axis=0, keepdims=True)
            hit = rid == first
        s = jnp.where(hit, -jnp.inf, s)
        if track:
            rank = jnp.where(hit, float(r), rank)
        vals.append(m)
    taken = (rank < k) if track else (s_in >= vals[-1])
    excess = jnp.sum(jnp.where(taken, 1.0, 0.0), axis=0, keepdims=True) - k
    return vals, rank, excess


def _route_body(h_ref, wq_ref, keys_ref, n1_ref, e1_ref, r2_ref, e2_ref, q_scr, s_scr, *, tt):
    q_scr[...] = _dot(h_ref[0], wq_ref[...]).astype(BF16)

    def head(h, carry):
        for p in range(2):
            off = pl.multiple_of((2 * h + p) * HEAD_W, HEAD_W)
            s_scr[p] = _dot_nt(keys_ref[2 * h + p], q_scr[:, pl.ds(off, HEAD_W)])
        excess = route_head(h, False)

        @pl.when(jnp.max(excess) > 0.0)
        def _():
            route_head(h, True)

        return carry

    def route_head(h, one_per_round):
        excess = jnp.zeros((1, HEAD_W), F32)
        for lc in range(tt // HEAD_W):
            ls = slice(lc * HEAD_W, (lc + 1) * HEAD_W)
            s1 = s_scr[0, :, ls]
            s2 = s_scr[1, :, ls]
            v1, r1, x1 = _top_rows(s1, PEER_TOPK, one_per_round, want_rank=False)
            v2, r2, x2 = _top_rows(s2, PEER_TOPK, one_per_round)
            v2a = jnp.concatenate(v2, axis=0)
            cands = [v1[0] + v2a] + [v1[a] + v2a[:8] for a in range(1, 8)] + [jnp.concatenate(v1[8:], axis=0) + v2[0]]
            cand = jnp.concatenate(cands, axis=0)
            vc, rc, xc = _top_rows(cand, PEER_TOPK, one_per_round, want_rank=False)
            excess = jnp.maximum(excess, jnp.maximum(jnp.maximum(x1, x2), xc))
            sel = (rc < PEER_TOPK) if one_per_round else (cand >= vc[-1])
            zsum = jnp.sum(jnp.where(sel, jnp.exp(cand - (v1[0] + v2[0])), 0.0), axis=0, keepdims=True)
            selc = jnp.where(sel, 1.0, 0.0)
            n_a = [jnp.sum(selc[0:16], axis=0, keepdims=True)]
            n_a += [jnp.sum(selc[8 + 8 * a:16 + 8 * a], axis=0, keepdims=True) for a in range(1, 8)]
            n_a += [selc[72 + a:73 + a] for a in range(8)]
            n1 = jnp.zeros_like(s1)
            for a in range(PEER_TOPK):
                n1 = jnp.where((r1 == a) if one_per_round else (s1 == v1[a]), n_a[a], n1)
            n1_ref[h, :, ls] = n1
            e1_ref[h, :, ls] = jnp.exp(s1 - v1[0]) * (0.5 / zsum)
            r2_ref[h, :, ls] = r2.astype(BF16)
            e2_ref[h, :, ls] = jnp.exp(s2 - v2[0]).astype(BF16)
        return excess

    lax.fori_loop(0, N_HEADS, head, 0)


def _peer_route(h2, wq_bf16, keys_bf16, tt):
    B, T, _ = h2.shape
    big = pl.BlockSpec((None, N_HEADS, PEER_N_KEYS, tt), lambda b, i: (b, 0, 0, i))
    f32_shape = jax.ShapeDtypeStruct((B, N_HEADS, PEER_N_KEYS, T), F32)
    bf16_shape = jax.ShapeDtypeStruct((B, N_HEADS, PEER_N_KEYS, T), BF16)
    return pl.pallas_call(
        functools.partial(_route_body, tt=tt),
        grid=(B, T // tt),
        in_specs=[pl.BlockSpec((1, tt, D_MODEL), lambda b, i: (b, i, 0)),
                  pl.BlockSpec((D_MODEL, 2 * N_HEADS * HEAD_W), lambda b, i: (0, 0)),
                  pl.BlockSpec((2 * N_HEADS, PEER_N_KEYS, HEAD_W), lambda b, i: (0, 0, 0))],
        out_specs=[big, big, big, big],
        out_shape=[f32_shape, f32_shape, bf16_shape, bf16_shape],
        scratch_shapes=[pltpu.VMEM((tt, 2 * N_HEADS * HEAD_W), BF16), pltpu.VMEM((2, PEER_N_KEYS, tt), F32)],
        compiler_params=_cparams(("parallel", "parallel")),
        name="peer_route",
    )(h2, wq_bf16, keys_bf16)


PEER_STEP_KEYS = 16
PEER_SUB_KEYS = 4


def _dense_body(h_ref, u_ref, vt_ref, n1_ref, e1_ref, r2_ref, e2_ref, x_ref, mod_ref, ln_ref, o_ref,
                act_scr, g_scr, w_scr, acc_scr, r2_scr, e2_scr, *, tt):
    k = pl.program_id(2)

    @pl.when(k == 0)
    def _():
        acc_scr[...] = jnp.zeros_like(acc_scr)
        r2_scr[...] = r2_ref[...]
        e2_scr[...] = e2_ref[...]

    hb = h_ref[0]
    sub = PEER_SUB_KEYS * PEER_N_KEYS

    def routing_weights(jj):
        for j in range(PEER_SUB_KEYS * jj, PEER_SUB_KEYS * (jj + 1)):
            for lc in range(tt // HEAD_W):
                ls = slice(lc * HEAD_W, (lc + 1) * HEAD_W)
                g = None
                for h in range(N_HEADS):
                    n_t = jnp.broadcast_to(n1_ref[h, j:j + 1, ls], (16, HEAD_W)).astype(BF16)[None]
                    e_t = jnp.broadcast_to(e1_ref[h, j:j + 1, ls], (16, HEAD_W)).astype(BF16)[None]
                    hit = r2_scr[h, :, ls].reshape(PEER_N_KEYS // 16, 16, HEAD_W) < n_t
                    term = jnp.where(hit, e2_scr[h, :, ls].reshape(PEER_N_KEYS // 16, 16, HEAD_W), 0.0) * e_t
                    g = term if g is None else g + term
                g_scr[j * PEER_N_KEYS:(j + 1) * PEER_N_KEYS, ls] = g.reshape(PEER_N_KEYS, HEAD_W)
        return lax.shift_right_logical(pltpu.bitcast(g[0], jnp.uint32), jnp.uint32(32))

    def after(x, zero_tile):
        xi = pltpu.bitcast(x, jnp.uint32)
        z = jnp.tile(zero_tile, (xi.shape[0] // zero_tile.shape[0], xi.shape[1] // zero_tile.shape[1]))
        return pltpu.bitcast(xi | z, x.dtype)

    nsub = PEER_STEP_KEYS // PEER_SUB_KEYS
    half = sub * nsub // 2
    act_scr[0:half, :] = _dot_nt(u_ref[0:half, :], hb)
    zero_tile = routing_weights(0)
    act_scr[half:, :] = _dot_nt(after(u_ref[half:, :], zero_tile), hb)
    for jj in range(1, nsub):
        routing_weights(jj)
    for jj in range(nsub):
        rows = slice(jj * sub, (jj + 1) * sub)
        a = act_scr[rows, :]
        gelu2 = a * (1.0 + lax.erf(a * (2.0 ** -0.5)))
        w_scr[rows, :] = (gelu2 * g_scr[rows, :].astype(F32)).astype(BF16)
        acc_scr[...] += _dot(vt_ref[0, :, rows], w_scr[rows, :])

    @pl.when(k == pl.num_programs(2) - 1)
    def _():
        y = DEEPNORM_ALPHA * x_ref[0] + mod_ref[0, 5:6, :] * acc_scr[...].T
        o_ref[0] = _layer_norm(y, ln_ref[0:1, :], ln_ref[1:2, :])


def _peer_dense(h2, x1, mod, u_bf16, vt_bf16, route, ln_gb, tt):
    B, T, _ = h2.shape
    n1, e1, r2, e2 = route
    per_b = mod.shape[0] > 1
    ne = PEER_STEP_KEYS * PEER_N_KEYS
    rows = pl.BlockSpec((None, N_HEADS, PEER_STEP_KEYS, tt), lambda b, i, k: (b, 0, k, i))
    full = pl.BlockSpec((None, N_HEADS, PEER_N_KEYS, tt), lambda b, i, k: (b, 0, 0, i))
    return pl.pallas_call(
        functools.partial(_dense_body, tt=tt),
        grid=(B, T // tt, PEER_N_KEYS // PEER_STEP_KEYS),
        in_specs=[pl.BlockSpec((1, tt, D_MODEL), lambda b, i, k: (b, i, 0)),
                  pl.BlockSpec((ne, D_MODEL), lambda b, i, k: (k, 0)),
                  pl.BlockSpec((1, D_MODEL, ne), lambda b, i, k: (k, 0, 0)),
                  rows, rows, full, full,
                  pl.BlockSpec((1, tt, D_MODEL), lambda b, i, k: (b, i, 0)),
                  pl.BlockSpec((1, 6, D_MODEL), (lambda b, i, k: (b, 0, 0)) if per_b else (lambda b, i, k: (0, 0, 0))),
                  pl.BlockSpec((2, D_MODEL), lambda b, i, k: (0, 0))],
        out_specs=pl.BlockSpec((1, tt, D_MODEL), lambda b, i, k: (b, i, 0)),
        out_shape=jax.ShapeDtypeStruct((B, T, D_MODEL), F32),
        scratch_shapes=[pltpu.VMEM((ne, tt), F32), pltpu.VMEM((ne, tt), BF16), pltpu.VMEM((ne, tt), BF16),
                        pltpu.VMEM((D_MODEL, tt), F32),
                        pltpu.VMEM((N_HEADS, PEER_N_KEYS, tt), BF16), pltpu.VMEM((N_HEADS, PEER_N_KEYS, tt), BF16)],
        compiler_params=_cparams(("parallel", "parallel", "arbitrary")),
        name="peer_dense",
    )(h2, u_bf16, vt_bf16, n1, e1, r2, e2, x1, mod, ln_gb)


def _trunk_layer(x, mod, layer, p, ctx, rope_tabs, consts, flat_rows):
    B, T, _ = x.shape
    if flat_rows is not None:
        z = _in_proj(x.reshape(-1, flat_rows, D_MODEL), mod, p["w_in"], None, tm=flat_rows)
        z = z.reshape(B, T, W_IN_COLS)
    else:
        z = _in_proj(x, mod, p["w_in"], rope_tabs, tm=PROJ_ROWS)
    if ctx is None:
        oatt = _attention(z, p["lam"], p["subln"], None, layer, p["att_scale"], tq=min(ATT_Q_ROWS, T), tk=T)
        o_f, o_b, s_fin = _hgrn(z, p["lb"], None, layer, consts)
    else:
        oatt = _attention(z, p["lam"], p["subln"], (ctx[0], ctx[1]), layer, p["att_scale"], tq=ATT_Q_ROWS,
                          tk=ATT_KEY_CHUNK)
        o_f, o_b, s_fin = _hgrn(z, p["lb"], ctx[2], layer, consts)
    x1, h2 = _merge(x, z, oatt, o_f, o_b, mod, p["w_branch"], p["w_out"], p["conv_w"], p["conv_b"],
                    p["rec_norm"], p["ln0"], tm=MERGE_ROWS)
    if flat_rows is not None:
        h2r, x1r = h2.reshape(-1, flat_rows, D_MODEL), x1.reshape(-1, flat_rows, D_MODEL)
    else:
        h2r, x1r = h2, x1
    route = _peer_route(h2r, p["peer_wq"], p["peer_keys"], PEER_TOKENS)
    x2 = _peer_dense(h2r, x1r, mod, p["peer_u"], p["peer_vt"], route, p["ln1"], PEER_TOKENS).reshape(B, T, D_MODEL)
    return x2, z, s_fin


def _layer_params(l, lb_all, w_in, attn_lambda, attn_subln, conv_w, conv_b, rec_norm, w_branch, w_out, ln_g, ln_b,
                  peer_wq, peer_keys, peer_u, peer_v):
    lam_init = 0.8 - 0.6 * math.exp(-0.3 * l)
    lp = attn_lambda[l].astype(F32)
    lam = jnp.exp(jnp.sum(lp[0] * lp[1])) - jnp.exp(jnp.sum(lp[2] * lp[3])) + lam_init
    return {
        "w_in": w_in[l].astype(BF16), "lam": lam.reshape(1), "att_scale": 1.0 - lam_init,
        "subln": attn_subln[l].reshape(1, HEAD_W), "lb": lb_all[l],
        "w_branch": w_branch[l].astype(BF16), "w_out": w_out[l].astype(BF16),
        "conv_w": conv_w[l], "conv_b": conv_b[l].reshape(1, D_MODEL),
        "rec_norm": jnp.tile(rec_norm[l], N_HEADS).reshape(1, D_MODEL),
        "ln0": jnp.stack([ln_g[l, 0], ln_b[l, 0]]), "ln1": jnp.stack([ln_g[l, 1], ln_b[l, 1]]),
        "peer_wq": peer_wq[l].astype(BF16),
        "peer_keys": peer_keys[l].astype(BF16).reshape(2 * N_HEADS, PEER_N_KEYS, HEAD_W),
        "peer_u": peer_u[l].astype(BF16),
        "peer_vt": peer_v[l].astype(BF16).reshape(-1, PEER_STEP_KEYS * PEER_N_KEYS, D_MODEL).transpose(0, 2, 1),
    }


def kernel(x_prompt, x_sample, c, cache_attn_k, cache_attn_v, state_hgrn, c_ctx, mod_w, mod_b, w_in, attn_lambda,
           attn_subln, conv_w, conv_b, rec_lb, rec_norm, w_branch, w_out, ln_g, ln_b, peer_wq, peer_keys, peer_u,
           peer_v):
    B, T, _ = x_prompt.shape
    Bs, Ts, _ = x_sample.shape
    P = cache_attn_k.shape[2]
    lb_all = jnp.cumsum(jax.nn.softmax(rec_lb.astype(F32), axis=0), axis=0)
    lb_all = lb_all - lb_all[:1]
    cmat = jnp.concatenate([c_ctx[None, :], c, jnp.zeros((8 - 1 - Bs, D_MODEL), F32)], axis=0)
    mods = _mod_vectors(cmat, mod_w, mod_b).reshape(DEPTH, 8, 6, D_MODEL)
    rope_tabs = _rope_tables(Ts)
    consts = _rec_constants()
    ck = cache_attn_k.reshape(Bs, DEPTH, P, N_HEADS * HEAD_W)
    cv = cache_attn_v.reshape(Bs, DEPTH, P, N_HEADS * HEAD_W)

    y_p, y_s = x_prompt, x_sample
    ks, vs, ss = [], [], []
    for l in range(DEPTH):
        p = _layer_params(l, lb_all, w_in, attn_lambda, attn_subln, conv_w, conv_b, rec_norm, w_branch, w_out,
                          ln_g, ln_b, peer_wq, peer_keys, peer_u, peer_v)
        y_p, z_p, s_p = _trunk_layer(y_p, mods[l, 0:1], l, p, None, None, consts, flat_rows=PROJ_ROWS)
        ks.append(z_p[..., COL_K * D_MODEL:(COL_K + 1) * D_MODEL].reshape(B, T, N_HEADS, HEAD_W))
        vs.append(z_p[..., COL_V * D_MODEL:(COL_V + 1) * D_MODEL].reshape(B, T, N_HEADS, HEAD_W))
        ss.append(s_p)
        y_s, _, _ = _trunk_layer(y_s, mods[l, 1:1 + Bs], l, p, (ck, cv, state_hgrn), rope_tabs, consts, flat_rows=None)
    return (y_p, y_s, jnp.stack(ks, axis=1), jnp.stack(vs, axis=1), jnp.stack(ss, axis=1))
```

```python
import functools
import math

import numpy as np
import jax
import jax.numpy as jnp
from jax import lax
from jax.experimental import pallas as pl
from jax.experimental.pallas import tpu as pltpu

F32 = jnp.float32
BF16 = jnp.bfloat16

D_MODEL = 1024
DEPTH = 2
GRID_W = 64
N_HEADS = 8
HEAD_W = 128
ATT_QK_DIM = 64
ROPE_BASE = 10000.0
ROPE_AXIS_DIM = ATT_QK_DIM // 2
PEER_N_KEYS = 128
PEER_TOPK = 16
LN_EPS = 1e-5
DEEPNORM_ALPHA = (2 * DEPTH) ** 0.25
W_IN_COLS = 14 * D_MODEL
COL_Q, COL_K, COL_V, COL_CB, COL_CC, COL_CX, COL_RQ, COL_RFF, COL_RFB, COL_RI, COL_RG, COL_G = range(12)

VMEM_LIMIT = 56 * 1024 * 1024
PROJ_ROWS = 2048
PROJ_COLS = 512
ATT_Q_ROWS = 1024
ATT_KEY_CHUNK = 512
MERGE_ROWS = 256
PEER_TOKENS = 512
REC_C = 128
REC_LEVELS = 7
REC_HEADS_PER_STEP = 8


def _cparams(sem):
    return pltpu.CompilerParams(dimension_semantics=sem, vmem_limit_bytes=VMEM_LIMIT)


def _dot(a, b):
    return jnp.dot(a, b, preferred_element_type=F32)


def _dot_nt(a, b):
    return lax.dot_general(a, b, (((1,), (1,)), ((), ())), preferred_element_type=F32)


def _sigmoid(x):
    e = jnp.exp(-jnp.abs(x))
    r = 1.0 / (1.0 + e)
    return jnp.where(x >= 0, r, e * r)


def _layer_norm(y, g, b):
    mu = jnp.mean(y, axis=-1, keepdims=True)
    yc = y - mu
    var = jnp.mean(yc * yc, axis=-1, keepdims=True)
    return yc * lax.rsqrt(var + LN_EPS) * g + b


def _mod_body(c_ref, w_ref, b_ref, o_ref):
    c = c_ref[...]
    s = (c * _sigmoid(c)).astype(BF16)
    o_ref[0] = _dot(s, w_ref[0].astype(BF16)) + b_ref[0]


def _mod_vectors(cmat, mod_w, mod_b):
    tn = 1536
    return pl.pallas_call(
        _mod_body,
        grid=(DEPTH, 6 * D_MODEL // tn),
        in_specs=[pl.BlockSpec((8, D_MODEL), lambda l, j: (0, 0)),
                  pl.BlockSpec((1, D_MODEL, tn), lambda l, j: (l, 0, j)),
                  pl.BlockSpec((1, 1, tn), lambda l, j: (l, 0, j))],
        out_specs=pl.BlockSpec((1, 8, tn), lambda l, j: (l, 0, j)),
        out_shape=jax.ShapeDtypeStruct((DEPTH, 8, 6 * D_MODEL), F32),
        compiler_params=_cparams(("parallel", "parallel")),
        name="mod_vectors",
    )(cmat, mod_w, mod_b.reshape(DEPTH, 1, 6 * D_MODEL))


def _in_proj_body(x_ref, mod_ref, w_ref, *refs, rope, tm):
    if rope:
        cos_ref, sin_ref, o_ref, h_ref = refs
    else:
        o_ref, h_ref = refs
    j = pl.program_id(2)

    @pl.when(j == 0)
    def _():
        h_ref[...] = (x_ref[0] * (1.0 + mod_ref[0, 1:2, :]) + mod_ref[0, 0:1, :]).astype(BF16)

    z = _dot(h_ref[...], w_ref[...])
    if not rope:
        o_ref[0] = z
        return
    rope_tiles = COL_V * D_MODEL // PROJ_COLS

    @pl.when(j < rope_tiles)
    def _():
        cos = cos_ref[...]
        sin = sin_ref[...]
        lane = lax.broadcasted_iota(jnp.int32, (tm, HEAD_W), 1)
        first = (lane % ROPE_AXIS_DIM) < (ROPE_AXIS_DIM // 2)
        for g in range(PROJ_COLS // HEAD_W):
            zg = z[:, g * HEAD_W:(g + 1) * HEAD_W]
            partner = jnp.where(first, pltpu.roll(zg, HEAD_W - ROPE_AXIS_DIM // 2, 1),
                                pltpu.roll(zg, ROPE_AXIS_DIM // 2, 1))
            o_ref[0, :, g * HEAD_W:(g + 1) * HEAD_W] = zg * cos + partner * sin

    @pl.when(j >= rope_tiles)
    def _():
        o_ref[0] = z


def _in_proj(x, mod, w_bf16, rope_tabs, tm):
    B, T, _ = x.shape
    per_b = mod.shape[0] > 1
    rope = rope_tabs is not None
    in_specs = [pl.BlockSpec((1, tm, D_MODEL), lambda b, i, j: (b, i, 0)),
                pl.BlockSpec((1, 6, D_MODEL), (lambda b, i, j: (b, 0, 0)) if per_b else (lambda b, i, j: (0, 0, 0))),
                pl.BlockSpec((D_MODEL, PROJ_COLS), lambda b, i, j: (0, j))]
    args = [x, mod, w_bf16]
    if rope:
        in_specs += [pl.BlockSpec((tm, HEAD_W), lambda b, i, j: (i, 0))] * 2
        args += list(rope_tabs)
    return pl.pallas_call(
        functools.partial(_in_proj_body, rope=rope, tm=tm),
        grid=(B, T // tm, W_IN_COLS // PROJ_COLS),
        in_specs=in_specs,
        out_specs=pl.BlockSpec((1, tm, PROJ_COLS), lambda b, i, j: (b, i, j)),
        out_shape=jax.ShapeDtypeStruct((B, T, W_IN_COLS), F32),
        scratch_shapes=[pltpu.VMEM((tm, D_MODEL), BF16)],
        compiler_params=_cparams(("parallel", "parallel", "arbitrary")),
        name="in_proj_rope" if rope else "in_proj",
    )(*args)


def _rope_tables(T):
    t = np.arange(T)
    pos = np.stack([t // GRID_W, t % GRID_W], axis=1).astype(np.float32)
    lane = np.arange(HEAD_W)
    axis = (lane % ATT_QK_DIM) // ROPE_AXIS_DIM
    r = lane % ROPE_AXIS_DIM
    half = ROPE_AXIS_DIM // 2
    freqs = ROPE_BASE ** (-jnp.arange(0, ROPE_AXIS_DIM, 2, dtype=F32) / ROPE_AXIS_DIM)
    ang = jnp.asarray(pos)[:, axis] * freqs[r % half][None, :]
    sign = jnp.asarray(np.where(r < half, -1.0, 1.0).astype(np.float32))[None, :]
    return jnp.cos(ang), jnp.sin(ang) * sign


def _attn_body(lam_ref, q_ref, k_ref, v_ref, *refs, tq, tk, n_self, n_ctx, out_scale):
    if n_ctx:
        kc_ref, vc_ref, g_ref, o_ref, kb_ref, vb_ref = refs
    else:
        g_ref, o_ref, kb_ref, vb_ref = refs
    t_self = n_self * tk

    @pl.when(pl.program_id(2) == 0)
    def _():
        kb_ref[0:t_self, :] = k_ref[0].astype(BF16)
        vb_ref[0:t_self, :] = v_ref[0].astype(BF16)
        if n_ctx:
            kb_ref[t_self:, :] = kc_ref[0, 0].astype(BF16)
            vb_ref[t_self:, :] = vc_ref[0, 0].astype(BF16)

    q = q_ref[0] * (ATT_QK_DIM ** -0.5 * math.log2(math.e))
    lane = lax.broadcasted_iota(jnp.int32, (tq, HEAD_W), 1)
    lo = lane < ATT_QK_DIM
    qs = jnp.concatenate([jnp.where(lo, q, 0.0), jnp.where(lo, 0.0, q)], axis=0).astype(BF16)

    m = jnp.full((2 * tq, 1), -jnp.inf, F32)
    l = jnp.zeros((2 * tq, 1), F32)
    acc = jnp.zeros((2 * tq, HEAD_W), F32)
    for c in range(n_self + n_ctx):
        s = _dot_nt(qs, kb_ref[c * tk:(c + 1) * tk, :])
        m_new = jnp.maximum(m, jnp.max(s, axis=-1, keepdims=True))
        a = jnp.exp2(m - m_new)
        p = jnp.exp2(s - m_new)
        l = a * l + jnp.sum(p, axis=-1, keepdims=True)
        acc = a * acc + _dot(p.astype(BF16), vb_ref[c * tk:(c + 1) * tk, :])
        m = m_new
    o = acc / l
    o = o[:tq] - lam_ref[0] * o[tq:]
    o = o * lax.rsqrt(jnp.mean(o * o, axis=-1, keepdims=True) + LN_EPS) * g_ref[...] * out_scale
    o_ref[0] = o.astype(BF16)


def _attention(z, lam, subln, ctx_kv, layer, out_scale, tq, tk):
    B, T, _ = z.shape
    n_self = T // tk
    in_specs = [pl.BlockSpec(memory_space=pltpu.SMEM),
                pl.BlockSpec((1, tq, HEAD_W), lambda b, h, i: (b, i, COL_Q * N_HEADS + h)),
                pl.BlockSpec((1, T, HEAD_W), lambda b, h, i: (b, 0, COL_K * N_HEADS + h)),
                pl.BlockSpec((1, T, HEAD_W), lambda b, h, i: (b, 0, COL_V * N_HEADS + h))]
    args = [lam, z, z, z]
    t_all = T
    n_ctx = 0
    if ctx_kv is not None:
        P = ctx_kv[0].shape[2]
        assert P == tk
        n_ctx = 1
        t_all = T + P
        in_specs += [pl.BlockSpec((1, 1, P, HEAD_W), lambda b, h, i: (b, layer, 0, h))] * 2
        args += list(ctx_kv)
    in_specs.append(pl.BlockSpec((1, HEAD_W), lambda b, h, i: (0, 0)))
    args.append(subln)
    return pl.pallas_call(
        functools.partial(_attn_body, tq=tq, tk=tk, n_self=n_self, n_ctx=n_ctx, out_scale=out_scale),
        grid=(B, N_HEADS, T // tq),
        in_specs=in_specs,
        out_specs=pl.BlockSpec((1, tq, HEAD_W), lambda b, h, i: (b, i, h)),
        out_shape=jax.ShapeDtypeStruct((B, T, D_MODEL), BF16),
        scratch_shapes=[pltpu.VMEM((t_all, HEAD_W), BF16), pltpu.VMEM((t_all, HEAD_W), BF16)],
        compiler_params=_cparams(("parallel", "parallel", "arbitrary")),
        name="attention_ctx" if n_ctx else "attention",
    )(*args)


def _rec_constants():
    C = REC_C
    t = np.arange(C)[:, None]
    j = np.arange(C)[None, :]
    blocks = [(j <= t), (j > t)]
    for l in range(REC_LEVELS):
        m = 1 << l
        seg0 = (t // m) * m
        odd = ((t // m) % 2) == 1
        blocks.append(np.where(odd, (j >= seg0) & (j <= t), (j > t) & (j <= seg0 + m - 1)))
    mf = np.concatenate(blocks, axis=0).astype(np.float32)
    x = t ^ j
    lv = np.where(x == 0, REC_LEVELS, np.floor(np.log2(np.maximum(x, 1))).astype(np.int64))
    lvf = np.where(j <= t, lv, REC_LEVELS + 1).astype(np.int32)
    mb = mf.reshape(-1, C, C)[:, ::-1, ::-1].reshape(-1, C)
    lvb = lvf[::-1, ::-1]
    mf, mb = np.concatenate([mf, mf], axis=1), np.concatenate([mb, mb], axis=1)
    return (jnp.asarray(mf, BF16), jnp.asarray(np.ascontiguousarray(mb), BF16),
            jnp.asarray(lvf), jnp.asarray(np.ascontiguousarray(lvb)))


def _rec_pair(rq, ri, fx, lb, m_ref, lv, sts, backward):
    C = REC_C
    e = jnp.exp(-jnp.abs(fx))
    r = 1.0 / (1.0 + e)
    pos = fx >= 0
    sig = jnp.where(pos, r, e * r)
    nsig = jnp.where(pos, e * r, r)
    logf = jnp.log(lb + (1.0 - lb) * sig)
    kk = (1.0 - lb) * nsig
    q = rq * _sigmoid(rq)
    hi = logf.astype(BF16)
    mid = (logf - hi.astype(F32)).astype(BF16)
    lf2 = jnp.concatenate([hi, mid], axis=0)

    def expo(blk):
        return _dot(m_ref[blk * C:(blk + 1) * C, :], lf2)

    row = lax.broadcasted_iota(jnp.int32, (C, 2 * HEAD_W), 0)
    if backward:
        row = (C - 1) - row
    qb = q.astype(BF16)
    kb = kk.astype(BF16)
    ws = []
    for l in range(REC_LEVELS):
        odd = ((row >> l) & 1) == 1
        ws.append((jnp.exp(expo(2 + l)) * jnp.where(odd, q, kk)).astype(BF16))
    b_incl = expo(0)
    qi = (q * jnp.exp(b_incl)).astype(BF16)
    ki = (kk * jnp.exp(expo(1))).astype(BF16)
    dec = jnp.exp(b_incl[0:1, :] if backward else b_incl[C - 1:C, :])
    outs, new_sts = [], []
    for g in range(2):
        ls = slice(g * HEAD_W, (g + 1) * HEAD_W)
        a = jnp.where(lv == REC_LEVELS, _dot_nt(qb[:, ls], kb[:, ls]), 0.0)
        for l in range(REC_LEVELS):
            a = jnp.where(lv == l, _dot_nt(ws[l][:, ls], ws[l][:, ls]), a)
        rig = ri[:, ls]
        outs.append(_dot(a.astype(BF16), rig.astype(BF16)) + _dot_nt(qi[:, ls], sts[g].astype(BF16)))
        new_sts.append(sts[g] * dec[:, ls] + _dot(rig.T.astype(BF16), ki[:, ls]))
    return jnp.concatenate(outs, axis=1), new_sts


def _hgrn_body(mf_ref, mb_ref, lvf_ref, lvb_ref, rqf_ref, rif_ref, ff_ref, rqb_ref, rib_ref, fb_ref, lb_ref,
               *refs, has_s0):
    if has_s0:
        s0_ref, of_ref, ob_ref, so_ref, sf_scr, sb_scr = refs
    else:
        of_ref, ob_ref, so_ref, sf_scr, sb_scr = refs
    c = pl.program_id(2)

    @pl.when(c == 0)
    def _():
        for g in range(REC_HEADS_PER_STEP):
            if has_s0:
                sf_scr[g] = s0_ref[0, 0, g].T
                sb_scr[g] = s0_ref[0, 1, g].T
            else:
                sf_scr[g] = jnp.zeros((HEAD_W, HEAD_W), F32)
                sb_scr[g] = jnp.zeros((HEAD_W, HEAD_W), F32)

    for g in range(0, REC_HEADS_PER_STEP, 2):
        ls = slice(g * HEAD_W, (g + 2) * HEAD_W)
        o, st = _rec_pair(rqf_ref[0, :, ls], rif_ref[0, :, ls], ff_ref[0, :, ls], lb_ref[0:1, ls], mf_ref,
                          lvf_ref[...], [sf_scr[g], sf_scr[g + 1]], False)
        of_ref[0, :, ls] = o
        sf_scr[g] = st[0]
        sf_scr[g + 1] = st[1]
        o, st = _rec_pair(rqb_ref[0, :, ls], rib_ref[0, :, ls], fb_ref[0, :, ls], lb_ref[1:2, ls], mb_ref,
                          lvb_ref[...], [sb_scr[g], sb_scr[g + 1]], True)
        ob_ref[0, :, ls] = o
        sb_scr[g] = st[0]
        sb_scr[g + 1] = st[1]

    @pl.when(c == pl.num_programs(2) - 1)
    def _():
        for g in range(REC_HEADS_PER_STEP):
            so_ref[0, 0, g] = sf_scr[g].T
            so_ref[0, 1, g] = sb_scr[g].T


def _hgrn(z, lb, s0, layer, consts):
    B, T, _ = z.shape
    n = T // REC_C
    C = REC_C
    G = REC_HEADS_PER_STEP
    ng = N_HEADS // G
    W = G * HEAD_W

    def fwd(col):
        return pl.BlockSpec((1, C, W), lambda b, h, c: (b, c, col * ng + h))

    def bwd(col):
        return pl.BlockSpec((1, C, W), lambda b, h, c: (b, n - 1 - c, col * ng + h))

    const2 = lambda b, h, c: (0, 0)
    in_specs = [pl.BlockSpec(((2 + REC_LEVELS) * C, 2 * C), const2), pl.BlockSpec(((2 + REC_LEVELS) * C, 2 * C), const2),
                pl.BlockSpec((C, C), const2), pl.BlockSpec((C, C), const2),
                fwd(COL_RQ), fwd(COL_RI), fwd(COL_RFF), bwd(COL_RQ), bwd(COL_RI), bwd(COL_RFB),
                pl.BlockSpec((2, W), lambda b, h, c: (0, h))]
    args = list(consts) + [z] * 6 + [lb]
    if s0 is not None:
        s0v = s0.reshape(B, DEPTH * 2, N_HEADS, HEAD_W, HEAD_W)
        in_specs.append(pl.BlockSpec((1, 2, G, HEAD_W, HEAD_W), lambda b, h, c: (b, layer, h, 0, 0)))
        args.append(s0v)
    return pl.pallas_call(
        functools.partial(_hgrn_body, has_s0=s0 is not None),
        grid=(B, ng, n),
        in_specs=in_specs,
        out_specs=[pl.BlockSpec((1, C, W), lambda b, h, c: (b, c, h)),
                   pl.BlockSpec((1, C, W), lambda b, h, c: (b, n - 1 - c, h)),
                   pl.BlockSpec((1, 2, G, HEAD_W, HEAD_W), lambda b, h, c: (b, 0, h, 0, 0))],
        out_shape=[jax.ShapeDtypeStruct((B, T, D_MODEL), F32), jax.ShapeDtypeStruct((B, T, D_MODEL), F32),
                   jax.ShapeDtypeStruct((B, 2, N_HEADS, HEAD_W, HEAD_W), F32)],
        scratch_shapes=[pltpu.VMEM((G, HEAD_W, HEAD_W), F32), pltpu.VMEM((G, HEAD_W, HEAD_W), F32)],
        compiler_params=_cparams(("parallel", "parallel", "arbitrary")),
        name="hgrn_ctx" if s0 is not None else "hgrn",
    )(*args)


def _merge_body(oatt_ref, zb_ref, zc_ref, zx_ref, zcp_ref, zxp_ref, zcn_ref, zxn_ref, of_ref, ob_ref, rg_ref,
                g0_ref, g1_ref, g2_ref, x_ref, mod_ref, wb_ref, wo_ref, cw_ref, cb_ref, rn_ref, ln_ref,
                x1_ref, h2_ref, *, tm):
    i = pl.program_id(1)
    u = zc_ref[0] * zx_ref[0]
    row = lax.broadcasted_iota(jnp.int32, (tm, D_MODEL), 0)
    prev_ok = (i > 0).astype(F32)
    next_ok = (i < pl.num_programs(1) - 1).astype(F32)
    u_prev_edge = zcp_ref[0, 7:8, :] * zxp_ref[0, 7:8, :] * prev_ok
    u_next_edge = zcn_ref[0, 0:1, :] * zxn_ref[0, 0:1, :] * next_ok
    up = jnp.where(row == 0, u_prev_edge, pltpu.roll(u, 1, 0))
    un = jnp.where(row == tm - 1, u_next_edge, pltpu.roll(u, tm - 1, 0))
    conv = up * cw_ref[0:1, :] + u * cw_ref[1:2, :] + un * cw_ref[2:3, :] + cb_ref[...]
    o_conv = (zb_ref[0] * conv).astype(BF16)
    s = of_ref[0] + ob_ref[0]
    rg = rg_ref[0]
    parts = []
    for h in range(N_HEADS):
        sh = s[:, h * HEAD_W:(h + 1) * HEAD_W]
        parts.append(sh * lax.rsqrt(jnp.mean(sh * sh, axis=-1, keepdims=True) + LN_EPS))
    o_rec = (jnp.concatenate(parts, axis=1) * rn_ref[...] * (rg * _sigmoid(rg))).astype(BF16)
    merged = (_sigmoid(g0_ref[0]) * _dot(oatt_ref[0], wb_ref[0])
              + _sigmoid(g1_ref[0]) * _dot(o_conv, wb_ref[1])
              + _sigmoid(g2_ref[0]) * _dot(o_rec, wb_ref[2]))
    mix = _dot(merged.astype(BF16), wo_ref[...])
    y = DEEPNORM_ALPHA * x_ref[0] + mod_ref[0, 2:3, :] * mix
    x1 = _layer_norm(y, ln_ref[0:1, :], ln_ref[1:2, :])
    x1_ref[0] = x1
    h2_ref[0] = (x1 * (1.0 + mod_ref[0, 4:5, :]) + mod_ref[0, 3:4, :]).astype(BF16)


def _merge(x, z, oatt, o_f, o_b, mod, wb_bf16, wo_bf16, conv_w, conv_b, rec_norm_t, ln_gb, tm):
    B, T, _ = x.shape
    per_b = mod.shape[0] > 1
    nb8 = tm // 8
    last8 = T // 8 - 1

    def col(c):
        return pl.BlockSpec((1, tm, D_MODEL), lambda b, i: (b, i, c))

    def prev8(c):
        return pl.BlockSpec((1, 8, D_MODEL), lambda b, i: (b, jnp.maximum(i * nb8 - 1, 0), c))

    def next8(c):
        return pl.BlockSpec((1, 8, D_MODEL), lambda b, i: (b, jnp.minimum((i + 1) * nb8, last8), c))

    tile = pl.BlockSpec((1, tm, D_MODEL), lambda b, i: (b, i, 0))
    full2 = lambda b, i: (0, 0)
    in_specs = [tile, col(COL_CB), col(COL_CC), col(COL_CX), prev8(COL_CC), prev8(COL_CX), next8(COL_CC), next8(COL_CX),
                tile, tile, col(COL_RG), col(COL_G), col(COL_G + 1), col(COL_G + 2), tile,
                pl.BlockSpec((1, 6, D_MODEL), (lambda b, i: (b, 0, 0)) if per_b else (lambda b, i: (0, 0, 0))),
                pl.BlockSpec((3, D_MODEL, D_MODEL), lambda b, i: (0, 0, 0)),
                pl.BlockSpec((D_MODEL, D_MODEL), full2),
                pl.BlockSpec((3, D_MODEL), full2), pl.BlockSpec((1, D_MODEL), full2),
                pl.BlockSpec((1, D_MODEL), full2), pl.BlockSpec((2, D_MODEL), full2)]
    return pl.pallas_call(
        functools.partial(_merge_body, tm=tm),
        grid=(B, T // tm),
        in_specs=in_specs,
        out_specs=[tile, tile],
        out_shape=[jax.ShapeDtypeStruct((B, T, D_MODEL), F32), jax.ShapeDtypeStruct((B, T, D_MODEL), BF16)],
        compiler_params=_cparams(("parallel", "arbitrary")),
        name="merge",
    )(oatt, z, z, z, z, z, z, z, o_f, o_b, z, z, z, z, x, mod, wb_bf16, wo_bf16, conv_w, conv_b, rec_norm_t, ln_gb)


def _top_rows(s, k, one_per_round, want_rank=True):
    n, w = s.shape
    s_in = s
    track = one_per_round or want_rank
    rid = lax.broadcasted_iota(jnp.int32, (n, w), 0).astype(F32)
    rank = jnp.full((n, w), float(k), F32) if track else None
    vals = []
    for r in range(k):
        m = jnp.max(s, axis=0, keepdims=True)
        hit = s == m
        if one_per_round:
            first = jnp.min(jnp.where(hit, rid, float(n)), axis=0, keepdims=True)
            hit = rid == first
        s = jnp.where(hit, -jnp.inf, s)
        if track:
            rank = jnp.where(hit, float(r), rank)
        vals.append(m)
    taken = (rank < k) if track else (s_in >= vals[-1])
    excess = jnp.sum(jnp.where(taken, 1.0, 0.0), axis=0, keepdims=True) - k
    return vals, rank, excess


def _route_body(h_ref, wq_ref, keys_ref, n1_ref, e1_ref, r2_ref, e2_ref, q_scr, s_scr, *, tt):
    q_scr[...] = _dot(h_ref[0], wq_ref[...]).astype(BF16)

    def head(h, carry):
        for p in range(2):
            off = pl.multiple_of((2 * h + p) * HEAD_W, HEAD_W)
            s_scr[p] = _dot_nt(keys_ref[2 * h + p], q_scr[:, pl.ds(off, HEAD_W)])
        excess = route_head(h, False)

        @pl.when(jnp.max(excess) > 0.0)
        def _():
            route_head(h, True)

        return carry

    def route_head(h, one_per_round):
        excess = jnp.zeros((1, HEAD_W), F32)
        for lc in range(tt // HEAD_W):
            ls = slice(lc * HEAD_W, (lc + 1) * HEAD_W)
            s1 = s_scr[0, :, ls]
            s2 = s_scr[1, :, ls]
            v1, r1, x1 = _top_rows(s1, PEER_TOPK, one_per_round, want_rank=False)
            v2, r2, x2 = _top_rows(s2, PEER_TOPK, one_per_round)
            v2a = jnp.concatenate(v2, axis=0)
            cands = [v1[0] + v2a] + [v1[a] + v2a[:8] for a in range(1, 8)] + [jnp.concatenate(v1[8:], axis=0) + v2[0]]
            cand = jnp.concatenate(cands, axis=0)
            vc, rc, xc = _top_rows(cand, PEER_TOPK, one_per_round, want_rank=False)
            excess = jnp.maximum(excess, jnp.maximum(jnp.maximum(x1, x2), xc))
            sel = (rc < PEER_TOPK) if one_per_round else (cand >= vc[-1])
            zsum = jnp.sum(jnp.where(sel, jnp.exp(cand - (v1[0] + v2[0])), 0.0), axis=0, keepdims=True)
            selc = jnp.where(sel, 1.0, 0.0)
            n_a = [jnp.sum(selc[0:16], axis=0, keepdims=True)]
            n_a += [jnp.sum(selc[8 + 8 * a:16 + 8 * a], axis=0, keepdims=True) for a in range(1, 8)]
            n_a += [selc[72 + a:73 + a] for a in range(8)]
            n1 = jnp.zeros_like(s1)
            for a in range(PEER_TOPK):
                n1 = jnp.where((r1 == a) if one_per_round else (s1 == v1[a]), n_a[a], n1)
            n1_ref[h, :, ls] = n1
            e1_ref[h, :, ls] = jnp.exp(s1 - v1[0]) * (0.5 / zsum)
            r2_ref[h, :, ls] = r2.astype(BF16)
            e2_ref[h, :, ls] = jnp.exp(s2 - v2[0]).astype(BF16)
        return excess

    lax.fori_loop(0, N_HEADS, head, 0)


def _peer_route(h2, wq_bf16, keys_bf16, tt):
    B, T, _ = h2.shape
    big = pl.BlockSpec((None, N_HEADS, PEER_N_KEYS, tt), lambda b, i: (b, 0, 0, i))
    f32_shape = jax.ShapeDtypeStruct((B, N_HEADS, PEER_N_KEYS, T), F32)
    bf16_shape = jax.ShapeDtypeStruct((B, N_HEADS, PEER_N_KEYS, T), BF16)
    return pl.pallas_call(
        functools.partial(_route_body, tt=tt),
        grid=(B, T // tt),
        in_specs=[pl.BlockSpec((1, tt, D_MODEL), lambda b, i: (b, i, 0)),
                  pl.BlockSpec((D_MODEL, 2 * N_HEADS * HEAD_W), lambda b, i: (0, 0)),
                  pl.BlockSpec((2 * N_HEADS, PEER_N_KEYS, HEAD_W), lambda b, i: (0, 0, 0))],
        out_specs=[big, big, big, big],
        out_shape=[f32_shape, f32_shape, bf16_shape, bf16_shape],
        scratch_shapes=[pltpu.VMEM((tt, 2 * N_HEADS * HEAD_W), BF16), pltpu.VMEM((2, PEER_N_KEYS, tt), F32)],
        compiler_params=_cparams(("parallel", "parallel")),
        name="peer_route",
    )(h2, wq_bf16, keys_bf16)


PEER_STEP_KEYS = 16
PEER_SUB_KEYS = 4


def _dense_body(h_ref, u_ref, vt_ref, n1_ref, e1_ref, r2_ref, e2_ref, x_ref, mod_ref, ln_ref, o_ref,
                act_scr, g_scr, w_scr, acc_scr, r2_scr, e2_scr, *, tt):
    k = pl.program_id(2)

    @pl.when(k == 0)
    def _():
        acc_scr[...] = jnp.zeros_like(acc_scr)
        r2_scr[...] = r2_ref[...]
        e2_scr[...] = e2_ref[...]

    hb = h_ref[0]
    sub = PEER_SUB_KEYS * PEER_N_KEYS

    def routing_weights(jj):
        for j in range(PEER_SUB_KEYS * jj, PEER_SUB_KEYS * (jj + 1)):
            for lc in range(tt // HEAD_W):
                ls = slice(lc * HEAD_W, (lc + 1) * HEAD_W)
                g = None
                for h in range(N_HEADS):
                    n_t = jnp.broadcast_to(n1_ref[h, j:j + 1, ls], (16, HEAD_W)).astype(BF16)[None]
                    e_t = jnp.broadcast_to(e1_ref[h, j:j + 1, ls], (16, HEAD_W)).astype(BF16)[None]
                    hit = r2_scr[h, :, ls].reshape(PEER_N_KEYS // 16, 16, HEAD_W) < n_t
                    term = jnp.where(hit, e2_scr[h, :, ls].reshape(PEER_N_KEYS // 16, 16, HEAD_W), 0.0) * e_t
                    g = term if g is None else g + term
                g_scr[j * PEER_N_KEYS:(j + 1) * PEER_N_KEYS, ls] = g.reshape(PEER_N_KEYS, HEAD_W)
        return lax.shift_right_logical(pltpu.bitcast(g[0], jnp.uint32), jnp.uint32(32))

    def after(x, zero_tile):
        xi = pltpu.bitcast(x, jnp.uint32)
        z = jnp.tile(zero_tile, (xi.shape[0] // zero_tile.shape[0], xi.shape[1] // zero_tile.shape[1]))
        return pltpu.bitcast(xi | z, x.dtype)

    nsub = PEER_STEP_KEYS // PEER_SUB_KEYS
    half = sub * nsub // 2
    act_scr[0:half, :] = _dot_nt(u_ref[0:half, :], hb)
    zero_tile = routing_weights(0)
    act_scr[half:, :] = _dot_nt(after(u_ref[half:, :], zero_tile), hb)
    for jj in range(1, nsub):
        routing_weights(jj)
    for jj in range(nsub):
        rows = slice(jj * sub, (jj + 1) * sub)
        a = act_scr[rows, :]
        gelu2 = a * (1.0 + lax.erf(a * (2.0 ** -0.5)))
        w_scr[rows, :] = gelu2.astype(BF16) * g_scr[rows, :]
        acc_scr[...] += _dot(vt_ref[0, :, rows], w_scr[rows, :])

    @pl.when(k == pl.num_programs(2) - 1)
    def _():
        y = DEEPNORM_ALPHA * x_ref[0] + mod_ref[0, 5:6, :] * acc_scr[...].T
        o_ref[0] = _layer_norm(y, ln_ref[0:1, :], ln_ref[1:2, :])


def _peer_dense(h2, x1, mod, u_bf16, vt_bf16, route, ln_gb, tt):
    B, T, _ = h2.shape
    n1, e1, r2, e2 = route
    per_b = mod.shape[0] > 1
    ne = PEER_STEP_KEYS * PEER_N_KEYS
    rows = pl.BlockSpec((None, N_HEADS, PEER_STEP_KEYS, tt), lambda b, i, k: (b, 0, k, i))
    full = pl.BlockSpec((None, N_HEADS, PEER_N_KEYS, tt), lambda b, i, k: (b, 0, 0, i))
    return pl.pallas_call(
        functools.partial(_dense_body, tt=tt),
        grid=(B, T // tt, PEER_N_KEYS // PEER_STEP_KEYS),
        in_specs=[pl.BlockSpec((1, tt, D_MODEL), lambda b, i, k: (b, i, 0)),
                  pl.BlockSpec((ne, D_MODEL), lambda b, i, k: (k, 0)),
                  pl.BlockSpec((1, D_MODEL, ne), lambda b, i, k: (k, 0, 0)),
                  rows, rows, full, full,
                  pl.BlockSpec((1, tt, D_MODEL), lambda b, i, k: (b, i, 0)),
                  pl.BlockSpec((1, 6, D_MODEL), (lambda b, i, k: (b, 0, 0)) if per_b else (lambda b, i, k: (0, 0, 0))),
                  pl.BlockSpec((2, D_MODEL), lambda b, i, k: (0, 0))],
        out_specs=pl.BlockSpec((1, tt, D_MODEL), lambda b, i, k: (b, i, 0)),
        out_shape=jax.ShapeDtypeStruct((B, T, D_MODEL), F32),
        scratch_shapes=[pltpu.VMEM((ne, tt), F32), pltpu.VMEM((ne, tt), BF16), pltpu.VMEM((ne, tt), BF16),
                        pltpu.VMEM((D_MODEL, tt), F32),
                        pltpu.VMEM((N_HEADS, PEER_N_KEYS, tt), BF16), pltpu.VMEM((N_HEADS, PEER_N_KEYS, tt), BF16)],
        compiler_params=_cparams(("parallel", "parallel", "arbitrary")),
        name="peer_dense",
    )(h2, u_bf16, vt_bf16, n1, e1, r2, e2, x1, mod, ln_gb)


def _trunk_layer(x, mod, layer, p, ctx, rope_tabs, consts, flat_rows):
    B, T, _ = x.shape
    if flat_rows is not None:
        z = _in_proj(x.reshape(-1, flat_rows, D_MODEL), mod, p["w_in"], None, tm=flat_rows)
        z = z.reshape(B, T, W_IN_COLS)
    else:
        z = _in_proj(x, mod, p["w_in"], rope_tabs, tm=PROJ_ROWS)
    if ctx is None:
        oatt = _attention(z, p["lam"], p["subln"], None, layer, p["att_scale"], tq=min(ATT_Q_ROWS, T), tk=T)
        o_f, o_b, s_fin = _hgrn(z, p["lb"], None, layer, consts)
    else:
        oatt = _attention(z, p["lam"], p["subln"], (ctx[0], ctx[1]), layer, p["att_scale"], tq=ATT_Q_ROWS,
                          tk=ATT_KEY_CHUNK)
        o_f, o_b, s_fin = _hgrn(z, p["lb"], ctx[2], layer, consts)
    x1, h2 = _merge(x, z, oatt, o_f, o_b, mod, p["w_branch"], p["w_out"], p["conv_w"], p["conv_b"],
                    p["rec_norm"], p["ln0"], tm=MERGE_ROWS)
    if flat_rows is not None:
        h2r, x1r = h2.reshape(-1, flat_rows, D_MODEL), x1.reshape(-1, flat_rows, D_MODEL)
    else:
        h2r, x1r = h2, x1
    route = _peer_route(h2r, p["peer_wq"], p["peer_keys"], PEER_TOKENS)
    x2 = _peer_dense(h2r, x1r, mod, p["peer_u"], p["peer_vt"], route, p["ln1"], PEER_TOKENS).reshape(B, T, D_MODEL)
    return x2, z, s_fin


def _layer_params(l, lb_all, w_in, attn_lambda, attn_subln, conv_w, conv_b, rec_norm, w_branch, w_out, ln_g, ln_b,
                  peer_wq, peer_keys, peer_u, peer_v):
    lam_init = 0.8 - 0.6 * math.exp(-0.3 * l)
    lp = attn_lambda[l].astype(F32)
    lam = jnp.exp(jnp.sum(lp[0] * lp[1])) - jnp.exp(jnp.sum(lp[2] * lp[3])) + lam_init
    return {
        "w_in": w_in[l].astype(BF16), "lam": lam.reshape(1), "att_scale": 1.0 - lam_init,
        "subln": attn_subln[l].reshape(1, HEAD_W), "lb": lb_all[l],
        "w_branch": w_branch[l].astype(BF16), "w_out": w_out[l].astype(BF16),
        "conv_w": conv_w[l], "conv_b": conv_b[l].reshape(1, D_MODEL),
        "rec_norm": jnp.tile(rec_norm[l], N_HEADS).reshape(1, D_MODEL),
        "ln0": jnp.stack([ln_g[l, 0], ln_b[l, 0]]), "ln1": jnp.stack([ln_g[l, 1], ln_b[l, 1]]),
        "peer_wq": peer_wq[l].astype(BF16),
        "peer_keys": peer_keys[l].astype(BF16).reshape(2 * N_HEADS, PEER_N_KEYS, HEAD_W),
        "peer_u": peer_u[l].astype(BF16),
        "peer_vt": peer_v[l].astype(BF16).reshape(-1, PEER_STEP_KEYS * PEER_N_KEYS, D_MODEL).transpose(0, 2, 1),
    }


def kernel(x_prompt, x_sample, c, cache_attn_k, cache_attn_v, state_hgrn, c_ctx, mod_w, mod_b, w_in, attn_lambda,
           attn_subln, conv_w, conv_b, rec_lb, rec_norm, w_branch, w_out, ln_g, ln_b, peer_wq, peer_keys, peer_u,
           peer_v):
    B, T, _ = x_prompt.shape
    Bs, Ts, _ = x_sample.shape
    P = cache_attn_k.shape[2]
    lb_all = jnp.cumsum(jax.nn.softmax(rec_lb.astype(F32), axis=0), axis=0)
    lb_all = lb_all - lb_all[:1]
    cmat = jnp.concatenate([c_ctx[None, :], c, jnp.zeros((8 - 1 - Bs, D_MODEL), F32)], axis=0)
    mods = _mod_vectors(cmat, mod_w, mod_b).reshape(DEPTH, 8, 6, D_MODEL)
    rope_tabs = _rope_tables(Ts)
    consts = _rec_constants()
    ck = cache_attn_k.reshape(Bs, DEPTH, P, N_HEADS * HEAD_W)
    cv = cache_attn_v.reshape(Bs, DEPTH, P, N_HEADS * HEAD_W)

    y_p, y_s = x_prompt, x_sample
    ks, vs, ss = [], [], []
    for l in range(DEPTH):
        p = _layer_params(l, lb_all, w_in, attn_lambda, attn_subln, conv_w, conv_b, rec_norm, w_branch, w_out,
                          ln_g, ln_b, peer_wq, peer_keys, peer_u, peer_v)
        y_p, z_p, s_p = _trunk_layer(y_p, mods[l, 0:1], l, p, None, None, consts, flat_rows=PROJ_ROWS)
        ks.append(z_p[..., COL_K * D_MODEL:(COL_K + 1) * D_MODEL].reshape(B, T, N_HEADS, HEAD_W))
        vs.append(z_p[..., COL_V * D_MODEL:(COL_V + 1) * D_MODEL].reshape(B, T, N_HEADS, HEAD_W))
        ss.append(s_p)
        y_s, _, _ = _trunk_layer(y_s, mods[l, 1:1 + Bs], l, p, (ck, cv, state_hgrn), rope_tabs, consts, flat_rows=None)
    return (y_p, y_s, jnp.stack(ks, axis=1), jnp.stack(vs, axis=1), jnp.stack(ss, axis=1))
```

```python
import functools
import math

import numpy as np
import jax
import jax.numpy as jnp
from jax import lax
from jax.experimental import pallas as pl
from jax.experimental.pallas import tpu as pltpu

F32 = jnp.float32
BF16 = jnp.bfloat16

D_MODEL = 1024
DEPTH = 2
GRID_W = 64
N_HEADS = 8
HEAD_W = 128
ATT_QK_DIM = 64
ROPE_BASE = 10000.0
ROPE_AXIS_DIM = ATT_QK_DIM // 2
PEER_N_KEYS = 128
PEER_TOPK = 16
LN_EPS = 1e-5
DEEPNORM_ALPHA = (2 * DEPTH) ** 0.25
REF_COL_ORDER = ("q", "k", "v", "cb", "cc", "cx", "rq", "rff", "rfb", "ri", "rg", "g0", "g1", "g2")
COL_Q, COL_K, COL_V, COL_RQ, COL_RFF, COL_RFB, COL_RI = range(7)
WIDE_COLS = ("q", "k", "v", "rq", "rff", "rfb", "ri")
COL_CB, COL_CC, COL_CX, COL_RG, COL_G = range(5)
NARROW_COLS = ("cb", "cc", "cx", "rg", "g0", "g1", "g2")
N_WIDE = len(WIDE_COLS) * D_MODEL
N_NARROW = len(NARROW_COLS) * D_MODEL

VMEM_LIMIT = 56 * 1024 * 1024
PROJ_ROWS = 2048
PROJ_COLS = 512
ATT_Q_ROWS = 1024
ATT_KEY_CHUNK = 512
MERGE_ROWS = 256
HALO_ROWS = 16
PEER_TOKENS = 512
REC_C = 128
REC_LEVELS = 7
REC_HEADS_PER_STEP = 8


def _cparams(sem):
    return pltpu.CompilerParams(dimension_semantics=sem, vmem_limit_bytes=VMEM_LIMIT)


def _dot(a, b):
    return jnp.dot(a, b, preferred_element_type=F32)


def _dot_nt(a, b):
    return lax.dot_general(a, b, (((1,), (1,)), ((), ())), preferred_element_type=F32)


def _sigmoid(x):
    e = jnp.exp(-jnp.abs(x))
    r = 1.0 / (1.0 + e)
    return jnp.where(x >= 0, r, e * r)


def _layer_norm(y, g, b):
    mu = jnp.mean(y, axis=-1, keepdims=True)
    yc = y - mu
    var = jnp.mean(yc * yc, axis=-1, keepdims=True)
    return yc * lax.rsqrt(var + LN_EPS) * g + b


def _mod_body(c_ref, w_ref, b_ref, o_ref):
    c = c_ref[...]
    s = (c * _sigmoid(c)).astype(BF16)
    o_ref[0] = _dot(s, w_ref[0].astype(BF16)) + b_ref[0]


def _mod_vectors(cmat, mod_w, mod_b):
    tn = 1536
    return pl.pallas_call(
        _mod_body,
        grid=(DEPTH, 6 * D_MODEL // tn),
        in_specs=[pl.BlockSpec((8, D_MODEL), lambda l, j: (0, 0)),
                  pl.BlockSpec((1, D_MODEL, tn), lambda l, j: (l, 0, j)),
                  pl.BlockSpec((1, 1, tn), lambda l, j: (l, 0, j))],
        out_specs=pl.BlockSpec((1, 8, tn), lambda l, j: (l, 0, j)),
        out_shape=jax.ShapeDtypeStruct((DEPTH, 8, 6 * D_MODEL), F32),
        compiler_params=_cparams(("parallel", "parallel")),
        name="mod_vectors",
    )(cmat, mod_w, mod_b.reshape(DEPTH, 1, 6 * D_MODEL))


def _in_proj_body(x_ref, mod_ref, w_ref, *refs, rope, tm):
    if rope:
        cos_ref, sin_ref, wide_ref, narrow_ref, h_ref = refs
    else:
        wide_ref, narrow_ref, h_ref = refs
    j = pl.program_id(2)
    wide_tiles = N_WIDE // PROJ_COLS
    rope_tiles = COL_V * D_MODEL // PROJ_COLS if rope else 0

    @pl.when(j == 0)
    def _():
        h_ref[...] = (x_ref[0] * (1.0 + mod_ref[0, 1:2, :]) + mod_ref[0, 0:1, :]).astype(BF16)

    z = _dot(h_ref[...], w_ref[...])

    if rope:
        @pl.when(j < rope_tiles)
        def _():
            cos = cos_ref[...]
            sin = sin_ref[...]
            lane = lax.broadcasted_iota(jnp.int32, (tm, HEAD_W), 1)
            first = (lane % ROPE_AXIS_DIM) < (ROPE_AXIS_DIM // 2)
            for g in range(PROJ_COLS // HEAD_W):
                zg = z[:, g * HEAD_W:(g + 1) * HEAD_W]
                partner = jnp.where(first, pltpu.roll(zg, HEAD_W - ROPE_AXIS_DIM // 2, 1),
                                    pltpu.roll(zg, ROPE_AXIS_DIM // 2, 1))
                wide_ref[0, :, g * HEAD_W:(g + 1) * HEAD_W] = zg * cos + partner * sin

    @pl.when((j >= rope_tiles) & (j < wide_tiles))
    def _():
        wide_ref[0] = z

    @pl.when(j >= wide_tiles)
    def _():
        narrow_ref[0] = z.astype(BF16)


def _in_proj(x, mod, w_bf16, rope_tabs, tm):
    B, T, _ = x.shape
    wide_tiles = N_WIDE // PROJ_COLS
    per_b = mod.shape[0] > 1
    rope = rope_tabs is not None
    in_specs = [pl.BlockSpec((1, tm, D_MODEL), lambda b, i, j: (b, i, 0)),
                pl.BlockSpec((1, 6, D_MODEL), (lambda b, i, j: (b, 0, 0)) if per_b else (lambda b, i, j: (0, 0, 0))),
                pl.BlockSpec((D_MODEL, PROJ_COLS), lambda b, i, j: (0, j))]
    args = [x, mod, w_bf16]
    if rope:
        in_specs += [pl.BlockSpec((tm, HEAD_W), lambda b, i, j: (i, 0))] * 2
        args += list(rope_tabs)
    return pl.pallas_call(
        functools.partial(_in_proj_body, rope=rope, tm=tm),
        grid=(B, T // tm, (N_WIDE + N_NARROW) // PROJ_COLS),
        in_specs=in_specs,
        out_specs=[pl.BlockSpec((1, tm, PROJ_COLS), lambda b, i, j: (b, i, jnp.minimum(j, wide_tiles - 1))),
                   pl.BlockSpec((1, tm, PROJ_COLS), lambda b, i, j: (b, i, jnp.maximum(j - wide_tiles, 0)))],
        out_shape=[jax.ShapeDtypeStruct((B, T, N_WIDE), F32), jax.ShapeDtypeStruct((B, T, N_NARROW), BF16)],
        scratch_shapes=[pltpu.VMEM((tm, D_MODEL), BF16)],
        compiler_params=_cparams(("parallel", "parallel", "arbitrary")),
        name="in_proj_rope" if rope else "in_proj",
    )(*args)


def _rope_tables(T):
    t = np.arange(T)
    pos = np.stack([t // GRID_W, t % GRID_W], axis=1).astype(np.float32)
    lane = np.arange(HEAD_W)
    axis = (lane % ATT_QK_DIM) // ROPE_AXIS_DIM
    r = lane % ROPE_AXIS_DIM
    half = ROPE_AXIS_DIM // 2
    freqs = ROPE_BASE ** (-jnp.arange(0, ROPE_AXIS_DIM, 2, dtype=F32) / ROPE_AXIS_DIM)
    ang = jnp.asarray(pos)[:, axis] * freqs[r % half][None, :]
    sign = jnp.asarray(np.where(r < half, -1.0, 1.0).astype(np.float32))[None, :]
    return jnp.cos(ang), jnp.sin(ang) * sign


def _attn_body(lam_ref, q_ref, k_ref, v_ref, *refs, tq, tk, n_self, n_ctx, out_scale):
    if n_ctx:
        kc_ref, vc_ref, g_ref, o_ref, kb_ref, vb_ref = refs
    else:
        g_ref, o_ref, kb_ref, vb_ref = refs
    t_self = n_self * tk

    @pl.when(pl.program_id(2) == 0)
    def _():
        kb_ref[0:t_self, :] = k_ref[0].astype(BF16)
        vb_ref[0:t_self, :] = v_ref[0].astype(BF16)
        if n_ctx:
            kb_ref[t_self:, :] = kc_ref[0, 0].astype(BF16)
            vb_ref[t_self:, :] = vc_ref[0, 0].astype(BF16)

    q = q_ref[0] * (ATT_QK_DIM ** -0.5 * math.log2(math.e))
    lane = lax.broadcasted_iota(jnp.int32, (tq, HEAD_W), 1)
    lo = lane < ATT_QK_DIM
    qs = jnp.concatenate([jnp.where(lo, q, 0.0), jnp.where(lo, 0.0, q)], axis=0).astype(BF16)

    m = jnp.full((2 * tq, 1), -jnp.inf, F32)
    l = jnp.zeros((2 * tq, 1), F32)
    acc = jnp.zeros((2 * tq, HEAD_W), F32)
    for c in range(n_self + n_ctx):
        s = _dot_nt(qs, kb_ref[c * tk:(c + 1) * tk, :])
        m_new = jnp.maximum(m, jnp.max(s, axis=-1, keepdims=True))
        a = jnp.exp2(m - m_new)
        p = jnp.exp2(s - m_new)
        l = a * l + jnp.sum(p, axis=-1, keepdims=True)
        acc = a * acc + _dot(p.astype(BF16), vb_ref[c * tk:(c + 1) * tk, :])
        m = m_new
    o = acc / l
    o = o[:tq] - lam_ref[0] * o[tq:]
    o = o * lax.rsqrt(jnp.mean(o * o, axis=-1, keepdims=True) + LN_EPS) * g_ref[...] * out_scale
    o_ref[0] = o.astype(BF16)


def _attention(z, lam, subln, ctx_kv, layer, out_scale, tq, tk):
    B, T, _ = z.shape
    n_self = T // tk
    in_specs = [pl.BlockSpec(memory_space=pltpu.SMEM),
                pl.BlockSpec((1, tq, HEAD_W), lambda b, h, i: (b, i, COL_Q * N_HEADS + h)),
                pl.BlockSpec((1, T, HEAD_W), lambda b, h, i: (b, 0, COL_K * N_HEADS + h)),
                pl.BlockSpec((1, T, HEAD_W), lambda b, h, i: (b, 0, COL_V * N_HEADS + h))]
    args = [lam, z, z, z]
    t_all = T
    n_ctx = 0
    if ctx_kv is not None:
        P = ctx_kv[0].shape[2]
        assert P == tk
        n_ctx = 1
        t_all = T + P
        in_specs += [pl.BlockSpec((1, 1, P, HEAD_W), lambda b, h, i: (b, layer, 0, h))] * 2
        args += list(ctx_kv)
    in_specs.append(pl.BlockSpec((1, HEAD_W), lambda b, h, i: (0, 0)))
    args.append(subln)
    return pl.pallas_call(
        functools.partial(_attn_body, tq=tq, tk=tk, n_self=n_self, n_ctx=n_ctx, out_scale=out_scale),
        grid=(B, N_HEADS, T // tq),
        in_specs=in_specs,
        out_specs=pl.BlockSpec((1, tq, HEAD_W), lambda b, h, i: (b, i, h)),
        out_shape=jax.ShapeDtypeStruct((B, T, D_MODEL), BF16),
        scratch_shapes=[pltpu.VMEM((t_all, HEAD_W), BF16), pltpu.VMEM((t_all, HEAD_W), BF16)],
        compiler_params=_cparams(("parallel", "parallel", "arbitrary")),
        name="attention_ctx" if n_ctx else "attention",
    )(*args)


def _rec_constants():
    C = REC_C
    t = np.arange(C)[:, None]
    j = np.arange(C)[None, :]
    blocks = [(j <= t), (j > t)]
    for l in range(REC_LEVELS):
        m = 1 << l
        seg0 = (t // m) * m
        odd = ((t // m) % 2) == 1
        blocks.append(np.where(odd, (j >= seg0) & (j <= t), (j > t) & (j <= seg0 + m - 1)))
    mf = np.concatenate(blocks, axis=0).astype(np.float32)
    x = t ^ j
    lv = np.where(x == 0, REC_LEVELS, np.floor(np.log2(np.maximum(x, 1))).astype(np.int64))
    lvf = np.where(j <= t, lv, REC_LEVELS + 1).astype(np.int32)
    mb = mf.reshape(-1, C, C)[:, ::-1, ::-1].reshape(-1, C)
    lvb = lvf[::-1, ::-1]
    mf, mb = np.concatenate([mf, mf], axis=1), np.concatenate([mb, mb], axis=1)
    return (jnp.asarray(mf, BF16), jnp.asarray(np.ascontiguousarray(mb), BF16),
            jnp.asarray(lvf), jnp.asarray(np.ascontiguousarray(lvb)))


def _rec_pair(rq, ri, fx, lb, m_ref, lv, sts, backward):
    C = REC_C
    e = jnp.exp(-jnp.abs(fx))
    r = 1.0 / (1.0 + e)
    pos = fx >= 0
    sig = jnp.where(pos, r, e * r)
    nsig = jnp.where(pos, e * r, r)
    logf = jnp.log(lb + (1.0 - lb) * sig)
    kk = (1.0 - lb) * nsig
    q = rq * _sigmoid(rq)
    hi = logf.astype(BF16)
    mid = (logf - hi.astype(F32)).astype(BF16)
    lf2 = jnp.concatenate([hi, mid], axis=0)

    def expo(blk):
        return _dot(m_ref[blk * C:(blk + 1) * C, :], lf2)

    row = lax.broadcasted_iota(jnp.int32, (C, 2 * HEAD_W), 0)
    if backward:
        row = (C - 1) - row
    qb = q.astype(BF16)
    kb = kk.astype(BF16)
    ws = []
    for l in range(REC_LEVELS):
        odd = ((row >> l) & 1) == 1
        ws.append((jnp.exp(expo(2 + l)) * jnp.where(odd, q, kk)).astype(BF16))
    b_incl = expo(0)
    qi = (q * jnp.exp(b_incl)).astype(BF16)
    ki = (kk * jnp.exp(expo(1))).astype(BF16)
    dec = jnp.exp(b_incl[0:1, :] if backward else b_incl[C - 1:C, :])
    outs, new_sts = [], []
    for g in range(2):
        ls = slice(g * HEAD_W, (g + 1) * HEAD_W)
        a = jnp.where(lv == REC_LEVELS, _dot_nt(qb[:, ls], kb[:, ls]), 0.0)
        for l in range(REC_LEVELS):
            a = jnp.where(lv == l, _dot_nt(ws[l][:, ls], ws[l][:, ls]), a)
        rig = ri[:, ls]
        outs.append(_dot(a.astype(BF16), rig.astype(BF16)) + _dot_nt(qi[:, ls], sts[g].astype(BF16)))
        new_sts.append(sts[g] * dec[:, ls] + _dot(rig.T.astype(BF16), ki[:, ls]))
    return jnp.concatenate(outs, axis=1), new_sts


def _hgrn_body(mf_ref, mb_ref, lvf_ref, lvb_ref, rqf_ref, rif_ref, ff_ref, rqb_ref, rib_ref, fb_ref, lb_ref,
               *refs, has_s0):
    if has_s0:
        s0_ref, of_ref, ob_ref, so_ref, sf_scr, sb_scr = refs
    else:
        of_ref, ob_ref, so_ref, sf_scr, sb_scr = refs
    c = pl.program_id(2)

    @pl.when(c == 0)
    def _():
        for g in range(REC_HEADS_PER_STEP):
            if has_s0:
                sf_scr[g] = s0_ref[0, 0, g].T
                sb_scr[g] = s0_ref[0, 1, g].T
            else:
                sf_scr[g] = jnp.zeros((HEAD_W, HEAD_W), F32)
                sb_scr[g] = jnp.zeros((HEAD_W, HEAD_W), F32)

    for g in range(0, REC_HEADS_PER_STEP, 2):
        ls = slice(g * HEAD_W, (g + 2) * HEAD_W)
        o, st = _rec_pair(rqf_ref[0, :, ls], rif_ref[0, :, ls], ff_ref[0, :, ls], lb_ref[0:1, ls], mf_ref,
                          lvf_ref[...], [sf_scr[g], sf_scr[g + 1]], False)
        of_ref[0, :, ls] = o
        sf_scr[g] = st[0]
        sf_scr[g + 1] = st[1]
        o, st = _rec_pair(rqb_ref[0, :, ls], rib_ref[0, :, ls], fb_ref[0, :, ls], lb_ref[1:2, ls], mb_ref,
                          lvb_ref[...], [sb_scr[g], sb_scr[g + 1]], True)
        ob_ref[0, :, ls] = o
        sb_scr[g] = st[0]
        sb_scr[g + 1] = st[1]

    @pl.when(c == pl.num_programs(2) - 1)
    def _():
        for g in range(REC_HEADS_PER_STEP):
            so_ref[0, 0, g] = sf_scr[g].T
            so_ref[0, 1, g] = sb_scr[g].T


def _hgrn(z, lb, s0, layer, consts):
    B, T, _ = z.shape
    n = T // REC_C
    C = REC_C
    G = REC_HEADS_PER_STEP
    ng = N_HEADS // G
    W = G * HEAD_W

    def fwd(col):
        return pl.BlockSpec((1, C, W), lambda b, h, c: (b, c, col * ng + h))

    def bwd(col):
        return pl.BlockSpec((1, C, W), lambda b, h, c: (b, n - 1 - c, col * ng + h))

    const2 = lambda b, h, c: (0, 0)
    in_specs = [pl.BlockSpec(((2 + REC_LEVELS) * C, 2 * C), const2), pl.BlockSpec(((2 + REC_LEVELS) * C, 2 * C), const2),
                pl.BlockSpec((C, C), const2), pl.BlockSpec((C, C), const2),
                fwd(COL_RQ), fwd(COL_RI), fwd(COL_RFF), bwd(COL_RQ), bwd(COL_RI), bwd(COL_RFB),
                pl.BlockSpec((2, W), lambda b, h, c: (0, h))]
    args = list(consts) + [z] * 6 + [lb]
    if s0 is not None:
        s0v = s0.reshape(B, DEPTH * 2, N_HEADS, HEAD_W, HEAD_W)
        in_specs.append(pl.BlockSpec((1, 2, G, HEAD_W, HEAD_W), lambda b, h, c: (b, layer, h, 0, 0)))
        args.append(s0v)
    return pl.pallas_call(
        functools.partial(_hgrn_body, has_s0=s0 is not None),
        grid=(B, ng, n),
        in_specs=in_specs,
        out_specs=[pl.BlockSpec((1, C, W), lambda b, h, c: (b, c, h)),
                   pl.BlockSpec((1, C, W), lambda b, h, c: (b, n - 1 - c, h)),
                   pl.BlockSpec((1, 2, G, HEAD_W, HEAD_W), lambda b, h, c: (b, 0, h, 0, 0))],
        out_shape=[jax.ShapeDtypeStruct((B, T, D_MODEL), F32), jax.ShapeDtypeStruct((B, T, D_MODEL), F32),
                   jax.ShapeDtypeStruct((B, 2, N_HEADS, HEAD_W, HEAD_W), F32)],
        scratch_shapes=[pltpu.VMEM((G, HEAD_W, HEAD_W), F32), pltpu.VMEM((G, HEAD_W, HEAD_W), F32)],
        compiler_params=_cparams(("parallel", "parallel", "arbitrary")),
        name="hgrn_ctx" if s0 is not None else "hgrn",
    )(*args)


def _merge_body(oatt_ref, zb_ref, zc_ref, zx_ref, zcp_ref, zxp_ref, zcn_ref, zxn_ref, of_ref, ob_ref, rg_ref,
                g0_ref, g1_ref, g2_ref, x_ref, mod_ref, wb_ref, wo_ref, cw_ref, cb_ref, rn_ref, ln_ref,
                x1_ref, h2_ref, *, tm):
    i = pl.program_id(1)
    u = zc_ref[0].astype(F32) * zx_ref[0].astype(F32)
    row = lax.broadcasted_iota(jnp.int32, (tm, D_MODEL), 0)
    prev_ok = (i > 0).astype(F32)
    next_ok = (i < pl.num_programs(1) - 1).astype(F32)
    halo_p = zcp_ref[0].astype(F32) * zxp_ref[0].astype(F32)
    halo_n = zcn_ref[0].astype(F32) * zxn_ref[0].astype(F32)
    u_prev_edge = halo_p[HALO_ROWS - 1:HALO_ROWS, :] * prev_ok
    u_next_edge = halo_n[0:1, :] * next_ok
    up = jnp.where(row == 0, u_prev_edge, pltpu.roll(u, 1, 0))
    un = jnp.where(row == tm - 1, u_next_edge, pltpu.roll(u, tm - 1, 0))
    conv = up * cw_ref[0:1, :] + u * cw_ref[1:2, :] + un * cw_ref[2:3, :] + cb_ref[...]
    o_conv = (zb_ref[0].astype(F32) * conv).astype(BF16)
    s = of_ref[0] + ob_ref[0]
    rg = rg_ref[0].astype(F32)
    parts = []
    for h in range(N_HEADS):
        sh = s[:, h * HEAD_W:(h + 1) * HEAD_W]
        parts.append(sh * lax.rsqrt(jnp.mean(sh * sh, axis=-1, keepdims=True) + LN_EPS))
    o_rec = (jnp.concatenate(parts, axis=1) * rn_ref[...] * (rg * _sigmoid(rg))).astype(BF16)
    merged = (_sigmoid(g0_ref[0].astype(F32)) * _dot(oatt_ref[0], wb_ref[0])
              + _sigmoid(g1_ref[0].astype(F32)) * _dot(o_conv, wb_ref[1])
              + _sigmoid(g2_ref[0].astype(F32)) * _dot(o_rec, wb_ref[2]))
    mix = _dot(merged.astype(BF16), wo_ref[...])
    y = DEEPNORM_ALPHA * x_ref[0] + mod_ref[0, 2:3, :] * mix
    x1 = _layer_norm(y, ln_ref[0:1, :], ln_ref[1:2, :])
    x1_ref[0] = x1
    h2_ref[0] = (x1 * (1.0 + mod_ref[0, 4:5, :]) + mod_ref[0, 3:4, :]).astype(BF16)


def _merge(x, z, oatt, o_f, o_b, mod, wb_bf16, wo_bf16, conv_w, conv_b, rec_norm_t, ln_gb, tm):
    B, T, _ = x.shape
    per_b = mod.shape[0] > 1
    nb8 = tm // HALO_ROWS
    last8 = T // HALO_ROWS - 1

    def col(c):
        return pl.BlockSpec((1, tm, D_MODEL), lambda b, i: (b, i, c))

    def prev8(c):
        return pl.BlockSpec((1, HALO_ROWS, D_MODEL), lambda b, i: (b, jnp.maximum(i * nb8 - 1, 0), c))

    def next8(c):
        return pl.BlockSpec((1, HALO_ROWS, D_MODEL), lambda b, i: (b, jnp.minimum((i + 1) * nb8, last8), c))

    tile = pl.BlockSpec((1, tm, D_MODEL), lambda b, i: (b, i, 0))
    full2 = lambda b, i: (0, 0)
    in_specs = [tile, col(COL_CB), col(COL_CC), col(COL_CX), prev8(COL_CC), prev8(COL_CX), next8(COL_CC), next8(COL_CX),
                tile, tile, col(COL_RG), col(COL_G), col(COL_G + 1), col(COL_G + 2), tile,
                pl.BlockSpec((1, 6, D_MODEL), (lambda b, i: (b, 0, 0)) if per_b else (lambda b, i: (0, 0, 0))),
                pl.BlockSpec((3, D_MODEL, D_MODEL), lambda b, i: (0, 0, 0)),
                pl.BlockSpec((D_MODEL, D_MODEL), full2),
                pl.BlockSpec((3, D_MODEL), full2), pl.BlockSpec((1, D_MODEL), full2),
                pl.BlockSpec((1, D_MODEL), full2), pl.BlockSpec((2, D_MODEL), full2)]
    return pl.pallas_call(
        functools.partial(_merge_body, tm=tm),
        grid=(B, T // tm),
        in_specs=in_specs,
        out_specs=[tile, tile],
        out_shape=[jax.ShapeDtypeStruct((B, T, D_MODEL), F32), jax.ShapeDtypeStruct((B, T, D_MODEL), BF16)],
        compiler_params=_cparams(("parallel", "arbitrary")),
        name="merge",
    )(oatt, z, z, z, z, z, z, z, o_f, o_b, z, z, z, z, x, mod, wb_bf16, wo_bf16, conv_w, conv_b, rec_norm_t, ln_gb)


def _top_rows(s, k, one_per_round, want_rank=True):
    n, w = s.shape
    s_in = s
    track = one_per_round or want_rank
    rid = lax.broadcasted_iota(jnp.int32, (n, w), 0).astype(F32)
    rank = jnp.full((n, w), float(k), F32) if track else None
    vals = []
    for r in range(k):
        m = jnp.max(s, axis=0, keepdims=True)
        hit = s == m
        if one_per_round:
            first = jnp.min(jnp.where(hit, rid, float(n)), axis=0, keepdims=True)
            hit = rid == first
        s = jnp.where(hit, -jnp.inf, s)
        if track:
            rank = jnp.where(hit, float(r), rank)
        vals.append(m)
    taken = (rank < k) if track else (s_in >= vals[-1])
    excess = jnp.sum(jnp.where(taken, 1.0, 0.0), axis=0, keepdims=True) - k
    return vals, rank, excess


def _route_body(h_ref, wq_ref, keys_ref, n1_ref, e1_ref, r2_ref, e2_ref, q_scr, s_scr, *, tt):
    q_scr[...] = _dot(h_ref[0], wq_ref[...]).astype(BF16)

    def head(h, carry):
        for p in range(2):
            off = pl.multiple_of((2 * h + p) * HEAD_W, HEAD_W)
            s_scr[p] = _dot_nt(keys_ref[2 * h + p], q_scr[:, pl.ds(off, HEAD_W)])
        excess = route_head(h, False)

        @pl.when(jnp.max(excess) > 0.0)
        def _():
            route_head(h, True)

        return carry

    def route_head(h, one_per_round):
        excess = jnp.zeros((1, HEAD_W), F32)
        for lc in range(tt // HEAD_W):
            ls = slice(lc * HEAD_W, (lc + 1) * HEAD_W)
            s1 = s_scr[0, :, ls]
            s2 = s_scr[1, :, ls]
            v1, r1, x1 = _top_rows(s1, PEER_TOPK, one_per_round, want_rank=False)
            v2, r2, x2 = _top_rows(s2, PEER_TOPK, one_per_round)
            v2a = jnp.concatenate(v2, axis=0)
            cands = [v1[0] + v2a] + [v1[a] + v2a[:8] for a in range(1, 8)] + [jnp.concatenate(v1[8:], axis=0) + v2[0]]
            cand = jnp.concatenate(cands, axis=0)
            vc, rc, xc = _top_rows(cand, PEER_TOPK, one_per_round, want_rank=False)
            excess = jnp.maximum(excess, jnp.maximum(jnp.maximum(x1, x2), xc))
            sel = (rc < PEER_TOPK) if one_per_round else (cand >= vc[-1])
            zsum = jnp.sum(jnp.where(sel, jnp.exp(cand - (v1[0] + v2[0])), 0.0), axis=0, keepdims=True)
            selc = jnp.where(sel, 1.0, 0.0)
            n_a = [jnp.sum(selc[0:16], axis=0, keepdims=True)]
            n_a += [jnp.sum(selc[8 + 8 * a:16 + 8 * a], axis=0, keepdims=True) for a in range(1, 8)]
            n_a += [selc[72 + a:73 + a] for a in range(8)]
            n1 = jnp.zeros_like(s1)
            for a in range(PEER_TOPK):
                n1 = jnp.where((r1 == a) if one_per_round else (s1 == v1[a]), n_a[a], n1)
            n1_ref[h, :, ls] = n1
            e1_ref[h, :, ls] = jnp.exp(s1 - v1[0]) * (0.5 / zsum)
            r2_ref[h, :, ls] = r2.astype(BF16)
            e2_ref[h, :, ls] = jnp.exp(s2 - v2[0]).astype(BF16)
        return excess

    lax.fori_loop(0, N_HEADS, head, 0)


def _peer_route(h2, wq_bf16, keys_bf16, tt):
    B, T, _ = h2.shape
    big = pl.BlockSpec((None, N_HEADS, PEER_N_KEYS, tt), lambda b, i: (b, 0, 0, i))
    f32_shape = jax.ShapeDtypeStruct((B, N_HEADS, PEER_N_KEYS, T), F32)
    bf16_shape = jax.ShapeDtypeStruct((B, N_HEADS, PEER_N_KEYS, T), BF16)
    return pl.pallas_call(
        functools.partial(_route_body, tt=tt),
        grid=(B, T // tt),
        in_specs=[pl.BlockSpec((1, tt, D_MODEL), lambda b, i: (b, i, 0)),
                  pl.BlockSpec((D_MODEL, 2 * N_HEADS * HEAD_W), lambda b, i: (0, 0)),
                  pl.BlockSpec((2 * N_HEADS, PEER_N_KEYS, HEAD_W), lambda b, i: (0, 0, 0))],
        out_specs=[big, big, big, big],
        out_shape=[f32_shape, f32_shape, bf16_shape, bf16_shape],
        scratch_shapes=[pltpu.VMEM((tt, 2 * N_HEADS * HEAD_W), BF16), pltpu.VMEM((2, PEER_N_KEYS, tt), F32)],
        compiler_params=_cparams(("parallel", "parallel")),
        name="peer_route",
    )(h2, wq_bf16, keys_bf16)


PEER_STEP_KEYS = 16
PEER_SUB_KEYS = 4


def _dense_body(h_ref, u_ref, vt_ref, n1_ref, e1_ref, r2_ref, e2_ref, x_ref, mod_ref, ln_ref, o_ref,
                act_scr, g_scr, w_scr, acc_scr, r2_scr, e2_scr, *, tt):
    k = pl.program_id(2)

    @pl.when(k == 0)
    def _():
        acc_scr[...] = jnp.zeros_like(acc_scr)
        r2_scr[...] = r2_ref[...]
        e2_scr[...] = e2_ref[...]

    hb = h_ref[0]
    sub = PEER_SUB_KEYS * PEER_N_KEYS

    def routing_weights(jj):
        for j in range(PEER_SUB_KEYS * jj, PEER_SUB_KEYS * (jj + 1)):
            for lc in range(tt // HEAD_W):
                ls = slice(lc * HEAD_W, (lc + 1) * HEAD_W)
                g = None
                for h in range(N_HEADS):
                    n_t = jnp.broadcast_to(n1_ref[h, j:j + 1, ls], (16, HEAD_W)).astype(BF16)[None]
                    e_t = jnp.broadcast_to(e1_ref[h, j:j + 1, ls], (16, HEAD_W)).astype(BF16)[None]
                    hit = r2_scr[h, :, ls].reshape(PEER_N_KEYS // 16, 16, HEAD_W) < n_t
                    term = jnp.where(hit, e2_scr[h, :, ls].reshape(PEER_N_KEYS // 16, 16, HEAD_W), 0.0) * e_t
                    g = term if g is None else g + term
                g_scr[j * PEER_N_KEYS:(j + 1) * PEER_N_KEYS, ls] = g.reshape(PEER_N_KEYS, HEAD_W)
        return lax.shift_right_logical(pltpu.bitcast(g[0], jnp.uint32), jnp.uint32(32))

    def after(x, zero_tile):
        xi = pltpu.bitcast(x, jnp.uint32)
        z = jnp.tile(zero_tile, (xi.shape[0] // zero_tile.shape[0], xi.shape[1] // zero_tile.shape[1]))
        return pltpu.bitcast(xi | z, x.dtype)

    nsub = PEER_STEP_KEYS // PEER_SUB_KEYS
    half = sub * nsub // 2
    act_scr[0:half, :] = _dot_nt(u_ref[0:half, :], hb)
    zero_tile = routing_weights(0)
    act_scr[half:, :] = _dot_nt(after(u_ref[half:, :], zero_tile), hb)
    for jj in range(1, nsub):
        routing_weights(jj)
    for jj in range(nsub):
        rows = slice(jj * sub, (jj + 1) * sub)
        a = act_scr[rows, :]
        gelu2 = a * (1.0 + lax.erf(a * (2.0 ** -0.5)))
        w_scr[rows, :] = gelu2.astype(BF16) * g_scr[rows, :]
        acc_scr[...] += _dot(vt_ref[0, :, rows], w_scr[rows, :])

    @pl.when(k == pl.num_programs(2) - 1)
    def _():
        y = DEEPNORM_ALPHA * x_ref[0] + mod_ref[0, 5:6, :] * acc_scr[...].T
        o_ref[0] = _layer_norm(y, ln_ref[0:1, :], ln_ref[1:2, :])


def _peer_dense(h2, x1, mod, u_bf16, vt_bf16, route, ln_gb, tt):
    B, T, _ = h2.shape
    n1, e1, r2, e2 = route
    per_b = mod.shape[0] > 1
    ne = PEER_STEP_KEYS * PEER_N_KEYS
    rows = pl.BlockSpec((None, N_HEADS, PEER_STEP_KEYS, tt), lambda b, i, k: (b, 0, k, i))
    full = pl.BlockSpec((None, N_HEADS, PEER_N_KEYS, tt), lambda b, i, k: (b, 0, 0, i))
    return pl.pallas_call(
        functools.partial(_dense_body, tt=tt),
        grid=(B, T // tt, PEER_N_KEYS // PEER_STEP_KEYS),
        in_specs=[pl.BlockSpec((1, tt, D_MODEL), lambda b, i, k: (b, i, 0)),
                  pl.BlockSpec((ne, D_MODEL), lambda b, i, k: (k, 0)),
                  pl.BlockSpec((1, D_MODEL, ne), lambda b, i, k: (k, 0, 0)),
                  rows, rows, full, full,
                  pl.BlockSpec((1, tt, D_MODEL), lambda b, i, k: (b, i, 0)),
                  pl.BlockSpec((1, 6, D_MODEL), (lambda b, i, k: (b, 0, 0)) if per_b else (lambda b, i, k: (0, 0, 0))),
                  pl.BlockSpec((2, D_MODEL), lambda b, i, k: (0, 0))],
        out_specs=pl.BlockSpec((1, tt, D_MODEL), lambda b, i, k: (b, i, 0)),
        out_shape=jax.ShapeDtypeStruct((B, T, D_MODEL), F32),
        scratch_shapes=[pltpu.VMEM((ne, tt), F32), pltpu.VMEM((ne, tt), BF16), pltpu.VMEM((ne, tt), BF16),
                        pltpu.VMEM((D_MODEL, tt), F32),
                        pltpu.VMEM((N_HEADS, PEER_N_KEYS, tt), BF16), pltpu.VMEM((N_HEADS, PEER_N_KEYS, tt), BF16)],
        compiler_params=_cparams(("parallel", "parallel", "arbitrary")),
        name="peer_dense",
    )(h2, u_bf16, vt_bf16, n1, e1, r2, e2, x1, mod, ln_gb)


def _trunk_layer(x, mod, layer, p, ctx, rope_tabs, consts, flat_rows):
    B, T, _ = x.shape
    if flat_rows is not None:
        z, zn = _in_proj(x.reshape(-1, flat_rows, D_MODEL), mod, p["w_in"], None, tm=flat_rows)
        z, zn = z.reshape(B, T, N_WIDE), zn.reshape(B, T, N_NARROW)
    else:
        z, zn = _in_proj(x, mod, p["w_in"], rope_tabs, tm=PROJ_ROWS)
    if ctx is None:
        oatt = _attention(z, p["lam"], p["subln"], None, layer, p["att_scale"], tq=min(ATT_Q_ROWS, T), tk=T)
        o_f, o_b, s_fin = _hgrn(z, p["lb"], None, layer, consts)
    else:
        oatt = _attention(z, p["lam"], p["subln"], (ctx[0], ctx[1]), layer, p["att_scale"], tq=ATT_Q_ROWS,
                          tk=ATT_KEY_CHUNK)
        o_f, o_b, s_fin = _hgrn(z, p["lb"], ctx[2], layer, consts)
    x1, h2 = _merge(x, zn, oatt, o_f, o_b, mod, p["w_branch"], p["w_out"], p["conv_w"], p["conv_b"],
                    p["rec_norm"], p["ln0"], tm=MERGE_ROWS)
    if flat_rows is not None:
        h2r, x1r = h2.reshape(-1, flat_rows, D_MODEL), x1.reshape(-1, flat_rows, D_MODEL)
    else:
        h2r, x1r = h2, x1
    route = _peer_route(h2r, p["peer_wq"], p["peer_keys"], PEER_TOKENS)
    x2 = _peer_dense(h2r, x1r, mod, p["peer_u"], p["peer_vt"], route, p["ln1"], PEER_TOKENS).reshape(B, T, D_MODEL)
    return x2, z, s_fin


def _layer_params(l, lb_all, w_in, attn_lambda, attn_subln, conv_w, conv_b, rec_norm, w_branch, w_out, ln_g, ln_b,
                  peer_wq, peer_keys, peer_u, peer_v):
    lam_init = 0.8 - 0.6 * math.exp(-0.3 * l)
    lp = attn_lambda[l].astype(F32)
    lam = jnp.exp(jnp.sum(lp[0] * lp[1])) - jnp.exp(jnp.sum(lp[2] * lp[3])) + lam_init
    return {
        "w_in": jnp.concatenate([w_in[l][:, REF_COL_ORDER.index(c) * D_MODEL:(REF_COL_ORDER.index(c) + 1) * D_MODEL]
                                 for c in WIDE_COLS + NARROW_COLS], axis=1).astype(BF16),
        "lam": lam.reshape(1), "att_scale": 1.0 - lam_init,
        "subln": attn_subln[l].reshape(1, HEAD_W), "lb": lb_all[l],
        "w_branch": w_branch[l].astype(BF16), "w_out": w_out[l].astype(BF16),
        "conv_w": conv_w[l], "conv_b": conv_b[l].reshape(1, D_MODEL),
        "rec_norm": jnp.tile(rec_norm[l], N_HEADS).reshape(1, D_MODEL),
        "ln0": jnp.stack([ln_g[l, 0], ln_b[l, 0]]), "ln1": jnp.stack([ln_g[l, 1], ln_b[l, 1]]),
        "peer_wq": peer_wq[l].astype(BF16),
        "peer_keys": peer_keys[l].astype(BF16).reshape(2 * N_HEADS, PEER_N_KEYS, HEAD_W),
        "peer_u": peer_u[l].astype(BF16),
        "peer_vt": peer_v[l].astype(BF16).reshape(-1, PEER_STEP_KEYS * PEER_N_KEYS, D_MODEL).transpose(0, 2, 1),
    }


def kernel(x_prompt, x_sample, c, cache_attn_k, cache_attn_v, state_hgrn, c_ctx, mod_w, mod_b, w_in, attn_lambda,
           attn_subln, conv_w, conv_b, rec_lb, rec_norm, w_branch, w_out, ln_g, ln_b, peer_wq, peer_keys, peer_u,
           peer_v):
    B, T, _ = x_prompt.shape
    Bs, Ts, _ = x_sample.shape
    P = cache_attn_k.shape[2]
    lb_all = jnp.cumsum(jax.nn.softmax(rec_lb.astype(F32), axis=0), axis=0)
    lb_all = lb_all - lb_all[:1]
    cmat = jnp.concatenate([c_ctx[None, :], c, jnp.zeros((8 - 1 - Bs, D_MODEL), F32)], axis=0)
    mods = _mod_vectors(cmat, mod_w, mod_b).reshape(DEPTH, 8, 6, D_MODEL)
    rope_tabs = _rope_tables(Ts)
    consts = _rec_constants()
    ck = cache_attn_k.reshape(Bs, DEPTH, P, N_HEADS * HEAD_W)
    cv = cache_attn_v.reshape(Bs, DEPTH, P, N_HEADS * HEAD_W)

    y_p, y_s = x_prompt, x_sample
    ks, vs, ss = [], [], []
    for l in range(DEPTH):
        p = _layer_params(l, lb_all, w_in, attn_lambda, attn_subln, conv_w, conv_b, rec_norm, w_branch, w_out,
                          ln_g, ln_b, peer_wq, peer_keys, peer_u, peer_v)
        y_p, z_p, s_p = _trunk_layer(y_p, mods[l, 0:1], l, p, None, None, consts, flat_rows=PROJ_ROWS)
        ks.append(z_p[..., COL_K * D_MODEL:(COL_K + 1) * D_MODEL].reshape(B, T, N_HEADS, HEAD_W))
        vs.append(z_p[..., COL_V * D_MODEL:(COL_V + 1) * D_MODEL].reshape(B, T, N_HEADS, HEAD_W))
        ss.append(s_p)
        y_s, _, _ = _trunk_layer(y_s, mods[l, 1:1 + Bs], l, p, (ck, cv, state_hgrn), rope_tabs, consts, flat_rows=None)
    return (y_p, y_s, jnp.stack(ks, axis=1), jnp.stack(vs, axis=1), jnp.stack(ss, axis=1))
```

```python
import functools
import math

import numpy as np
import jax
import jax.numpy as jnp
from jax import lax
from jax.experimental import pallas as pl
from jax.experimental.pallas import tpu as pltpu

F32 = jnp.float32
BF16 = jnp.bfloat16

D_MODEL = 1024
DEPTH = 2
GRID_W = 64
N_HEADS = 8
HEAD_W = 128
ATT_QK_DIM = 64
ROPE_BASE = 10000.0
ROPE_AXIS_DIM = ATT_QK_DIM // 2
PEER_N_KEYS = 128
PEER_TOPK = 16
LN_EPS = 1e-5
DEEPNORM_ALPHA = (2 * DEPTH) ** 0.25
W_IN_COLS = 14 * D_MODEL
COL_Q, COL_K, COL_V, COL_CB, COL_CC, COL_CX, COL_RQ, COL_RFF, COL_RFB, COL_RI, COL_RG, COL_G = range(12)

VMEM_LIMIT = 56 * 1024 * 1024
PROJ_ROWS = 2048
PROJ_COLS = 512
ATT_Q_ROWS = 1024
ATT_KEY_CHUNK = 512
MERGE_ROWS = 256
PEER_TOKENS = 512
REC_C = 128
REC_LEVELS = 7
REC_HEADS_PER_STEP = 8


def _cparams(sem):
    return pltpu.CompilerParams(dimension_semantics=sem, vmem_limit_bytes=VMEM_LIMIT)


def _dot(a, b):
    return jnp.dot(a, b, preferred_element_type=F32)


def _dot_nt(a, b):
    return lax.dot_general(a, b, (((1,), (1,)), ((), ())), preferred_element_type=F32)


def _sigmoid(x):
    e = jnp.exp(-jnp.abs(x))
    r = 1.0 / (1.0 + e)
    return jnp.where(x >= 0, r, e * r)


def _layer_norm(y, g, b):
    mu = jnp.mean(y, axis=-1, keepdims=True)
    yc = y - mu
    var = jnp.mean(yc * yc, axis=-1, keepdims=True)
    return yc * lax.rsqrt(var + LN_EPS) * g + b


def _mod_body(c_ref, w_ref, b_ref, o_ref):
    c = c_ref[...]
    s = (c * _sigmoid(c)).astype(BF16)
    o_ref[0] = _dot(s, w_ref[0].astype(BF16)) + b_ref[0]


def _mod_vectors(cmat, mod_w, mod_b):
    tn = 1536
    return pl.pallas_call(
        _mod_body,
        grid=(DEPTH, 6 * D_MODEL // tn),
        in_specs=[pl.BlockSpec((8, D_MODEL), lambda l, j: (0, 0)),
                  pl.BlockSpec((1, D_MODEL, tn), lambda l, j: (l, 0, j)),
                  pl.BlockSpec((1, 1, tn), lambda l, j: (l, 0, j))],
        out_specs=pl.BlockSpec((1, 8, tn), lambda l, j: (l, 0, j)),
        out_shape=jax.ShapeDtypeStruct((DEPTH, 8, 6 * D_MODEL), F32),
        compiler_params=_cparams(("parallel", "parallel")),
        name="mod_vectors",
    )(cmat, mod_w, mod_b.reshape(DEPTH, 1, 6 * D_MODEL))


def _in_proj_body(x_ref, mod_ref, w_ref, *refs, rope, tm):
    if rope:
        cos_ref, sin_ref, o_ref, h_ref = refs
    else:
        o_ref, h_ref = refs
    j = pl.program_id(2)

    @pl.when(j == 0)
    def _():
        h_ref[...] = (x_ref[0] * (1.0 + mod_ref[0, 1:2, :]) + mod_ref[0, 0:1, :]).astype(BF16)

    z = _dot(h_ref[...], w_ref[...])
    if not rope:
        o_ref[0] = z
        return
    rope_tiles = COL_V * D_MODEL // PROJ_COLS

    @pl.when(j < rope_tiles)
    def _():
        cos = cos_ref[...]
        sin = sin_ref[...]
        lane = lax.broadcasted_iota(jnp.int32, (tm, HEAD_W), 1)
        first = (lane % ROPE_AXIS_DIM) < (ROPE_AXIS_DIM // 2)
        for g in range(PROJ_COLS // HEAD_W):
            zg = z[:, g * HEAD_W:(g + 1) * HEAD_W]
            partner = jnp.where(first, pltpu.roll(zg, HEAD_W - ROPE_AXIS_DIM // 2, 1),
                                pltpu.roll(zg, ROPE_AXIS_DIM // 2, 1))
            o_ref[0, :, g * HEAD_W:(g + 1) * HEAD_W] = zg * cos + partner * sin

    @pl.when(j >= rope_tiles)
    def _():
        o_ref[0] = z


def _in_proj(x, mod, w_bf16, rope_tabs, tm):
    B, T, _ = x.shape
    per_b = mod.shape[0] > 1
    rope = rope_tabs is not None
    in_specs = [pl.BlockSpec((1, tm, D_MODEL), lambda b, i, j: (b, i, 0)),
                pl.BlockSpec((1, 6, D_MODEL), (lambda b, i, j: (b, 0, 0)) if per_b else (lambda b, i, j: (0, 0, 0))),
                pl.BlockSpec((D_MODEL, PROJ_COLS), lambda b, i, j: (0, j))]
    args = [x, mod, w_bf16]
    if rope:
        in_specs += [pl.BlockSpec((tm, HEAD_W), lambda b, i, j: (i, 0))] * 2
        args += list(rope_tabs)
    return pl.pallas_call(
        functools.partial(_in_proj_body, rope=rope, tm=tm),
        grid=(B, T // tm, W_IN_COLS // PROJ_COLS),
        in_specs=in_specs,
        out_specs=pl.BlockSpec((1, tm, PROJ_COLS), lambda b, i, j: (b, i, j)),
        out_shape=jax.ShapeDtypeStruct((B, T, W_IN_COLS), F32),
        scratch_shapes=[pltpu.VMEM((tm, D_MODEL), BF16)],
        compiler_params=_cparams(("parallel", "parallel", "arbitrary")),
        name="in_proj_rope" if rope else "in_proj",
    )(*args)


def _rope_tables(T):
    t = np.arange(T)
    pos = np.stack([t // GRID_W, t % GRID_W], axis=1).astype(np.float32)
    lane = np.arange(HEAD_W)
    axis = (lane % ATT_QK_DIM) // ROPE_AXIS_DIM
    r = lane % ROPE_AXIS_DIM
    half = ROPE_AXIS_DIM // 2
    freqs = ROPE_BASE ** (-jnp.arange(0, ROPE_AXIS_DIM, 2, dtype=F32) / ROPE_AXIS_DIM)
    ang = jnp.asarray(pos)[:, axis] * freqs[r % half][None, :]
    sign = jnp.asarray(np.where(r < half, -1.0, 1.0).astype(np.float32))[None, :]
    return jnp.cos(ang), jnp.sin(ang) * sign


def _attn_body(lam_ref, q_ref, k_ref, v_ref, *refs, tq, tk, n_self, n_ctx, out_scale):
    if n_ctx:
        kc_ref, vc_ref, g_ref, o_ref, kb_ref, vb_ref = refs
    else:
        g_ref, o_ref, kb_ref, vb_ref = refs
    t_self = n_self * tk

    @pl.when(pl.program_id(2) == 0)
    def _():
        kb_ref[0:t_self, :] = k_ref[0].astype(BF16)
        vb_ref[0:t_self, :] = v_ref[0].astype(BF16)
        if n_ctx:
            kb_ref[t_self:, :] = kc_ref[0, 0].astype(BF16)
            vb_ref[t_self:, :] = vc_ref[0, 0].astype(BF16)

    q = q_ref[0] * (ATT_QK_DIM ** -0.5 * math.log2(math.e))
    lane = lax.broadcasted_iota(jnp.int32, (tq, HEAD_W), 1)
    lo = lane < ATT_QK_DIM
    qs = jnp.concatenate([jnp.where(lo, q, 0.0), jnp.where(lo, 0.0, q)], axis=0).astype(BF16)

    m = jnp.full((2 * tq, 1), -jnp.inf, F32)
    l = jnp.zeros((2 * tq, 1), F32)
    acc = jnp.zeros((2 * tq, HEAD_W), F32)
    for c in range(n_self + n_ctx):
        s = _dot_nt(qs, kb_ref[c * tk:(c + 1) * tk, :])
        m_new = jnp.maximum(m, jnp.max(s, axis=-1, keepdims=True))
        a = jnp.exp2(m - m_new)
        p = jnp.exp2(s - m_new)
        l = a * l + jnp.sum(p, axis=-1, keepdims=True)
        acc = a * acc + _dot(p.astype(BF16), vb_ref[c * tk:(c + 1) * tk, :])
        m = m_new
    o = acc / l
    o = o[:tq] - lam_ref[0] * o[tq:]
    o = o * lax.rsqrt(jnp.mean(o * o, axis=-1, keepdims=True) + LN_EPS) * g_ref[...] * out_scale
    o_ref[0] = o.astype(BF16)


def _attention(z, lam, subln, ctx_kv, layer, out_scale, tq, tk):
    B, T, _ = z.shape
    n_self = T // tk
    in_specs = [pl.BlockSpec(memory_space=pltpu.SMEM),
                pl.BlockSpec((1, tq, HEAD_W), lambda b, h, i: (b, i, COL_Q * N_HEADS + h)),
                pl.BlockSpec((1, T, HEAD_W), lambda b, h, i: (b, 0, COL_K * N_HEADS + h)),
                pl.BlockSpec((1, T, HEAD_W), lambda b, h, i: (b, 0, COL_V * N_HEADS + h))]
    args = [lam, z, z, z]
    t_all = T
    n_ctx = 0
    if ctx_kv is not None:
        P = ctx_kv[0].shape[2]
        assert P == tk
        n_ctx = 1
        t_all = T + P
        in_specs += [pl.BlockSpec((1, 1, P, HEAD_W), lambda b, h, i: (b, layer, 0, h))] * 2
        args += list(ctx_kv)
    in_specs.append(pl.BlockSpec((1, HEAD_W), lambda b, h, i: (0, 0)))
    args.append(subln)
    return pl.pallas_call(
        functools.partial(_attn_body, tq=tq, tk=tk, n_self=n_self, n_ctx=n_ctx, out_scale=out_scale),
        grid=(B, N_HEADS, T // tq),
        in_specs=in_specs,
        out_specs=pl.BlockSpec((1, tq, HEAD_W), lambda b, h, i: (b, i, h)),
        out_shape=jax.ShapeDtypeStruct((B, T, D_MODEL), BF16),
        scratch_shapes=[pltpu.VMEM((t_all, HEAD_W), BF16), pltpu.VMEM((t_all, HEAD_W), BF16)],
        compiler_params=_cparams(("parallel", "parallel", "arbitrary")),
        name="attention_ctx" if n_ctx else "attention",
    )(*args)


def _rec_constants():
    C = REC_C
    t = np.arange(C)[:, None]
    j = np.arange(C)[None, :]
    blocks = [(j <= t), (j > t)]
    for l in range(REC_LEVELS):
        m = 1 << l
        seg0 = (t // m) * m
        odd = ((t // m) % 2) == 1
        blocks.append(np.where(odd, (j >= seg0) & (j <= t), (j > t) & (j <= seg0 + m - 1)))
    mf = np.concatenate(blocks, axis=0).astype(np.float32)
    x = t ^ j
    lv = np.where(x == 0, REC_LEVELS, np.floor(np.log2(np.maximum(x, 1))).astype(np.int64))
    lvf = np.where(j <= t, lv, REC_LEVELS + 1).astype(np.int32)
    mb = mf.reshape(-1, C, C)[:, ::-1, ::-1].reshape(-1, C)
    lvb = lvf[::-1, ::-1]
    mf, mb = np.concatenate([mf, mf], axis=1), np.concatenate([mb, mb], axis=1)
    return (jnp.asarray(mf, BF16), jnp.asarray(np.ascontiguousarray(mb), BF16),
            jnp.asarray(lvf), jnp.asarray(np.ascontiguousarray(lvb)))


def _rec_pair(rq, ri, fx, lb, m_ref, lv, sts, backward):
    C = REC_C
    e = jnp.exp(-jnp.abs(fx))
    r = 1.0 / (1.0 + e)
    pos = fx >= 0
    sig = jnp.where(pos, r, e * r)
    nsig = jnp.where(pos, e * r, r)
    logf = jnp.log(lb + (1.0 - lb) * sig)
    kk = (1.0 - lb) * nsig
    q = rq * _sigmoid(rq)
    hi = logf.astype(BF16)
    mid = (logf - hi.astype(F32)).astype(BF16)
    lf2 = jnp.concatenate([hi, mid], axis=0)

    def expo(blk):
        return _dot(m_ref[blk * C:(blk + 1) * C, :], lf2)

    row = lax.broadcasted_iota(jnp.int32, (C, 2 * HEAD_W), 0)
    if backward:
        row = (C - 1) - row
    qb = q.astype(BF16)
    kb = kk.astype(BF16)
    ws = []
    for l in range(REC_LEVELS):
        odd = ((row >> l) & 1) == 1
        ws.append((jnp.exp(expo(2 + l)) * jnp.where(odd, q, kk)).astype(BF16))
    b_incl = expo(0)
    qi = (q * jnp.exp(b_incl)).astype(BF16)
    ki = (kk * jnp.exp(expo(1))).astype(BF16)
    dec = jnp.exp(b_incl[0:1, :] if backward else b_incl[C - 1:C, :])
    outs, new_sts = [], []
    for g in range(2):
        ls = slice(g * HEAD_W, (g + 1) * HEAD_W)
        a = jnp.where(lv == REC_LEVELS, _dot_nt(qb[:, ls], kb[:, ls]), 0.0)
        for l in range(REC_LEVELS):
            a = jnp.where(lv == l, _dot_nt(ws[l][:, ls], ws[l][:, ls]), a)
        rig = ri[:, ls]
        outs.append(_dot(a.astype(BF16), rig.astype(BF16)) + _dot_nt(qi[:, ls], sts[g].astype(BF16)))
        new_sts.append(sts[g] * dec[:, ls] + _dot(rig.T.astype(BF16), ki[:, ls]))
    return jnp.concatenate(outs, axis=1), new_sts


def _hgrn_body(mf_ref, mb_ref, lvf_ref, lvb_ref, rqf_ref, rif_ref, ff_ref, rqb_ref, rib_ref, fb_ref, lb_ref,
               *refs, has_s0):
    if has_s0:
        s0_ref, of_ref, ob_ref, so_ref, sf_scr, sb_scr = refs
    else:
        of_ref, ob_ref, so_ref, sf_scr, sb_scr = refs
    c = pl.program_id(2)

    @pl.when(c == 0)
    def _():
        for g in range(REC_HEADS_PER_STEP):
            if has_s0:
                sf_scr[g] = s0_ref[0, 0, g].T
                sb_scr[g] = s0_ref[0, 1, g].T
            else:
                sf_scr[g] = jnp.zeros((HEAD_W, HEAD_W), F32)
                sb_scr[g] = jnp.zeros((HEAD_W, HEAD_W), F32)

    for g in range(0, REC_HEADS_PER_STEP, 2):
        ls = slice(g * HEAD_W, (g + 2) * HEAD_W)
        o, st = _rec_pair(rqf_ref[0, :, ls], rif_ref[0, :, ls], ff_ref[0, :, ls], lb_ref[0:1, ls], mf_ref,
                          lvf_ref[...], [sf_scr[g], sf_scr[g + 1]], False)
        of_ref[0, :, ls] = o
        sf_scr[g] = st[0]
        sf_scr[g + 1] = st[1]
        o, st = _rec_pair(rqb_ref[0, :, ls], rib_ref[0, :, ls], fb_ref[0, :, ls], lb_ref[1:2, ls], mb_ref,
                          lvb_ref[...], [sb_scr[g], sb_scr[g + 1]], True)
        ob_ref[0, :, ls] = o
        sb_scr[g] = st[0]
        sb_scr[g + 1] = st[1]

    @pl.when(c == pl.num_programs(2) - 1)
    def _():
        for g in range(REC_HEADS_PER_STEP):
            so_ref[0, 0, g] = sf_scr[g].T
            so_ref[0, 1, g] = sb_scr[g].T


def _hgrn(z, lb, s0, layer, consts):
    B, T, _ = z.shape
    n = T // REC_C
    C = REC_C
    G = REC_HEADS_PER_STEP
    ng = N_HEADS // G
    W = G * HEAD_W

    def fwd(col):
        return pl.BlockSpec((1, C, W), lambda b, h, c: (b, c, col * ng + h))

    def bwd(col):
        return pl.BlockSpec((1, C, W), lambda b, h, c: (b, n - 1 - c, col * ng + h))

    const2 = lambda b, h, c: (0, 0)
    in_specs = [pl.BlockSpec(((2 + REC_LEVELS) * C, 2 * C), const2), pl.BlockSpec(((2 + REC_LEVELS) * C, 2 * C), const2),
                pl.BlockSpec((C, C), const2), pl.BlockSpec((C, C), const2),
                fwd(COL_RQ), fwd(COL_RI), fwd(COL_RFF), bwd(COL_RQ), bwd(COL_RI), bwd(COL_RFB),
                pl.BlockSpec((2, W), lambda b, h, c: (0, h))]
    args = list(consts) + [z] * 6 + [lb]
    if s0 is not None:
        s0v = s0.reshape(B, DEPTH * 2, N_HEADS, HEAD_W, HEAD_W)
        in_specs.append(pl.BlockSpec((1, 2, G, HEAD_W, HEAD_W), lambda b, h, c: (b, layer, h, 0, 0)))
        args.append(s0v)
    return pl.pallas_call(
        functools.partial(_hgrn_body, has_s0=s0 is not None),
        grid=(B, ng, n),
        in_specs=in_specs,
        out_specs=[pl.BlockSpec((1, C, W), lambda b, h, c: (b, c, h)),
                   pl.BlockSpec((1, C, W), lambda b, h, c: (b, n - 1 - c, h)),
                   pl.BlockSpec((1, 2, G, HEAD_W, HEAD_W), lambda b, h, c: (b, 0, h, 0, 0))],
        out_shape=[jax.ShapeDtypeStruct((B, T, D_MODEL), F32), jax.ShapeDtypeStruct((B, T, D_MODEL), F32),
                   jax.ShapeDtypeStruct((B, 2, N_HEADS, HEAD_W, HEAD_W), F32)],
        scratch_shapes=[pltpu.VMEM((G, HEAD_W, HEAD_W), F32), pltpu.VMEM((G, HEAD_W, HEAD_W), F32)],
        compiler_params=_cparams(("parallel", "parallel", "arbitrary")),
        name="hgrn_ctx" if s0 is not None else "hgrn",
    )(*args)


def _merge_body(oatt_ref, zb_ref, zc_ref, zx_ref, zcp_ref, zxp_ref, zcn_ref, zxn_ref, of_ref, ob_ref, rg_ref,
                g0_ref, g1_ref, g2_ref, x_ref, mod_ref, wb_ref, wo_ref, cw_ref, cb_ref, rn_ref, ln_ref,
                x1_ref, h2_ref, *, tm):
    i = pl.program_id(1)
    u = zc_ref[0] * zx_ref[0]
    row = lax.broadcasted_iota(jnp.int32, (tm, D_MODEL), 0)
    prev_ok = (i > 0).astype(F32)
    next_ok = (i < pl.num_programs(1) - 1).astype(F32)
    u_prev_edge = zcp_ref[0, 7:8, :] * zxp_ref[0, 7:8, :] * prev_ok
    u_next_edge = zcn_ref[0, 0:1, :] * zxn_ref[0, 0:1, :] * next_ok
    up = jnp.where(row == 0, u_prev_edge, pltpu.roll(u, 1, 0))
    un = jnp.where(row == tm - 1, u_next_edge, pltpu.roll(u, tm - 1, 0))
    conv = up * cw_ref[0:1, :] + u * cw_ref[1:2, :] + un * cw_ref[2:3, :] + cb_ref[...]
    o_conv = (zb_ref[0] * conv).astype(BF16)
    s = of_ref[0] + ob_ref[0]
    rg = rg_ref[0]
    parts = []
    for h in range(N_HEADS):
        sh = s[:, h * HEAD_W:(h + 1) * HEAD_W]
        parts.append(sh * lax.rsqrt(jnp.mean(sh * sh, axis=-1, keepdims=True) + LN_EPS))
    o_rec = (jnp.concatenate(parts, axis=1) * rn_ref[...] * (rg * _sigmoid(rg))).astype(BF16)
    merged = (_sigmoid(g0_ref[0]) * _dot(oatt_ref[0], wb_ref[0])
              + _sigmoid(g1_ref[0]) * _dot(o_conv, wb_ref[1])
              + _sigmoid(g2_ref[0]) * _dot(o_rec, wb_ref[2]))
    mix = _dot(merged.astype(BF16), wo_ref[...])
    y = DEEPNORM_ALPHA * x_ref[0] + mod_ref[0, 2:3, :] * mix
    x1 = _layer_norm(y, ln_ref[0:1, :], ln_ref[1:2, :])
    x1_ref[0] = x1
    h2_ref[0] = (x1 * (1.0 + mod_ref[0, 4:5, :]) + mod_ref[0, 3:4, :]).astype(BF16)


def _merge(x, z, oatt, o_f, o_b, mod, wb_bf16, wo_bf16, conv_w, conv_b, rec_norm_t, ln_gb, tm):
    B, T, _ = x.shape
    per_b = mod.shape[0] > 1
    nb8 = tm // 8
    last8 = T // 8 - 1

    def col(c):
        return pl.BlockSpec((1, tm, D_MODEL), lambda b, i: (b, i, c))

    def prev8(c):
        return pl.BlockSpec((1, 8, D_MODEL), lambda b, i: (b, jnp.maximum(i * nb8 - 1, 0), c))

    def next8(c):
        return pl.BlockSpec((1, 8, D_MODEL), lambda b, i: (b, jnp.minimum((i + 1) * nb8, last8), c))

    tile = pl.BlockSpec((1, tm, D_MODEL), lambda b, i: (b, i, 0))
    full2 = lambda b, i: (0, 0)
    in_specs = [tile, col(COL_CB), col(COL_CC), col(COL_CX), prev8(COL_CC), prev8(COL_CX), next8(COL_CC), next8(COL_CX),
                tile, tile, col(COL_RG), col(COL_G), col(COL_G + 1), col(COL_G + 2), tile,
                pl.BlockSpec((1, 6, D_MODEL), (lambda b, i: (b, 0, 0)) if per_b else (lambda b, i: (0, 0, 0))),
                pl.BlockSpec((3, D_MODEL, D_MODEL), lambda b, i: (0, 0, 0)),
                pl.BlockSpec((D_MODEL, D_MODEL), full2),
                pl.BlockSpec((3, D_MODEL), full2), pl.BlockSpec((1, D_MODEL), full2),
                pl.BlockSpec((1, D_MODEL), full2), pl.BlockSpec((2, D_MODEL), full2)]
    return pl.pallas_call(
        functools.partial(_merge_body, tm=tm),
        grid=(B, T // tm),
        in_specs=in_specs,
        out_specs=[tile, tile],
        out_shape=[jax.ShapeDtypeStruct((B, T, D_MODEL), F32), jax.ShapeDtypeStruct((B, T, D_MODEL), BF16)],
        compiler_params=_cparams(("parallel", "arbitrary")),
        name="merge",
    )(oatt, z, z, z, z, z, z, z, o_f, o_b, z, z, z, z, x, mod, wb_bf16, wo_bf16, conv_w, conv_b, rec_norm_t, ln_gb)


def _top_rows(s, k, one_per_round, want_rank=True):
    n, w = s.shape
    s_in = s
    track = one_per_round or want_rank
    rid = lax.broadcasted_iota(jnp.int32, (n, w), 0).astype(F32)
    rank = jnp.full((n, w), float(k), F32) if track else None
    vals = []
    for r in range(k):
        m = jnp.max(s, axis=0, keepdims=True)
        hit = s == m
        if one_per_round:
            first = jnp.min(jnp.where(hit, rid, float(n)), axis=0, keepdims=True)
            hit = rid == first
        s = jnp.where(hit, -jnp.inf, s)
        if track:
            rank = jnp.where(hit, float(r), rank)
        vals.append(m)
    taken = (rank < k) if track else (s_in >= vals[-1])
    excess = jnp.sum(jnp.where(taken, 1.0, 0.0), axis=0, keepdims=True) - k
    return vals, rank, excess


def _route_body(h_ref, wq_ref, keys_ref, n1_ref, e1_ref, r2_ref, e2_ref, q_scr, s_scr, *, tt):
    q_scr[...] = _dot(h_ref[0], wq_ref[...]).astype(BF16)

    def head(h, carry):
        for p in range(2):
            off = pl.multiple_of((2 * h + p) * HEAD_W, HEAD_W)
            s_scr[p] = _dot_nt(keys_ref[2 * h + p], q_scr[:, pl.ds(off, HEAD_W)])
        excess = route_head(h, False)

        @pl.when(jnp.max(excess) > 0.0)
        def _():
            route_head(h, True)

        return carry

    def route_head(h, one_per_round):
        excess = jnp.zeros((1, HEAD_W), F32)
        for lc in range(tt // HEAD_W):
            ls = slice(lc * HEAD_W, (lc + 1) * HEAD_W)
            s1 = s_scr[0, :, ls]
            s2 = s_scr[1, :, ls]
            v1, r1, x1 = _top_rows(s1, PEER_TOPK, one_per_round, want_rank=False)
            v2, r2, x2 = _top_rows(s2, PEER_TOPK, one_per_round)
            v2a = jnp.concatenate(v2, axis=0)
            cands = [v1[0] + v2a] + [v1[a] + v2a[:8] for a in range(1, 8)] + [jnp.concatenate(v1[8:], axis=0) + v2[0]]
            cand = jnp.concatenate(cands, axis=0)
            vc, rc, xc = _top_rows(cand, PEER_TOPK, one_per_round, want_rank=False)
            excess = jnp.maximum(excess, jnp.maximum(jnp.maximum(x1, x2), xc))
            sel = (rc < PEER_TOPK) if one_per_round else (cand >= vc[-1])
            zsum = jnp.sum(jnp.where(sel, jnp.exp(cand - (v1[0] + v2[0])), 0.0), axis=0, keepdims=True)
            selc = jnp.where(sel, 1.0, 0.0)
            n_a = [jnp.sum(selc[0:16], axis=0, keepdims=True)]
            n_a += [jnp.sum(selc[8 + 8 * a:16 + 8 * a], axis=0, keepdims=True) for a in range(1, 8)]
            n_a += [selc[72 + a:73 + a] for a in range(8)]
            n1 = jnp.zeros_like(s1)
            for a in range(PEER_TOPK):
                n1 = jnp.where((r1 == a) if one_per_round else (s1 == v1[a]), n_a[a], n1)
            n1_ref[h, :, ls] = n1
            e1_ref[h, :, ls] = jnp.exp(s1 - v1[0]) * (0.5 / zsum)
            r2_ref[h, :, ls] = r2.astype(BF16)
            e2_ref[h, :, ls] = jnp.exp(s2 - v2[0]).astype(BF16)
        return excess

    lax.fori_loop(0, N_HEADS, head, 0)


def _peer_route(h2, wq_bf16, keys_bf16, tt):
    B, T, _ = h2.shape
    big = pl.BlockSpec((None, N_HEADS, PEER_N_KEYS, tt), lambda b, i: (b, 0, 0, i))
    f32_shape = jax.ShapeDtypeStruct((B, N_HEADS, PEER_N_KEYS, T), F32)
    bf16_shape = jax.ShapeDtypeStruct((B, N_HEADS, PEER_N_KEYS, T), BF16)
    return pl.pallas_call(
        functools.partial(_route_body, tt=tt),
        grid=(B, T // tt),
        in_specs=[pl.BlockSpec((1, tt, D_MODEL), lambda b, i: (b, i, 0)),
                  pl.BlockSpec((D_MODEL, 2 * N_HEADS * HEAD_W), lambda b, i: (0, 0)),
                  pl.BlockSpec((2 * N_HEADS, PEER_N_KEYS, HEAD_W), lambda b, i: (0, 0, 0))],
        out_specs=[big, big, big, big],
        out_shape=[f32_shape, f32_shape, bf16_shape, bf16_shape],
        scratch_shapes=[pltpu.VMEM((tt, 2 * N_HEADS * HEAD_W), BF16), pltpu.VMEM((2, PEER_N_KEYS, tt), F32)],
        compiler_params=_cparams(("parallel", "parallel")),
        name="peer_route",
    )(h2, wq_bf16, keys_bf16)


PEER_STEP_KEYS = 16
PEER_SUB_KEYS = 4


def _dense_body(h_ref, u_ref, vt_ref, n1_ref, e1_ref, r2_ref, e2_ref, x_ref, mod_ref, ln_ref, o_ref,
                act_scr, g_scr, w_scr, acc_scr, r2_scr, e2_scr, *, tt):
    k = pl.program_id(2)

    @pl.when(k == 0)
    def _():
        acc_scr[...] = jnp.zeros_like(acc_scr)
        r2_scr[...] = r2_ref[...]
        e2_scr[...] = e2_ref[...]

    hb = h_ref[0]
    sub = PEER_SUB_KEYS * PEER_N_KEYS

    def routing_weights(jj):
        for j in range(PEER_SUB_KEYS * jj, PEER_SUB_KEYS * (jj + 1)):
            for lc in range(tt // HEAD_W):
                ls = slice(lc * HEAD_W, (lc + 1) * HEAD_W)
                g = None
                for h in range(N_HEADS):
                    n_t = jnp.broadcast_to(n1_ref[h, j:j + 1, ls], (16, HEAD_W)).astype(BF16)[None]
                    e_t = jnp.broadcast_to(e1_ref[h, j:j + 1, ls], (16, HEAD_W)).astype(BF16)[None]
                    hit = r2_scr[h, :, ls].reshape(PEER_N_KEYS // 16, 16, HEAD_W) < n_t
                    term = jnp.where(hit, e2_scr[h, :, ls].reshape(PEER_N_KEYS // 16, 16, HEAD_W), 0.0) * e_t
                    g = term if g is None else g + term
                g_scr[j * PEER_N_KEYS:(j + 1) * PEER_N_KEYS, ls] = g.reshape(PEER_N_KEYS, HEAD_W)

    nsub = PEER_STEP_KEYS // PEER_SUB_KEYS
    half = sub * nsub // 2
    act_scr[0:half, :] = _dot_nt(u_ref[0:half, :], hb)
    act_scr[half:, :] = _dot_nt(u_ref[half:, :], hb)
    for jj in range(nsub):
        routing_weights(jj)
    for jj in range(nsub):
        rows = slice(jj * sub, (jj + 1) * sub)
        a = act_scr[rows, :]
        gelu2 = a * (1.0 + lax.erf(a * (2.0 ** -0.5)))
        w_scr[rows, :] = gelu2.astype(BF16) * g_scr[rows, :]
        acc_scr[...] += _dot(vt_ref[0, :, rows], w_scr[rows, :])

    @pl.when(k == pl.num_programs(2) - 1)
    def _():
        y = DEEPNORM_ALPHA * x_ref[0] + mod_ref[0, 5:6, :] * acc_scr[...].T
        o_ref[0] = _layer_norm(y, ln_ref[0:1, :], ln_ref[1:2, :])


def _peer_dense(h2, x1, mod, u_bf16, vt_bf16, route, ln_gb, tt):
    B, T, _ = h2.shape
    n1, e1, r2, e2 = route
    per_b = mod.shape[0] > 1
    ne = PEER_STEP_KEYS * PEER_N_KEYS
    rows = pl.BlockSpec((None, N_HEADS, PEER_STEP_KEYS, tt), lambda b, i, k: (b, 0, k, i))
    full = pl.BlockSpec((None, N_HEADS, PEER_N_KEYS, tt), lambda b, i, k: (b, 0, 0, i))
    return pl.pallas_call(
        functools.partial(_dense_body, tt=tt),
        grid=(B, T // tt, PEER_N_KEYS // PEER_STEP_KEYS),
        in_specs=[pl.BlockSpec((1, tt, D_MODEL), lambda b, i, k: (b, i, 0)),
                  pl.BlockSpec((ne, D_MODEL), lambda b, i, k: (k, 0)),
                  pl.BlockSpec((1, D_MODEL, ne), lambda b, i, k: (k, 0, 0)),
                  rows, rows, full, full,
                  pl.BlockSpec((1, tt, D_MODEL), lambda b, i, k: (b, i, 0)),
                  pl.BlockSpec((1, 6, D_MODEL), (lambda b, i, k: (b, 0, 0)) if per_b else (lambda b, i, k: (0, 0, 0))),
                  pl.BlockSpec((2, D_MODEL), lambda b, i, k: (0, 0))],
        out_specs=pl.BlockSpec((1, tt, D_MODEL), lambda b, i, k: (b, i, 0)),
        out_shape=jax.ShapeDtypeStruct((B, T, D_MODEL), F32),
        scratch_shapes=[pltpu.VMEM((ne, tt), F32), pltpu.VMEM((ne, tt), BF16), pltpu.VMEM((ne, tt), BF16),
                        pltpu.VMEM((D_MODEL, tt), F32),
                        pltpu.VMEM((N_HEADS, PEER_N_KEYS, tt), BF16), pltpu.VMEM((N_HEADS, PEER_N_KEYS, tt), BF16)],
        compiler_params=_cparams(("parallel", "parallel", "arbitrary")),
        name="peer_dense",
    )(h2, u_bf16, vt_bf16, n1, e1, r2, e2, x1, mod, ln_gb)


def _trunk_layer(x, mod, layer, p, ctx, rope_tabs, consts, flat_rows):
    B, T, _ = x.shape
    if flat_rows is not None:
        z = _in_proj(x.reshape(-1, flat_rows, D_MODEL), mod, p["w_in"], None, tm=flat_rows)
        z = z.reshape(B, T, W_IN_COLS)
    else:
        z = _in_proj(x, mod, p["w_in"], rope_tabs, tm=PROJ_ROWS)
    if ctx is None:
        oatt = _attention(z, p["lam"], p["subln"], None, layer, p["att_scale"], tq=min(ATT_Q_ROWS, T), tk=T)
        o_f, o_b, s_fin = _hgrn(z, p["lb"], None, layer, consts)
    else:
        oatt = _attention(z, p["lam"], p["subln"], (ctx[0], ctx[1]), layer, p["att_scale"], tq=ATT_Q_ROWS,
                          tk=ATT_KEY_CHUNK)
        o_f, o_b, s_fin = _hgrn(z, p["lb"], ctx[2], layer, consts)
    x1, h2 = _merge(x, z, oatt, o_f, o_b, mod, p["w_branch"], p["w_out"], p["conv_w"], p["conv_b"],
                    p["rec_norm"], p["ln0"], tm=MERGE_ROWS)
    if flat_rows is not None:
        h2r, x1r = h2.reshape(-1, flat_rows, D_MODEL), x1.reshape(-1, flat_rows, D_MODEL)
    else:
        h2r, x1r = h2, x1
    route = _peer_route(h2r, p["peer_wq"], p["peer_keys"], PEER_TOKENS)
    x2 = _peer_dense(h2r, x1r, mod, p["peer_u"], p["peer_vt"], route, p["ln1"], PEER_TOKENS).reshape(B, T, D_MODEL)
    return x2, z, s_fin


def _layer_params(l, lb_all, w_in, attn_lambda, attn_subln, conv_w, conv_b, rec_norm, w_branch, w_out, ln_g, ln_b,
                  peer_wq, peer_keys, peer_u, peer_v):
    lam_init = 0.8 - 0.6 * math.exp(-0.3 * l)
    lp = attn_lambda[l].astype(F32)
    lam = jnp.exp(jnp.sum(lp[0] * lp[1])) - jnp.exp(jnp.sum(lp[2] * lp[3])) + lam_init
    return {
        "w_in": w_in[l].astype(BF16), "lam": lam.reshape(1), "att_scale": 1.0 - lam_init,
        "subln": attn_subln[l].reshape(1, HEAD_W), "lb": lb_all[l],
        "w_branch": w_branch[l].astype(BF16), "w_out": w_out[l].astype(BF16),
        "conv_w": conv_w[l], "conv_b": conv_b[l].reshape(1, D_MODEL),
        "rec_norm": jnp.tile(rec_norm[l], N_HEADS).reshape(1, D_MODEL),
        "ln0": jnp.stack([ln_g[l, 0], ln_b[l, 0]]), "ln1": jnp.stack([ln_g[l, 1], ln_b[l, 1]]),
        "peer_wq": peer_wq[l].astype(BF16),
        "peer_keys": peer_keys[l].astype(BF16).reshape(2 * N_HEADS, PEER_N_KEYS, HEAD_W),
        "peer_u": peer_u[l].astype(BF16),
        "peer_vt": peer_v[l].astype(BF16).reshape(-1, PEER_STEP_KEYS * PEER_N_KEYS, D_MODEL).transpose(0, 2, 1),
    }


def kernel(x_prompt, x_sample, c, cache_attn_k, cache_attn_v, state_hgrn, c_ctx, mod_w, mod_b, w_in, attn_lambda,
           attn_subln, conv_w, conv_b, rec_lb, rec_norm, w_branch, w_out, ln_g, ln_b, peer_wq, peer_keys, peer_u,
           peer_v):
    B, T, _ = x_prompt.shape
    Bs, Ts, _ = x_sample.shape
    P = cache_attn_k.shape[2]
    lb_all = jnp.cumsum(jax.nn.softmax(rec_lb.astype(F32), axis=0), axis=0)
    lb_all = lb_all - lb_all[:1]
    cmat = jnp.concatenate([c_ctx[None, :], c, jnp.zeros((8 - 1 - Bs, D_MODEL), F32)], axis=0)
    mods = _mod_vectors(cmat, mod_w, mod_b).reshape(DEPTH, 8, 6, D_MODEL)
    rope_tabs = _rope_tables(Ts)
    consts = _rec_constants()
    ck = cache_attn_k.reshape(Bs, DEPTH, P, N_HEADS * HEAD_W)
    cv = cache_attn_v.reshape(Bs, DEPTH, P, N_HEADS * HEAD_W)

    y_p, y_s = x_prompt, x_sample
    ks, vs, ss = [], [], []
    for l in range(DEPTH):
        p = _layer_params(l, lb_all, w_in, attn_lambda, attn_subln, conv_w, conv_b, rec_norm, w_branch, w_out,
                          ln_g, ln_b, peer_wq, peer_keys, peer_u, peer_v)
        y_p, z_p, s_p = _trunk_layer(y_p, mods[l, 0:1], l, p, None, None, consts, flat_rows=PROJ_ROWS)
        ks.append(z_p[..., COL_K * D_MODEL:(COL_K + 1) * D_MODEL].reshape(B, T, N_HEADS, HEAD_W))
        vs.append(z_p[..., COL_V * D_MODEL:(COL_V + 1) * D_MODEL].reshape(B, T, N_HEADS, HEAD_W))
        ss.append(s_p)
        y_s, _, _ = _trunk_layer(y_s, mods[l, 1:1 + Bs], l, p, (ck, cv, state_hgrn), rope_tabs, consts, flat_rows=None)
    return (y_p, y_s, jnp.stack(ks, axis=1), jnp.stack(vs, axis=1), jnp.stack(ss, axis=1))
```

```python
import functools
import math

import numpy as np
import jax
import jax.numpy as jnp
from jax import lax
from jax.experimental import pallas as pl
from jax.experimental.pallas import tpu as pltpu

F32 = jnp.float32
BF16 = jnp.bfloat16

D_MODEL = 1024
DEPTH = 2
GRID_W = 64
N_HEADS = 8
HEAD_W = 128
ATT_QK_DIM = 64
ROPE_BASE = 10000.0
ROPE_AXIS_DIM = ATT_QK_DIM // 2
PEER_N_KEYS = 128
PEER_TOPK = 16
LN_EPS = 1e-5
DEEPNORM_ALPHA = (2 * DEPTH) ** 0.25
W_IN_COLS = 14 * D_MODEL
COL_Q, COL_K, COL_V, COL_CB, COL_CC, COL_CX, COL_RQ, COL_RFF, COL_RFB, COL_RI, COL_RG, COL_G = range(12)

VMEM_LIMIT = 56 * 1024 * 1024
PROJ_ROWS = 2048
PROJ_COLS = 512
ATT_Q_ROWS = 1024
ATT_KEY_CHUNK = 512
MERGE_ROWS = 256
PEER_TOKENS = 512
REC_C = 128
REC_LEVELS = 7
REC_HEADS_PER_STEP = 8


def _cparams(sem):
    return pltpu.CompilerParams(dimension_semantics=sem, vmem_limit_bytes=VMEM_LIMIT)


def _dot(a, b):
    return jnp.dot(a, b, preferred_element_type=F32)


def _dot_nt(a, b):
    return lax.dot_general(a, b, (((1,), (1,)), ((), ())), preferred_element_type=F32)


def _sigmoid(x):
    e = jnp.exp(-jnp.abs(x))
    r = 1.0 / (1.0 + e)
    return jnp.where(x >= 0, r, e * r)


def _layer_norm(y, g, b):
    mu = jnp.mean(y, axis=-1, keepdims=True)
    yc = y - mu
    var = jnp.mean(yc * yc, axis=-1, keepdims=True)
    return yc * lax.rsqrt(var + LN_EPS) * g + b


def _mod_body(c_ref, w_ref, b_ref, o_ref):
    c = c_ref[...]
    s = (c * _sigmoid(c)).astype(BF16)
    o_ref[0] = _dot(s, w_ref[0].astype(BF16)) + b_ref[0]


def _mod_vectors(cmat, mod_w, mod_b):
    tn = 1536
    return pl.pallas_call(
        _mod_body,
        grid=(DEPTH, 6 * D_MODEL // tn),
        in_specs=[pl.BlockSpec((8, D_MODEL), lambda l, j: (0, 0)),
                  pl.BlockSpec((1, D_MODEL, tn), lambda l, j: (l, 0, j)),
                  pl.BlockSpec((1, 1, tn), lambda l, j: (l, 0, j))],
        out_specs=pl.BlockSpec((1, 8, tn), lambda l, j: (l, 0, j)),
        out_shape=jax.ShapeDtypeStruct((DEPTH, 8, 6 * D_MODEL), F32),
        compiler_params=_cparams(("parallel", "parallel")),
        name="mod_vectors",
    )(cmat, mod_w, mod_b.reshape(DEPTH, 1, 6 * D_MODEL))


def _in_proj_body(x_ref, mod_ref, w_ref, *refs, rope, tm):
    if rope:
        cos_ref, sin_ref, o_ref, h_ref = refs
    else:
        o_ref, h_ref = refs
    j = pl.program_id(2)

    @pl.when(j == 0)
    def _():
        h_ref[...] = (x_ref[0] * (1.0 + mod_ref[0, 1:2, :]) + mod_ref[0, 0:1, :]).astype(BF16)

    z = _dot(h_ref[...], w_ref[...])
    if not rope:
        o_ref[0] = z
        return
    rope_tiles = COL_V * D_MODEL // PROJ_COLS

    @pl.when(j < rope_tiles)
    def _():
        cos = cos_ref[...]
        sin = sin_ref[...]
        lane = lax.broadcasted_iota(jnp.int32, (tm, HEAD_W), 1)
        first = (lane % ROPE_AXIS_DIM) < (ROPE_AXIS_DIM // 2)
        for g in range(PROJ_COLS // HEAD_W):
            zg = z[:, g * HEAD_W:(g + 1) * HEAD_W]
            partner = jnp.where(first, pltpu.roll(zg, HEAD_W - ROPE_AXIS_DIM // 2, 1),
                                pltpu.roll(zg, ROPE_AXIS_DIM // 2, 1))
            o_ref[0, :, g * HEAD_W:(g + 1) * HEAD_W] = zg * cos + partner * sin

    @pl.when(j >= rope_tiles)
    def _():
        o_ref[0] = z


def _in_proj(x, mod, w_bf16, rope_tabs, tm):
    B, T, _ = x.shape
    per_b = mod.shape[0] > 1
    rope = rope_tabs is not None
    in_specs = [pl.BlockSpec((1, tm, D_MODEL), lambda b, i, j: (b, i, 0)),
                pl.BlockSpec((1, 6, D_MODEL), (lambda b, i, j: (b, 0, 0)) if per_b else (lambda b, i, j: (0, 0, 0))),
                pl.BlockSpec((D_MODEL, PROJ_COLS), lambda b, i, j: (0, j))]
    args = [x, mod, w_bf16]
    if rope:
        in_specs += [pl.BlockSpec((tm, HEAD_W), lambda b, i, j: (i, 0))] * 2
        args += list(rope_tabs)
    return pl.pallas_call(
        functools.partial(_in_proj_body, rope=rope, tm=tm),
        grid=(B, T // tm, W_IN_COLS // PROJ_COLS),
        in_specs=in_specs,
        out_specs=pl.BlockSpec((1, tm, PROJ_COLS), lambda b, i, j: (b, i, j)),
        out_shape=jax.ShapeDtypeStruct((B, T, W_IN_COLS), F32),
        scratch_shapes=[pltpu.VMEM((tm, D_MODEL), BF16)],
        compiler_params=_cparams(("parallel", "parallel", "arbitrary")),
        name="in_proj_rope" if rope else "in_proj",
    )(*args)


def _rope_tables(T):
    t = np.arange(T)
    pos = np.stack([t // GRID_W, t % GRID_W], axis=1).astype(np.float32)
    lane = np.arange(HEAD_W)
    axis = (lane % ATT_QK_DIM) // ROPE_AXIS_DIM
    r = lane % ROPE_AXIS_DIM
    half = ROPE_AXIS_DIM // 2
    freqs = ROPE_BASE ** (-jnp.arange(0, ROPE_AXIS_DIM, 2, dtype=F32) / ROPE_AXIS_DIM)
    ang = jnp.asarray(pos)[:, axis] * freqs[r % half][None, :]
    sign = jnp.asarray(np.where(r < half, -1.0, 1.0).astype(np.float32))[None, :]
    return jnp.cos(ang), jnp.sin(ang) * sign


def _attn_body(lam_ref, q_ref, k_ref, v_ref, *refs, tq, tk, n_self, n_ctx, out_scale, hp):
    if n_ctx:
        kc_ref, vc_ref, g_ref, o_ref, kb_ref, vb_ref = refs
    else:
        g_ref, o_ref, kb_ref, vb_ref = refs
    t_self = n_self * tk

    @pl.when(pl.program_id(2) == 0)
    def _():
        for hd in range(hp):
            hs = slice(hd * HEAD_W, (hd + 1) * HEAD_W)
            kb_ref[hd, 0:t_self, :] = k_ref[0, :, hs].astype(BF16)
            vb_ref[hd, 0:t_self, :] = v_ref[0, :, hs].astype(BF16)
            if n_ctx:
                kb_ref[hd, t_self:, :] = kc_ref[0, 0, :, hs].astype(BF16)
                vb_ref[hd, t_self:, :] = vc_ref[0, 0, :, hs].astype(BF16)

    lane = lax.broadcasted_iota(jnp.int32, (tq, HEAD_W), 1)
    lo = lane < ATT_QK_DIM
    for hd in range(hp):
        hs = slice(hd * HEAD_W, (hd + 1) * HEAD_W)
        q = q_ref[0, :, hs] * (ATT_QK_DIM ** -0.5 * math.log2(math.e))
        qs = jnp.concatenate([jnp.where(lo, q, 0.0), jnp.where(lo, 0.0, q)], axis=0).astype(BF16)

        m = jnp.full((2 * tq, 1), -jnp.inf, F32)
        l = jnp.zeros((2 * tq, 1), F32)
        acc = jnp.zeros((2 * tq, HEAD_W), F32)
        for c in range(n_self + n_ctx):
            s = _dot_nt(qs, kb_ref[hd, c * tk:(c + 1) * tk, :])
            m_new = jnp.maximum(m, jnp.max(s, axis=-1, keepdims=True))
            a = jnp.exp2(m - m_new)
            p = jnp.exp2(s - m_new)
            l = a * l + jnp.sum(p, axis=-1, keepdims=True)
            acc = a * acc + _dot(p.astype(BF16), vb_ref[hd, c * tk:(c + 1) * tk, :])
            m = m_new
        o = acc / l
        o = o[:tq] - lam_ref[0] * o[tq:]
        o = o * lax.rsqrt(jnp.mean(o * o, axis=-1, keepdims=True) + LN_EPS) * g_ref[...] * out_scale
        o_ref[0, :, hs] = o.astype(BF16)


def _attention(z, lam, subln, ctx_kv, layer, out_scale, tq, tk):
    B, T, _ = z.shape
    n_self = T // tk
    hp = 1 if ctx_kv is not None else 2
    ng = N_HEADS // hp
    W = hp * HEAD_W
    in_specs = [pl.BlockSpec(memory_space=pltpu.SMEM),
                pl.BlockSpec((1, tq, W), lambda b, h, i: (b, i, COL_Q * ng + h)),
                pl.BlockSpec((1, T, W), lambda b, h, i: (b, 0, COL_K * ng + h)),
                pl.BlockSpec((1, T, W), lambda b, h, i: (b, 0, COL_V * ng + h))]
    args = [lam, z, z, z]
    t_all = T
    n_ctx = 0
    if ctx_kv is not None:
        P = ctx_kv[0].shape[2]
        assert P == tk
        n_ctx = 1
        t_all = T + P
        in_specs += [pl.BlockSpec((1, 1, P, W), lambda b, h, i: (b, layer, 0, h))] * 2
        args += list(ctx_kv)
    in_specs.append(pl.BlockSpec((1, HEAD_W), lambda b, h, i: (0, 0)))
    args.append(subln)
    return pl.pallas_call(
        functools.partial(_attn_body, tq=tq, tk=tk, n_self=n_self, n_ctx=n_ctx, out_scale=out_scale, hp=hp),
        grid=(B, ng, T // tq),
        in_specs=in_specs,
        out_specs=pl.BlockSpec((1, tq, W), lambda b, h, i: (b, i, h)),
        out_shape=jax.ShapeDtypeStruct((B, T, D_MODEL), BF16),
        scratch_shapes=[pltpu.VMEM((hp, t_all, HEAD_W), BF16), pltpu.VMEM((hp, t_all, HEAD_W), BF16)],
        compiler_params=_cparams(("parallel", "parallel", "arbitrary")),
        name="attention_ctx" if n_ctx else "attention",
    )(*args)


def _rec_constants():
    C = REC_C
    t = np.arange(C)[:, None]
    j = np.arange(C)[None, :]
    blocks = [(j <= t), (j > t)]
    for l in range(REC_LEVELS):
        m = 1 << l
        seg0 = (t // m) * m
        odd = ((t // m) % 2) == 1
        blocks.append(np.where(odd, (j >= seg0) & (j <= t), (j > t) & (j <= seg0 + m - 1)))
    mf = np.concatenate(blocks, axis=0).astype(np.float32)
    x = t ^ j
    lv = np.where(x == 0, REC_LEVELS, np.floor(np.log2(np.maximum(x, 1))).astype(np.int64))
    lvf = np.where(j <= t, lv, REC_LEVELS + 1).astype(np.int32)
    mb = mf.reshape(-1, C, C)[:, ::-1, ::-1].reshape(-1, C)
    lvb = lvf[::-1, ::-1]
    mf, mb = np.concatenate([mf, mf], axis=1), np.concatenate([mb, mb], axis=1)
    return (jnp.asarray(mf, BF16), jnp.asarray(np.ascontiguousarray(mb), BF16),
            jnp.asarray(lvf), jnp.asarray(np.ascontiguousarray(lvb)))


def _rec_pair(rq, ri, fx, lb, m_ref, lv, sts, backward):
    C = REC_C
    e = jnp.exp(-jnp.abs(fx))
    r = 1.0 / (1.0 + e)
    pos = fx >= 0
    sig = jnp.where(pos, r, e * r)
    nsig = jnp.where(pos, e * r, r)
    logf = jnp.log(lb + (1.0 - lb) * sig)
    kk = (1.0 - lb) * nsig
    q = rq * _sigmoid(rq)
    hi = logf.astype(BF16)
    mid = (logf - hi.astype(F32)).astype(BF16)
    lf2 = jnp.concatenate([hi, mid], axis=0)

    def expo(blk):
        return _dot(m_ref[blk * C:(blk + 1) * C, :], lf2)

    row = lax.broadcasted_iota(jnp.int32, (C, 2 * HEAD_W), 0)
    if backward:
        row = (C - 1) - row
    qb = q.astype(BF16)
    kb = kk.astype(BF16)
    ws = []
    for l in range(REC_LEVELS):
        odd = ((row >> l) & 1) == 1
        ws.append((jnp.exp(expo(2 + l)) * jnp.where(odd, q, kk)).astype(BF16))
    b_incl = expo(0)
    qi = (q * jnp.exp(b_incl)).astype(BF16)
    ki = (kk * jnp.exp(expo(1))).astype(BF16)
    dec = jnp.exp(b_incl[0:1, :] if backward else b_incl[C - 1:C, :])
    outs, new_sts = [], []
    for g in range(2):
        ls = slice(g * HEAD_W, (g + 1) * HEAD_W)
        a = jnp.where(lv == REC_LEVELS, _dot_nt(qb[:, ls], kb[:, ls]), 0.0)
        for l in range(REC_LEVELS):
            a = jnp.where(lv == l, _dot_nt(ws[l][:, ls], ws[l][:, ls]), a)
        rig = ri[:, ls]
        outs.append(_dot(a.astype(BF16), rig.astype(BF16)) + _dot_nt(qi[:, ls], sts[g].astype(BF16)))
        new_sts.append(sts[g] * dec[:, ls] + _dot(rig.T.astype(BF16), ki[:, ls]))
    return jnp.concatenate(outs, axis=1), new_sts


def _hgrn_body(mf_ref, mb_ref, lvf_ref, lvb_ref, rqf_ref, rif_ref, ff_ref, rqb_ref, rib_ref, fb_ref, lb_ref,
               *refs, has_s0):
    if has_s0:
        s0_ref, of_ref, ob_ref, so_ref, sf_scr, sb_scr = refs
    else:
        of_ref, ob_ref, so_ref, sf_scr, sb_scr = refs
    c = pl.program_id(2)

    @pl.when(c == 0)
    def _():
        for g in range(REC_HEADS_PER_STEP):
            if has_s0:
                sf_scr[g] = s0_ref[0, 0, g].T
                sb_scr[g] = s0_ref[0, 1, g].T
            else:
                sf_scr[g] = jnp.zeros((HEAD_W, HEAD_W), F32)
                sb_scr[g] = jnp.zeros((HEAD_W, HEAD_W), F32)

    for g in range(0, REC_HEADS_PER_STEP, 2):
        ls = slice(g * HEAD_W, (g + 2) * HEAD_W)
        o, st = _rec_pair(rqf_ref[0, :, ls], rif_ref[0, :, ls], ff_ref[0, :, ls], lb_ref[0:1, ls], mf_ref,
                          lvf_ref[...], [sf_scr[g], sf_scr[g + 1]], False)
        of_ref[0, :, ls] = o
        sf_scr[g] = st[0]
        sf_scr[g + 1] = st[1]
        o, st = _rec_pair(rqb_ref[0, :, ls], rib_ref[0, :, ls], fb_ref[0, :, ls], lb_ref[1:2, ls], mb_ref,
                          lvb_ref[...], [sb_scr[g], sb_scr[g + 1]], True)
        ob_ref[0, :, ls] = o
        sb_scr[g] = st[0]
        sb_scr[g + 1] = st[1]

    @pl.when(c == pl.num_programs(2) - 1)
    def _():
        for g in range(REC_HEADS_PER_STEP):
            so_ref[0, 0, g] = sf_scr[g].T
            so_ref[0, 1, g] = sb_scr[g].T


def _hgrn(z, lb, s0, layer, consts):
    B, T, _ = z.shape
    n = T // REC_C
    C = REC_C
    G = REC_HEADS_PER_STEP
    ng = N_HEADS // G
    W = G * HEAD_W

    def fwd(col):
        return pl.BlockSpec((1, C, W), lambda b, h, c: (b, c, col * ng + h))

    def bwd(col):
        return pl.BlockSpec((1, C, W), lambda b, h, c: (b, n - 1 - c, col * ng + h))

    const2 = lambda b, h, c: (0, 0)
    in_specs = [pl.BlockSpec(((2 + REC_LEVELS) * C, 2 * C), const2), pl.BlockSpec(((2 + REC_LEVELS) * C, 2 * C), const2),
                pl.BlockSpec((C, C), const2), pl.BlockSpec((C, C), const2),
                fwd(COL_RQ), fwd(COL_RI), fwd(COL_RFF), bwd(COL_RQ), bwd(COL_RI), bwd(COL_RFB),
                pl.BlockSpec((2, W), lambda b, h, c: (0, h))]
    args = list(consts) + [z] * 6 + [lb]
    if s0 is not None:
        s0v = s0.reshape(B, DEPTH * 2, N_HEADS, HEAD_W, HEAD_W)
        in_specs.append(pl.BlockSpec((1, 2, G, HEAD_W, HEAD_W), lambda b, h, c: (b, layer, h, 0, 0)))
        args.append(s0v)
    return pl.pallas_call(
        functools.partial(_hgrn_body, has_s0=s0 is not None),
        grid=(B, ng, n),
        in_specs=in_specs,
        out_specs=[pl.BlockSpec((1, C, W), lambda b, h, c: (b, c, h)),
                   pl.BlockSpec((1, C, W), lambda b, h, c: (b, n - 1 - c, h)),
                   pl.BlockSpec((1, 2, G, HEAD_W, HEAD_W), lambda b, h, c: (b, 0, h, 0, 0))],
        out_shape=[jax.ShapeDtypeStruct((B, T, D_MODEL), F32), jax.ShapeDtypeStruct((B, T, D_MODEL), F32),
                   jax.ShapeDtypeStruct((B, 2, N_HEADS, HEAD_W, HEAD_W), F32)],
        scratch_shapes=[pltpu.VMEM((G, HEAD_W, HEAD_W), F32), pltpu.VMEM((G, HEAD_W, HEAD_W), F32)],
        compiler_params=_cparams(("parallel", "parallel", "arbitrary")),
        name="hgrn_ctx" if s0 is not None else "hgrn",
    )(*args)


def _merge_body(oatt_ref, zb_ref, zc_ref, zx_ref, zcp_ref, zxp_ref, zcn_ref, zxn_ref, of_ref, ob_ref, rg_ref,
                g0_ref, g1_ref, g2_ref, x_ref, mod_ref, wb_ref, wo_ref, cw_ref, cb_ref, rn_ref, ln_ref,
                x1_ref, h2_ref, *, tm):
    i = pl.program_id(1)
    u = zc_ref[0] * zx_ref[0]
    row = lax.broadcasted_iota(jnp.int32, (tm, D_MODEL), 0)
    prev_ok = (i > 0).astype(F32)
    next_ok = (i < pl.num_programs(1) - 1).astype(F32)
    u_prev_edge = zcp_ref[0, 7:8, :] * zxp_ref[0, 7:8, :] * prev_ok
    u_next_edge = zcn_ref[0, 0:1, :] * zxn_ref[0, 0:1, :] * next_ok
    up = jnp.where(row == 0, u_prev_edge, pltpu.roll(u, 1, 0))
    un = jnp.where(row == tm - 1, u_next_edge, pltpu.roll(u, tm - 1, 0))
    conv = up * cw_ref[0:1, :] + u * cw_ref[1:2, :] + un * cw_ref[2:3, :] + cb_ref[...]
    o_conv = (zb_ref[0] * conv).astype(BF16)
    s = of_ref[0] + ob_ref[0]
    rg = rg_ref[0]
    parts = []
    for h in range(N_HEADS):
        sh = s[:, h * HEAD_W:(h + 1) * HEAD_W]
        parts.append(sh * lax.rsqrt(jnp.mean(sh * sh, axis=-1, keepdims=True) + LN_EPS))
    o_rec = (jnp.concatenate(parts, axis=1) * rn_ref[...] * (rg * _sigmoid(rg))).astype(BF16)
    merged = (_sigmoid(g0_ref[0]) * _dot(oatt_ref[0], wb_ref[0])
              + _sigmoid(g1_ref[0]) * _dot(o_conv, wb_ref[1])
              + _sigmoid(g2_ref[0]) * _dot(o_rec, wb_ref[2]))
    mix = _dot(merged.astype(BF16), wo_ref[...])
    y = DEEPNORM_ALPHA * x_ref[0] + mod_ref[0, 2:3, :] * mix
    x1 = _layer_norm(y, ln_ref[0:1, :], ln_ref[1:2, :])
    x1_ref[0] = x1
    h2_ref[0] = (x1 * (1.0 + mod_ref[0, 4:5, :]) + mod_ref[0, 3:4, :]).astype(BF16)


def _merge(x, z, oatt, o_f, o_b, mod, wb_bf16, wo_bf16, conv_w, conv_b, rec_norm_t, ln_gb, tm):
    B, T, _ = x.shape
    per_b = mod.shape[0] > 1
    nb8 = tm // 8
    last8 = T // 8 - 1

    def col(c):
        return pl.BlockSpec((1, tm, D_MODEL), lambda b, i: (b, i, c))

    def prev8(c):
        return pl.BlockSpec((1, 8, D_MODEL), lambda b, i: (b, jnp.maximum(i * nb8 - 1, 0), c))

    def next8(c):
        return pl.BlockSpec((1, 8, D_MODEL), lambda b, i: (b, jnp.minimum((i + 1) * nb8, last8), c))

    tile = pl.BlockSpec((1, tm, D_MODEL), lambda b, i: (b, i, 0))
    full2 = lambda b, i: (0, 0)
    in_specs = [tile, col(COL_CB), col(COL_CC), col(COL_CX), prev8(COL_CC), prev8(COL_CX), next8(COL_CC), next8(COL_CX),
                tile, tile, col(COL_RG), col(COL_G), col(COL_G + 1), col(COL_G + 2), tile,
                pl.BlockSpec((1, 6, D_MODEL), (lambda b, i: (b, 0, 0)) if per_b else (lambda b, i: (0, 0, 0))),
                pl.BlockSpec((3, D_MODEL, D_MODEL), lambda b, i: (0, 0, 0)),
                pl.BlockSpec((D_MODEL, D_MODEL), full2),
                pl.BlockSpec((3, D_MODEL), full2), pl.BlockSpec((1, D_MODEL), full2),
                pl.BlockSpec((1, D_MODEL), full2), pl.BlockSpec((2, D_MODEL), full2)]
    return pl.pallas_call(
        functools.partial(_merge_body, tm=tm),
        grid=(B, T // tm),
        in_specs=in_specs,
        out_specs=[tile, tile],
        out_shape=[jax.ShapeDtypeStruct((B, T, D_MODEL), F32), jax.ShapeDtypeStruct((B, T, D_MODEL), BF16)],
        compiler_params=_cparams(("parallel", "arbitrary")),
        name="merge",
    )(oatt, z, z, z, z, z, z, z, o_f, o_b, z, z, z, z, x, mod, wb_bf16, wo_bf16, conv_w, conv_b, rec_norm_t, ln_gb)


def _top_rows(s, k, one_per_round, want_rank=True):
    n, w = s.shape
    s_in = s
    track = one_per_round or want_rank
    rid = lax.broadcasted_iota(jnp.int32, (n, w), 0).astype(F32)
    rank = jnp.full((n, w), float(k), F32) if track else None
    vals = []
    for r in range(k):
        m = jnp.max(s, axis=0, keepdims=True)
        hit = s == m
        if one_per_round:
            first = jnp.min(jnp.where(hit, rid, float(n)), axis=0, keepdims=True)
            hit = rid == first
        s = jnp.where(hit, -jnp.inf, s)
        if track:
            rank = jnp.where(hit, float(r), rank)
        vals.append(m)
    taken = (rank < k) if track else (s_in >= vals[-1])
    excess = jnp.sum(jnp.where(taken, 1.0, 0.0), axis=0, keepdims=True) - k
    return vals, rank, excess


def _route_body(h_ref, wq_ref, keys_ref, n1_ref, e1_ref, r2_ref, e2_ref, q_scr, s_scr, *, tt):
    q_scr[...] = _dot(h_ref[0], wq_ref[...]).astype(BF16)

    def head(h, carry):
        for p in range(2):
            off = pl.multiple_of((2 * h + p) * HEAD_W, HEAD_W)
            s_scr[p] = _dot_nt(keys_ref[2 * h + p], q_scr[:, pl.ds(off, HEAD_W)])
        excess = route_head(h, False)

        @pl.when(jnp.max(excess) > 0.0)
        def _():
            route_head(h, True)

        return carry

    def route_head(h, one_per_round):
        excess = jnp.zeros((1, HEAD_W), F32)
        for lc in range(tt // HEAD_W):
            ls = slice(lc * HEAD_W, (lc + 1) * HEAD_W)
            s1 = s_scr[0, :, ls]
            s2 = s_scr[1, :, ls]
            v1, r1, x1 = _top_rows(s1, PEER_TOPK, one_per_round, want_rank=False)
            v2, r2, x2 = _top_rows(s2, PEER_TOPK, one_per_round)
            v2a = jnp.concatenate(v2, axis=0)
            cands = [v1[0] + v2a] + [v1[a] + v2a[:8] for a in range(1, 8)] + [jnp.concatenate(v1[8:], axis=0) + v2[0]]
            cand = jnp.concatenate(cands, axis=0)
            vc, rc, xc = _top_rows(cand, PEER_TOPK, one_per_round, want_rank=False)
            excess = jnp.maximum(excess, jnp.maximum(jnp.maximum(x1, x2), xc))
            sel = (rc < PEER_TOPK) if one_per_round else (cand >= vc[-1])
            zsum = jnp.sum(jnp.where(sel, jnp.exp(cand - (v1[0] + v2[0])), 0.0), axis=0, keepdims=True)
            selc = jnp.where(sel, 1.0, 0.0)
            n_a = [jnp.sum(selc[0:16], axis=0, keepdims=True)]
            n_a += [jnp.sum(selc[8 + 8 * a:16 + 8 * a], axis=0, keepdims=True) for a in range(1, 8)]
            n_a += [selc[72 + a:73 + a] for a in range(8)]
            n1 = jnp.zeros_like(s1)
            for a in range(PEER_TOPK):
                n1 = jnp.where((r1 == a) if one_per_round else (s1 == v1[a]), n_a[a], n1)
            n1_ref[h, :, ls] = n1
            e1_ref[h, :, ls] = jnp.exp(s1 - v1[0]) * (0.5 / zsum)
            r2_ref[h, :, ls] = r2.astype(BF16)
            e2_ref[h, :, ls] = jnp.exp(s2 - v2[0]).astype(BF16)
        return excess

    lax.fori_loop(0, N_HEADS, head, 0)


def _peer_route(h2, wq_bf16, keys_bf16, tt):
    B, T, _ = h2.shape
    big = pl.BlockSpec((None, N_HEADS, PEER_N_KEYS, tt), lambda b, i: (b, 0, 0, i))
    f32_shape = jax.ShapeDtypeStruct((B, N_HEADS, PEER_N_KEYS, T), F32)
    bf16_shape = jax.ShapeDtypeStruct((B, N_HEADS, PEER_N_KEYS, T), BF16)
    return pl.pallas_call(
        functools.partial(_route_body, tt=tt),
        grid=(B, T // tt),
        in_specs=[pl.BlockSpec((1, tt, D_MODEL), lambda b, i: (b, i, 0)),
                  pl.BlockSpec((D_MODEL, 2 * N_HEADS * HEAD_W), lambda b, i: (0, 0)),
                  pl.BlockSpec((2 * N_HEADS, PEER_N_KEYS, HEAD_W), lambda b, i: (0, 0, 0))],
        out_specs=[big, big, big, big],
        out_shape=[f32_shape, f32_shape, bf16_shape, bf16_shape],
        scratch_shapes=[pltpu.VMEM((tt, 2 * N_HEADS * HEAD_W), BF16), pltpu.VMEM((2, PEER_N_KEYS, tt), F32)],
        compiler_params=_cparams(("parallel", "parallel")),
        name="peer_route",
    )(h2, wq_bf16, keys_bf16)


PEER_STEP_KEYS = 16
PEER_SUB_KEYS = 4


def _dense_body(h_ref, u_ref, vt_ref, n1_ref, e1_ref, r2_ref, e2_ref, x_ref, mod_ref, ln_ref, o_ref,
                act_scr, g_scr, w_scr, acc_scr, r2_scr, e2_scr, *, tt):
    k = pl.program_id(2)

    @pl.when(k == 0)
    def _():
        acc_scr[...] = jnp.zeros_like(acc_scr)
        r2_scr[...] = r2_ref[...]
        e2_scr[...] = e2_ref[...]

    hb = h_ref[0]
    sub = PEER_SUB_KEYS * PEER_N_KEYS

    def routing_weights(jj):
        for j in range(PEER_SUB_KEYS * jj, PEER_SUB_KEYS * (jj + 1)):
            for lc in range(tt // HEAD_W):
                ls = slice(lc * HEAD_W, (lc + 1) * HEAD_W)
                g = None
                for h in range(N_HEADS):
                    n_t = jnp.broadcast_to(n1_ref[h, j:j + 1, ls], (16, HEAD_W)).astype(BF16)[None]
                    e_t = jnp.broadcast_to(e1_ref[h, j:j + 1, ls], (16, HEAD_W)).astype(BF16)[None]
                    hit = r2_scr[h, :, ls].reshape(PEER_N_KEYS // 16, 16, HEAD_W) < n_t
                    term = jnp.where(hit, e2_scr[h, :, ls].reshape(PEER_N_KEYS // 16, 16, HEAD_W), 0.0) * e_t
                    g = term if g is None else g + term
                g_scr[j * PEER_N_KEYS:(j + 1) * PEER_N_KEYS, ls] = g.reshape(PEER_N_KEYS, HEAD_W)

    nsub = PEER_STEP_KEYS // PEER_SUB_KEYS
    half = sub * nsub // 2
    act_scr[0:half, :] = _dot_nt(u_ref[0:half, :], hb)
    act_scr[half:, :] = _dot_nt(u_ref[half:, :], hb)
    for jj in range(nsub):
        routing_weights(jj)
    for jj in range(nsub):
        rows = slice(jj * sub, (jj + 1) * sub)
        a = act_scr[rows, :]
        gelu2 = a * (1.0 + lax.erf(a * (2.0 ** -0.5)))
        w_scr[rows, :] = gelu2.astype(BF16) * g_scr[rows, :]
        acc_scr[...] += _dot(vt_ref[0, :, rows], w_scr[rows, :])

    @pl.when(k == pl.num_programs(2) - 1)
    def _():
        y = DEEPNORM_ALPHA * x_ref[0] + mod_ref[0, 5:6, :] * acc_scr[...].T
        o_ref[0] = _layer_norm(y, ln_ref[0:1, :], ln_ref[1:2, :])


def _peer_dense(h2, x1, mod, u_bf16, vt_bf16, route, ln_gb, tt):
    B, T, _ = h2.shape
    n1, e1, r2, e2 = route
    per_b = mod.shape[0] > 1
    ne = PEER_STEP_KEYS * PEER_N_KEYS
    rows = pl.BlockSpec((None, N_HEADS, PEER_STEP_KEYS, tt), lambda b, i, k: (b, 0, k, i))
    full = pl.BlockSpec((None, N_HEADS, PEER_N_KEYS, tt), lambda b, i, k: (b, 0, 0, i))
    return pl.pallas_call(
        functools.partial(_dense_body, tt=tt),
        grid=(B, T // tt, PEER_N_KEYS // PEER_STEP_KEYS),
        in_specs=[pl.BlockSpec((1, tt, D_MODEL), lambda b, i, k: (b, i, 0)),
                  pl.BlockSpec((ne, D_MODEL), lambda b, i, k: (k, 0)),
                  pl.BlockSpec((1, D_MODEL, ne), lambda b, i, k: (k, 0, 0)),
                  rows, rows, full, full,
                  pl.BlockSpec((1, tt, D_MODEL), lambda b, i, k: (b, i, 0)),
                  pl.BlockSpec((1, 6, D_MODEL), (lambda b, i, k: (b, 0, 0)) if per_b else (lambda b, i, k: (0, 0, 0))),
                  pl.BlockSpec((2, D_MODEL), lambda b, i, k: (0, 0))],
        out_specs=pl.BlockSpec((1, tt, D_MODEL), lambda b, i, k: (b, i, 0)),
        out_shape=jax.ShapeDtypeStruct((B, T, D_MODEL), F32),
        scratch_shapes=[pltpu.VMEM((ne, tt), F32), pltpu.VMEM((ne, tt), BF16), pltpu.VMEM((ne, tt), BF16),
                        pltpu.VMEM((D_MODEL, tt), F32),
                        pltpu.VMEM((N_HEADS, PEER_N_KEYS, tt), BF16), pltpu.VMEM((N_HEADS, PEER_N_KEYS, tt), BF16)],
        compiler_params=_cparams(("parallel", "parallel", "arbitrary")),
        name="peer_dense",
    )(h2, u_bf16, vt_bf16, n1, e1, r2, e2, x1, mod, ln_gb)


def _trunk_layer(x, mod, layer, p, ctx, rope_tabs, consts, flat_rows):
    B, T, _ = x.shape
    if flat_rows is not None:
        z = _in_proj(x.reshape(-1, flat_rows, D_MODEL), mod, p["w_in"], None, tm=flat_rows)
        z = z.reshape(B, T, W_IN_COLS)
    else:
        z = _in_proj(x, mod, p["w_in"], rope_tabs, tm=PROJ_ROWS)
    if ctx is None:
        oatt = _attention(z, p["lam"], p["subln"], None, layer, p["att_scale"], tq=min(ATT_Q_ROWS, T), tk=T)
        o_f, o_b, s_fin = _hgrn(z, p["lb"], None, layer, consts)
    else:
        oatt = _attention(z, p["lam"], p["subln"], (ctx[0], ctx[1]), layer, p["att_scale"], tq=ATT_Q_ROWS,
                          tk=ATT_KEY_CHUNK)
        o_f, o_b, s_fin = _hgrn(z, p["lb"], ctx[2], layer, consts)
    x1, h2 = _merge(x, z, oatt, o_f, o_b, mod, p["w_branch"], p["w_out"], p["conv_w"], p["conv_b"],
                    p["rec_norm"], p["ln0"], tm=MERGE_ROWS)
    if flat_rows is not None:
        h2r, x1r = h2.reshape(-1, flat_rows, D_MODEL), x1.reshape(-1, flat_rows, D_MODEL)
    else:
        h2r, x1r = h2, x1
    route = _peer_route(h2r, p["peer_wq"], p["peer_keys"], PEER_TOKENS)
    x2 = _peer_dense(h2r, x1r, mod, p["peer_u"], p["peer_vt"], route, p["ln1"], PEER_TOKENS).reshape(B, T, D_MODEL)
    return x2, z, s_fin


def _layer_params(l, lb_all, w_in, attn_lambda, attn_subln, conv_w, conv_b, rec_norm, w_branch, w_out, ln_g, ln_b,
                  peer_wq, peer_keys, peer_u, peer_v):
    lam_init = 0.8 - 0.6 * math.exp(-0.3 * l)
    lp = attn_lambda[l].astype(F32)
    lam = jnp.exp(jnp.sum(lp[0] * lp[1])) - jnp.exp(jnp.sum(lp[2] * lp[3])) + lam_init
    return {
        "w_in": w_in[l].astype(BF16), "lam": lam.reshape(1), "att_scale": 1.0 - lam_init,
        "subln": attn_subln[l].reshape(1, HEAD_W), "lb": lb_all[l],
        "w_branch": w_branch[l].astype(BF16), "w_out": w_out[l].astype(BF16),
        "conv_w": conv_w[l], "conv_b": conv_b[l].reshape(1, D_MODEL),
        "rec_norm": jnp.tile(rec_norm[l], N_HEADS).reshape(1, D_MODEL),
        "ln0": jnp.stack([ln_g[l, 0], ln_b[l, 0]]), "ln1": jnp.stack([ln_g[l, 1], ln_b[l, 1]]),
        "peer_wq": peer_wq[l].astype(BF16),
        "peer_keys": peer_keys[l].astype(BF16).reshape(2 * N_HEADS, PEER_N_KEYS, HEAD_W),
        "peer_u": peer_u[l].astype(BF16),
        "peer_vt": peer_v[l].astype(BF16).reshape(-1, PEER_STEP_KEYS * PEER_N_KEYS, D_MODEL).transpose(0, 2, 1),
    }


def kernel(x_prompt, x_sample, c, cache_attn_k, cache_attn_v, state_hgrn, c_ctx, mod_w, mod_b, w_in, attn_lambda,
           attn_subln, conv_w, conv_b, rec_lb, rec_norm, w_branch, w_out, ln_g, ln_b, peer_wq, peer_keys, peer_u,
           peer_v):
    B, T, _ = x_prompt.shape
    Bs, Ts, _ = x_sample.shape
    P = cache_attn_k.shape[2]
    lb_all = jnp.cumsum(jax.nn.softmax(rec_lb.astype(F32), axis=0), axis=0)
    lb_all = lb_all - lb_all[:1]
    cmat = jnp.concatenate([c_ctx[None, :], c, jnp.zeros((8 - 1 - Bs, D_MODEL), F32)], axis=0)
    mods = _mod_vectors(cmat, mod_w, mod_b).reshape(DEPTH, 8, 6, D_MODEL)
    rope_tabs = _rope_tables(Ts)
    consts = _rec_constants()
    ck = cache_attn_k.reshape(Bs, DEPTH, P, N_HEADS * HEAD_W)
    cv = cache_attn_v.reshape(Bs, DEPTH, P, N_HEADS * HEAD_W)

    y_p, y_s = x_prompt, x_sample
    ks, vs, ss = [], [], []
    for l in range(DEPTH):
        p = _layer_params(l, lb_all, w_in, attn_lambda, attn_subln, conv_w, conv_b, rec_norm, w_branch, w_out,
                          ln_g, ln_b, peer_wq, peer_keys, peer_u, peer_v)
        y_p, z_p, s_p = _trunk_layer(y_p, mods[l, 0:1], l, p, None, None, consts, flat_rows=PROJ_ROWS)
        ks.append(z_p[..., COL_K * D_MODEL:(COL_K + 1) * D_MODEL].reshape(B, T, N_HEADS, HEAD_W))
        vs.append(z_p[..., COL_V * D_MODEL:(COL_V + 1) * D_MODEL].reshape(B, T, N_HEADS, HEAD_W))
        ss.append(s_p)
        y_s, _, _ = _trunk_layer(y_s, mods[l, 1:1 + Bs], l, p, (ck, cv, state_hgrn), rope_tabs, consts, flat_rows=None)
    return (y_p, y_s, jnp.stack(ks, axis=1), jnp.stack(vs, axis=1), jnp.stack(ss, axis=1))
```
